```python
import math
import jax
import jax.numpy as jnp
from jax import lax
import numpy as np

D_MODEL = 4096
BATCH = 2
SEQ = 4096
DEPTH = 2

HEAD_DIM = 128
D_MIX = D_MODEL
DN_WIDTH = 3 * D_MIX // 8
LRU_WIDTH = 3 * D_MIX // 8
SG_WIDTH = D_MIX - DN_WIDTH - LRU_WIDTH
DN_HEADS = DN_WIDTH // HEAD_DIM
LRU_BLOCKS = LRU_WIDTH // HEAD_DIM
SG_GROUPS = SG_WIDTH // HEAD_DIM
DN_CHUNK = 64
SG_CHUNK = 128
SHORT_CONV = 4
FFN_CONV = 3
D_FF = ((8 * D_MODEL // 3 + 255) // 256) * 256
LRU_C = 8.0
EPS = 1e-6
IN_SIZES = (DN_WIDTH, DN_WIDTH, DN_WIDTH, DN_WIDTH, DN_HEADS, DN_HEADS,
            LRU_WIDTH, LRU_WIDTH, SG_WIDTH, SG_WIDTH)
D_IN = sum(IN_SIZES)

kernel_name = "hybrid_deltanet_rglru_sgmlp_block"


def rms_norm(x, w):
    xf = x.astype(jnp.float32)
    y = xf * lax.rsqrt(jnp.mean(xf * xf, axis=-1, keepdims=True) + EPS)
    return (y * w.astype(jnp.float32)).astype(x.dtype)


def group_rms_norm(x, w, groups):
    b, s, c = x.shape
    xg = x.astype(jnp.float32).reshape(b, s, groups, c // groups)
    y = xg * lax.rsqrt(jnp.mean(xg * xg, axis=-1, keepdims=True) + EPS)
    return y.reshape(b, s, c) * w.astype(jnp.float32)


def causal_dwconv(x, w):
    k_w = w.shape[0]
    s = x.shape[1]
    xp = jnp.pad(x, ((0, 0), (k_w - 1, 0), (0, 0)))
    return sum(xp[:, j:j + s] * w[j] for j in range(k_w))


def chunk_heads(t):
    b, s, h = t.shape[:3]
    t = t.reshape(b, s // DN_CHUNK, DN_CHUNK, h, *t.shape[3:])
    return jnp.moveaxis(t, 3, 1)


def gated_delta_rule(q, k, v, g, beta):
    b, s, h, d = q.shape
    q, k, v = chunk_heads(q) * d ** -0.5, chunk_heads(k), chunk_heads(v)
    g, beta = chunk_heads(g), chunk_heads(beta)
    gc = jnp.cumsum(g, axis=-1)
    causal = jnp.tril(jnp.ones((DN_CHUNK, DN_CHUNK), dtype=bool))
    strict = jnp.tril(jnp.ones((DN_CHUNK, DN_CHUNK), dtype=bool), k=-1)
    decay = jnp.exp(jnp.where(causal, gc[..., :, None] - gc[..., None, :], -jnp.inf))
    k_beta = k * beta[..., None]
    a_low = jnp.where(strict, jnp.einsum('bhnid,bhnjd->bhnij', k_beta, k) * decay, 0.0)
    rhs = jnp.concatenate([v * beta[..., None], k_beta * jnp.exp(gc)[..., None]], axis=-1)
    sol = lax.linalg.triangular_solve(a_low, rhs, left_side=True, lower=True, unit_diagonal=True)
    u, w = sol[..., :d], sol[..., d:]
    attn = jnp.einsum('bhnid,bhnjd->bhnij', q, k) * decay
    q_dec = q * jnp.exp(gc)[..., None]
    k_dec = k * jnp.exp(gc[..., -1:] - gc)[..., None]
    chunk_decay = jnp.exp(gc[..., -1])

    def step(state, inp):
        u_n, w_n, q_n, k_n, a_n, d_n = inp
        v_new = u_n - jnp.einsum('bhcd,bhde->bhce', w_n, state)
        o_n = (jnp.einsum('bhcd,bhde->bhce', q_n, state)
               + jnp.einsum('bhcj,bhje->bhce', a_n, v_new))
        state = state * d_n[..., None, None] + jnp.einsum('bhcd,bhce->bhde', k_n, v_new)
        return state, o_n

    xs = tuple(jnp.moveaxis(t, 2, 0) for t in (u, w, q_dec, k_dec, attn, chunk_decay))
    state0 = jnp.zeros((b, h, d, d), jnp.float32)
    _, o = lax.scan(step, state0, xs)
    return jnp.transpose(o, (1, 0, 3, 2, 4)).reshape(b, s, h, d)


def deltanet_group(q, k, v, z, b_raw, a_raw, conv_w, a_log, dt_bias, norm_w):
    bsz, s = q.shape[:2]
    qkv = jax.nn.silu(causal_dwconv(jnp.concatenate([q, k, v], axis=-1), conv_w))
    qkv = qkv.astype(jnp.float32).reshape(bsz, s, 3, DN_HEADS, HEAD_DIM)
    q, k, v = qkv[:, :, 0], qkv[:, :, 1], qkv[:, :, 2]
    q = q * lax.rsqrt(jnp.sum(q * q, axis=-1, keepdims=True) + EPS)
    k = k * lax.rsqrt(jnp.sum(k * k, axis=-1, keepdims=True) + EPS)
    beta = jax.nn.sigmoid(b_raw.astype(jnp.float32))
    g = -jnp.exp(a_log.astype(jnp.float32)) * jax.nn.softplus(
        a_raw.astype(jnp.float32) + dt_bias.astype(jnp.float32))
    o = gated_delta_rule(q, k, v, g, beta)
    zg = jax.nn.silu(z.astype(jnp.float32).reshape(bsz, s, DN_HEADS, HEAD_DIM))
    o = o * lax.rsqrt(jnp.mean(o * o, axis=-1, keepdims=True) + EPS) * norm_w.astype(jnp.float32) * zg
    return o.reshape(bsz, s, DN_WIDTH)


def _linear_combine(left, right):
    a_l, b_l = left
    a_r, b_r = right
    return a_l * a_r, a_r * b_l + b_r


def rglru_group(xb, gate, conv_w, conv_b, w_a, b_a, w_x, b_x, lam, norm_w):
    bsz, s, _ = xb.shape
    xc = (causal_dwconv(xb, conv_w) + conv_b).astype(jnp.float32)
    xh = xc.reshape(bsz, s, LRU_BLOCKS, HEAD_DIM)
    r = jax.nn.sigmoid(jnp.einsum('bsgi,gij->bsgj', xh, w_a.astype(jnp.float32)).reshape(bsz, s, LRU_WIDTH)
                       + b_a.astype(jnp.float32))
    i_g = jax.nn.sigmoid(jnp.einsum('bsgi,gij->bsgj', xh, w_x.astype(jnp.float32)).reshape(bsz, s, LRU_WIDTH)
                         + b_x.astype(jnp.float32))
    log_a = -LRU_C * r * jax.nn.softplus(-lam.astype(jnp.float32))
    a = jnp.exp(log_a)
    inp = jnp.sqrt(-jnp.expm1(2.0 * log_a)) * (i_g * xc)
    _, h = lax.associative_scan(_linear_combine, (a, inp), axis=1)
    y = h * jax.nn.gelu(gate.astype(jnp.float32))
    return group_rms_norm(y, norm_w, LRU_BLOCKS)


def spatial_gating_group(u, v, ln_w, ln_b, w_s, b_s, norm_w):
    bsz, s, _ = u.shape
    u = jax.nn.gelu(u.astype(jnp.float32))
    v = jax.nn.gelu(v.astype(jnp.float32))
    mu = jnp.mean(v, axis=-1, keepdims=True)
    var = jnp.mean(jnp.square(v - mu), axis=-1, keepdims=True)
    v = (v - mu) * lax.rsqrt(var + EPS) * ln_w.astype(jnp.float32) + ln_b.astype(jnp.float32)
    v = v.reshape(bsz, s // SG_CHUNK, SG_CHUNK, SG_GROUPS, HEAD_DIM)
    w_causal = jnp.tril(w_s.astype(jnp.float32))
    z = jnp.einsum('gts,bnsgc->bntgc', w_causal, v) + b_s.astype(jnp.float32).T[:, :, None]
    y = u * z.reshape(bsz, s, SG_WIDTH)
    return group_rms_norm(y, norm_w, SG_GROUPS)


def setup_inputs(seed: int = 0) -> dict:
    key = jax.random.key(seed)
    ks = jax.random.split(key, 27)
    L = DEPTH

    def nrm(k, shape, scale):
        return jax.random.normal(k, shape, jnp.float32) * scale

    def gain(k, shape):
        return 1.0 + nrm(k, shape, 0.02)

    dt = jnp.exp(jax.random.uniform(ks[5], (L, DN_HEADS), jnp.float32, math.log(1e-3), math.log(1e-1)))
    a_target = jax.random.uniform(ks[13], (L, LRU_WIDTH), jnp.float32, 0.9, 0.999)
    sig = a_target ** (1.0 / LRU_C)
    return {
        "x": nrm(ks[0], (BATCH, SEQ, D_MODEL), 1.0),
        "norm_mix": gain(ks[1], (L, D_MODEL)),
        "w_in": nrm(ks[2], (L, D_MODEL, D_IN), D_MODEL ** -0.5),
        "dn_conv_w": nrm(ks[3], (L, SHORT_CONV, 3 * DN_WIDTH), SHORT_CONV ** -0.5),
        "dn_a_log": jnp.log(jax.random.uniform(ks[4], (L, DN_HEADS), jnp.float32, 1.0, 16.0)),
        "dn_dt_bias": dt + jnp.log(-jnp.expm1(-dt)),
        "dn_norm_w": gain(ks[6], (L, HEAD_DIM)),
        "lru_conv_w": nrm(ks[7], (L, SHORT_CONV, LRU_WIDTH), SHORT_CONV ** -0.5),
        "lru_conv_b": nrm(ks[8], (L, LRU_WIDTH), 0.02),
        "lru_w_a": nrm(ks[9], (L, LRU_BLOCKS, HEAD_DIM, HEAD_DIM), HEAD_DIM ** -0.5),
        "lru_b_a": nrm(ks[10], (L, LRU_WIDTH), 0.02),
        "lru_w_x": nrm(ks[11], (L, LRU_BLOCKS, HEAD_DIM, HEAD_DIM), HEAD_DIM ** -0.5),
        "lru_b_x": nrm(ks[12], (L, LRU_WIDTH), 0.02),
        "lru_lambda": jnp.log(sig) - jnp.log1p(-sig),
        "lru_norm_w": gain(ks[14], (L, LRU_WIDTH)),
        "sg_ln_w": gain(ks[15], (L, SG_WIDTH)),
        "sg_ln_b": nrm(ks[16], (L, SG_WIDTH), 0.02),
        "sg_w_s": nrm(ks[17], (L, SG_GROUPS, SG_CHUNK, SG_CHUNK), SG_CHUNK ** -0.5),
        "sg_b_s": gain(ks[18], (L, SG_GROUPS, SG_CHUNK)),
        "sg_norm_w": gain(ks[19], (L, SG_WIDTH)),
        "w_out": nrm(ks[20], (L, D_MIX, D_MODEL), D_MIX ** -0.5),
        "norm_ffn": gain(ks[21], (L, D_MODEL)),
        "w_up": nrm(ks[22], (L, D_MODEL, 2 * D_FF), D_MODEL ** -0.5),
        "ffn_conv_w": nrm(ks[23], (L, FFN_CONV, 2 * D_FF), FFN_CONV ** -0.5),
        "ffn_conv_b": nrm(ks[24], (L, 2 * D_FF), 0.02),
        "w_down": nrm(ks[25], (L, D_FF, D_MODEL), D_FF ** -0.5),
        "norm_final": gain(ks[26], (D_MODEL,)),
    }


def reference(x, norm_mix, w_in, dn_conv_w, dn_a_log, dn_dt_bias, dn_norm_w,
              lru_conv_w, lru_conv_b, lru_w_a, lru_b_a, lru_w_x, lru_b_x, lru_lambda, lru_norm_w,
              sg_ln_w, sg_ln_b, sg_w_s, sg_b_s, sg_norm_w, w_out,
              norm_ffn, w_up, ffn_conv_w, ffn_conv_b, w_down, norm_final):
    split_idx = [int(i) for i in np.cumsum(IN_SIZES)[:-1]]
    for l in range(DEPTH):
        h = rms_norm(x, norm_mix[l])
        proj = h @ w_in[l]
        q, k, v, z, b_raw, a_raw, lx, lg, su, sv = jnp.split(proj, split_idx, axis=-1)
        y_a = deltanet_group(q, k, v, z, b_raw, a_raw, dn_conv_w[l], dn_a_log[l], dn_dt_bias[l], dn_norm_w[l])
        y_b = rglru_group(lx, lg, lru_conv_w[l], lru_conv_b[l], lru_w_a[l], lru_b_a[l],
                          lru_w_x[l], lru_b_x[l], lru_lambda[l], lru_norm_w[l])
        y_c = spatial_gating_group(su, sv, sg_ln_w[l], sg_ln_b[l], sg_w_s[l], sg_b_s[l], sg_norm_w[l])
        mix = jnp.concatenate([y_a, y_b, y_c], axis=-1).astype(x.dtype)
        x = x + mix @ w_out[l]
        h = rms_norm(x, norm_ffn[l])
        hid = causal_dwconv(h @ w_up[l], ffn_conv_w[l]) + ffn_conv_b[l]
        gate, up = jnp.split(hid, 2, axis=-1)
        x = x + (jax.nn.silu(gate) * up) @ w_down[l]
    return rms_norm(x, norm_final)
```

```python
import functools

import jax
import jax.numpy as jnp
from jax import lax
from jax.experimental import pallas as pl
from jax.experimental.pallas import tpu as pltpu

F32 = jnp.float32
BF16 = jnp.bfloat16
HIGHEST = lax.Precision.HIGHEST

EPS = 1e-6
LANE = 128
HALO = 8
HEAD_DIM = LANE
DN_CHUNK = 64
SG_CHUNK = 128
SCAN_CHUNK = 64
SHORT_CONV = 4
LRU_C = 8.0
VMEM_LIMIT_BYTES = 56 * 1024 * 1024


def _params(*semantics):
    return pltpu.CompilerParams(dimension_semantics=semantics,
                                vmem_limit_bytes=VMEM_LIMIT_BYTES)


def _largest_tile(n, cap, unit):
    best = None
    t = unit
    while t <= min(n, cap):
        if n % t == 0:
            best = t
        t += unit
    assert best is not None, (n, cap, unit)
    return best


def _dot(a, b):
    return jnp.dot(a.astype(BF16), b.astype(BF16), preferred_element_type=F32)


def _dot_nt(a, b):
    return lax.dot_general(a.astype(BF16), b.astype(BF16), (((1,), (1,)), ((), ())),
                           preferred_element_type=F32)


def _dot_tn(a, b):
    return lax.dot_general(a.astype(BF16), b.astype(BF16), (((0,), (0,)), ((), ())),
                           preferred_element_type=F32)


def _softplus(x):
    return jnp.maximum(x, 0.0) + jnp.log1p(jnp.exp(-jnp.abs(x)))


def _silu(x):
    return x * jax.nn.sigmoid(x)


def _iota(shape, dim):
    return lax.broadcasted_iota(jnp.int32, shape, dim)


def _rmsnorm_kernel(x_ref, w_ref, o_ref):
    x = x_ref[...]
    ms = jnp.mean(x * x, axis=-1, keepdims=True)
    o_ref[...] = (x * lax.rsqrt(ms + EPS) * w_ref[...]).astype(o_ref.dtype)


def _rmsnorm(x2, w, out_dtype):
    m, d = x2.shape
    tm = _largest_tile(m, 256, 16)
    return pl.pallas_call(
        _rmsnorm_kernel,
        grid=(m // tm,),
        in_specs=[pl.BlockSpec((tm, d), lambda i: (i, 0)),
                  pl.BlockSpec((1, d), lambda i: (0, 0))],
        out_specs=pl.BlockSpec((tm, d), lambda i: (i, 0)),
        out_shape=jax.ShapeDtypeStruct((m, d), out_dtype),
        compiler_params=_params("parallel"),
        name="rmsnorm",
    )(x2, w.reshape(1, d))


def _matmul_kernel(*refs, nk, has_res):
    if has_res:
        a_ref, w_ref, res_ref, o_ref = refs[:4]
    else:
        a_ref, w_ref, o_ref = refs[:3]
        res_ref = None
    part = jnp.dot(a_ref[...], w_ref[...], preferred_element_type=F32)

    def finish(acc):
        if res_ref is not None:
            acc = acc + res_ref[...]
        o_ref[...] = acc.astype(o_ref.dtype)

    if nk == 1:
        finish(part)
        return
    acc_ref = refs[-1]
    k = pl.program_id(2)

    @pl.when(k == 0)
    def _():
        acc_ref[...] = part

    @pl.when(k > 0)
    def _():
        acc_ref[...] += part

    @pl.when(k == nk - 1)
    def _():
        finish(acc_ref[...])


def _matmul(a, w, *, residual=None, out_dtype=F32, tm_cap=512, tn_cap=512, tk_cap=4096,
            name="matmul"):
    m, k = a.shape
    n = w.shape[1]
    tm = _largest_tile(m, tm_cap, 16)
    tn = _largest_tile(n, tn_cap, LANE)
    tk = _largest_tile(k, tk_cap, LANE)
    nk = k // tk
    in_specs = [pl.BlockSpec((tm, tk), lambda j, i, kk: (i, kk)),
                pl.BlockSpec((tk, tn), lambda j, i, kk: (kk, j))]
    args = [a, w]
    if residual is not None:
        in_specs.append(pl.BlockSpec((tm, tn), lambda j, i, kk: (i, j)))
        args.append(residual)
    scratch = [pltpu.VMEM((tm, tn), F32)] if nk > 1 else []
    return pl.pallas_call(
        functools.partial(_matmul_kernel, nk=nk, has_res=residual is not None),
        grid=(n // tn, m // tm, nk),
        in_specs=in_specs,
        out_specs=pl.BlockSpec((tm, tn), lambda j, i, kk: (i, j)),
        out_shape=jax.ShapeDtypeStruct((m, n), out_dtype),
        scratch_shapes=scratch,
        compiler_params=_params("parallel", "parallel", "arbitrary"),
        name=name,
    )(*args)


def _causal_conv_rows(h, buf, cw_ref, first, rows):
    taps = cw_ref.shape[0]

    @pl.when(first)
    def _():
        buf[0:HALO, :] = jnp.zeros((HALO, buf.shape[1]), F32)

    @pl.when(jnp.logical_not(first))
    def _():
        buf[0:HALO, :] = buf[rows:rows + HALO, :]

    buf[HALO:, :] = h
    acc = None
    for j in range(taps):
        term = cw_ref[j:j + 1, :] * buf[pl.ds(HALO - taps + 1 + j, rows), :]
        acc = term if acc is None else acc + term
    return acc


def _ffn_up_kernel(a_ref, wg_ref, wu_ref, cwg_ref, cwu_ref, bg_ref, bu_ref, o_ref,
                   gbuf, ubuf, *, tm, tiles_per_seq):
    first = (pl.program_id(1) % tiles_per_seq) == 0
    a = a_ref[...]
    g = jnp.dot(a, wg_ref[...], preferred_element_type=F32)
    u = jnp.dot(a, wu_ref[...], preferred_element_type=F32)
    g = _causal_conv_rows(g, gbuf, cwg_ref, first, tm) + bg_ref[...]
    u = _causal_conv_rows(u, ubuf, cwu_ref, first, tm) + bu_ref[...]
    o_ref[...] = (_silu(g) * u).astype(o_ref.dtype)


def _ffn_up(h, w_up, conv_w, conv_b, seq):
    m, k = h.shape
    f = w_up.shape[1] // 2
    taps = conv_w.shape[0]
    tm = _largest_tile(seq, 512, 16)
    tf = _largest_tile(f, 256, LANE)
    nf = f // tf
    conv_b = conv_b.reshape(1, 2 * f)
    return pl.pallas_call(
        functools.partial(_ffn_up_kernel, tm=tm, tiles_per_seq=seq // tm),
        grid=(nf, m // tm),
        in_specs=[pl.BlockSpec((tm, k), lambda j, i: (i, 0)),
                  pl.BlockSpec((k, tf), lambda j, i: (0, j)),
                  pl.BlockSpec((k, tf), lambda j, i: (0, nf + j)),
                  pl.BlockSpec((taps, tf), lambda j, i: (0, j)),
                  pl.BlockSpec((taps, tf), lambda j, i: (0, nf + j)),
                  pl.BlockSpec((1, tf), lambda j, i: (0, j)),
                  pl.BlockSpec((1, tf), lambda j, i: (0, nf + j))],
        out_specs=pl.BlockSpec((tm, tf), lambda j, i: (i, j)),
        out_shape=jax.ShapeDtypeStruct((m, f), BF16),
        scratch_shapes=[pltpu.VMEM((tm + HALO, tf), F32),
                        pltpu.VMEM((tm + HALO, tf), F32)],
        compiler_params=_params("parallel", "arbitrary"),
        name="ffn_up_conv_gate",
    )(h, w_up, w_up, conv_w, conv_w, conv_b, conv_b)


def _load_with_halo(xpad, idx, src_ref, is_first, rows):
    width = xpad.shape[-1]

    @pl.when(is_first)
    def _():
        xpad[idx, 0:HALO, :] = jnp.zeros((HALO, width), F32)

    @pl.when(jnp.logical_not(is_first))
    def _():
        xpad[idx, 0:HALO, :] = xpad[idx, rows:rows + HALO, :]

    xpad[idx, HALO:, :] = src_ref[...]


def _short_conv(xpad, idx, cw_ref, r0, rows):
    taps = cw_ref.shape[0]
    acc = None
    for j in range(taps):
        term = cw_ref[j:j + 1, :] * xpad[idx, pl.ds(r0 + HALO - taps + 1 + j, rows), :]
        acc = term if acc is None else acc + term
    return acc


def _unit_lower_inverse(a_low):
    c = a_low.shape[0]
    eye = (_iota((c, c), 0) == _iota((c, c), 1)).astype(F32)
    p = eye - a_low
    pw = a_low
    n = 2
    while n < c:
        pw = _dot(pw, pw)
        p = p + _dot(p, pw)
        n *= 2
    return p


def _deltanet_kernel(q_ref, k_ref, v_ref, z_ref, ba_ref, cwq_ref, cwk_ref, cwv_ref,
                     gp_ref, nw_ref, o_ref, xpad, qkv, gct, state, *, sc, hb, n_heads):
    c = DN_CHUNK
    hblk = pl.program_id(1)
    is_first = pl.program_id(2) == 0

    @pl.when(is_first)
    def _():
        state[...] = jnp.zeros(state.shape, F32)

    for idx, src in enumerate((q_ref, k_ref, v_ref)):
        _load_with_halo(xpad, idx, src, is_first, sc)

    for idx, cw_ref in enumerate((cwq_ref, cwk_ref, cwv_ref)):
        for ci in range(sc // c):
            act = _silu(_short_conv(xpad, idx, cw_ref, ci * c, c))
            for i in range(hb):
                seg = act[:, i * LANE:(i + 1) * LANE]
                if idx < 2:
                    seg = seg * lax.rsqrt(jnp.sum(seg * seg, axis=-1, keepdims=True) + EPS)
                qkv[idx, ci * c:(ci + 1) * c, i * LANE:(i + 1) * LANE] = seg

    row = _iota((c, c), 0)
    col = _iota((c, c), 1)
    causal = row >= col
    strict = row > col
    tril_ones = causal.astype(F32)
    lane = _iota((c, LANE), 1)
    scale = HEAD_DIM ** -0.5

    def chunk_body(ci, carry):
        r0 = pl.multiple_of(ci * c, c)
        ba = ba_ref[pl.ds(r0, c), :]
        beta_all = jax.nn.sigmoid(ba)
        g_all = -jnp.exp(gp_ref[0:1, :]) * _softplus(ba + gp_ref[1:2, :])
        gc_all = jnp.dot(tril_ones, g_all, precision=HIGHEST,
                         preferred_element_type=F32)
        gct[...] = jnp.concatenate([gc_all, jnp.zeros((LANE - c, LANE), F32)], axis=0).T
        for i in range(hb):
            head = hblk * hb + i
            cs = slice(i * LANE, (i + 1) * LANE)
            bcol = jnp.sum(jnp.where(lane == head, beta_all, 0.0), axis=-1, keepdims=True)
            gcol = jnp.sum(jnp.where(lane == head + n_heads, gc_all, 0.0), axis=-1,
                           keepdims=True)
            grow = gct[pl.ds(head + n_heads, 1), 0:c]
            glast = gcol[c - 1:c, :]
            diff = gcol - grow
            decay = jnp.where(causal, jnp.exp(jnp.where(causal, diff, 0.0)), 0.0)
            egc = jnp.exp(gcol)

            q = qkv[0, pl.ds(r0, c), cs] * scale
            k = qkv[1, pl.ds(r0, c), cs]
            v = qkv[2, pl.ds(r0, c), cs]
            kb = k * bcol
            a_low = jnp.where(strict, _dot_nt(kb, k) * decay, 0.0)
            t_inv = _unit_lower_inverse(a_low)
            u = _dot(t_inv, v * bcol)
            w = _dot(t_inv, kb * egc)
            attn = _dot_nt(q, k) * decay
            q_dec = q * egc
            k_dec = k * jnp.exp(glast - gcol)

            s_prev = state[i]
            v_new = u - _dot(w, s_prev)
            o = _dot(q_dec, s_prev) + _dot(attn, v_new)
            state[i] = s_prev * jnp.exp(glast) + _dot_tn(k_dec, v_new)

            zg = _silu(z_ref[pl.ds(r0, c), cs])
            o = o * lax.rsqrt(jnp.mean(o * o, axis=-1, keepdims=True) + EPS)
            o_ref[pl.ds(r0, c), cs] = (o * nw_ref[...] * zg).astype(o_ref.dtype)
        return carry

    lax.fori_loop(0, sc // c, chunk_body, 0)


def _deltanet(proj3, ba3, conv_w, a_log, dt_bias, norm_w, n_heads, col0):
    b, s, _ = proj3.shape
    hb = 4 if n_heads % 4 == 0 else (3 if n_heads % 3 == 0 else 1)
    width = hb * LANE
    nhb = n_heads // hb
    sc = _largest_tile(s, 512, DN_CHUNK)
    base = col0 // width
    assert col0 % width == 0
    gp = jnp.zeros((2, LANE), F32)
    gp = gp.at[0, n_heads:2 * n_heads].set(a_log).at[1, n_heads:2 * n_heads].set(dt_bias)

    def col_spec(group):
        return pl.BlockSpec((None, sc, width),
                            lambda bi, hi, si, g=group: (bi, si, base + g * nhb + hi))

    def cw_spec(group):
        return pl.BlockSpec((SHORT_CONV, width), lambda bi, hi, si, g=group: (0, g * nhb + hi))

    return pl.pallas_call(
        functools.partial(_deltanet_kernel, sc=sc, hb=hb, n_heads=n_heads),
        grid=(b, nhb, s // sc),
        in_specs=[col_spec(0), col_spec(1), col_spec(2), col_spec(3),
                  pl.BlockSpec((None, sc, LANE), lambda bi, hi, si: (bi, si, 0)),
                  cw_spec(0), cw_spec(1), cw_spec(2),
                  pl.BlockSpec((2, LANE), lambda bi, hi, si: (0, 0)),
                  pl.BlockSpec((1, LANE), lambda bi, hi, si: (0, 0))],
        out_specs=pl.BlockSpec((None, sc, width), lambda bi, hi, si: (bi, si, hi)),
        out_shape=jax.ShapeDtypeStruct((b, s, n_heads * LANE), BF16),
        scratch_shapes=[pltpu.VMEM((3, sc + HALO, width), F32),
                        pltpu.VMEM((3, sc, width), F32),
                        pltpu.VMEM((LANE, LANE), F32),
                        pltpu.VMEM((hb, HEAD_DIM, HEAD_DIM), F32)],
        compiler_params=_params("parallel", "parallel", "arbitrary"),
        name="gated_deltanet",
    )(proj3, proj3, proj3, proj3, ba3, conv_w, conv_w, conv_w, gp, norm_w.reshape(1, LANE))


def _rglru_kernel(x_ref, g_ref, cw_ref, cb_ref, wa_ref, ba_ref, wx_ref, bx_ref, lam_ref,
                  nw_ref, o_ref, xpad, abuf, bbuf, hcarry, *, sc, gb):
    c = SCAN_CHUNK
    is_first = pl.program_id(2) == 0

    @pl.when(is_first)
    def _():
        hcarry[...] = jnp.zeros(hcarry.shape, F32)

    _load_with_halo(xpad, 0, x_ref, is_first, sc)

    rows = 2 * c
    for ci in range(sc // rows):
        xc = _short_conv(xpad, 0, cw_ref, ci * rows, rows) + cb_ref[...]
        for g in range(gb):
            cs = slice(g * LANE, (g + 1) * LANE)
            xg = xc[:, cs]
            r = jax.nn.sigmoid(_dot(xg, wa_ref[g]) + ba_ref[:, cs])
            ig = jax.nn.sigmoid(_dot(xg, wx_ref[g]) + bx_ref[:, cs])
            a = jnp.exp(-LRU_C * r * _softplus(-lam_ref[:, cs]))
            abuf[ci * rows:(ci + 1) * rows, cs] = a
            bbuf[ci * rows:(ci + 1) * rows, cs] = (
                jnp.sqrt((1.0 - a) * (1.0 + a)) * (ig * xg))

    row = _iota((c, LANE), 0)

    def chunk_body(ci, carry):
        r0 = pl.multiple_of(ci * c, c)
        for g in range(gb):
            cs = slice(g * LANE, (g + 1) * LANE)
            a = abuf[pl.ds(r0, c), cs]
            bb = bbuf[pl.ds(r0, c), cs]
            d = 1
            while d < c:
                keep = row >= d
                a_sh = jnp.where(keep, pltpu.roll(a, d, 0), 1.0)
                b_sh = jnp.where(keep, pltpu.roll(bb, d, 0), 0.0)
                bb = a * b_sh + bb
                a = a * a_sh
                d *= 2
            h = bb + a * hcarry[:, cs]
            hcarry[:, cs] = h[c - 1:c, :]
            y = h * jax.nn.gelu(g_ref[pl.ds(r0, c), cs])
            y = y * lax.rsqrt(jnp.mean(y * y, axis=-1, keepdims=True) + EPS)
            o_ref[pl.ds(r0, c), cs] = (y * nw_ref[:, cs]).astype(o_ref.dtype)
        return carry

    lax.fori_loop(0, sc // c, chunk_body, 0)


def _rglru(proj3, conv_w, conv_b, w_a, b_a, w_x, b_x, lam, norm_w, col_x, col_g):
    b, s, _ = proj3.shape
    nblk = w_a.shape[0]
    gb = 4 if nblk % 4 == 0 else (3 if nblk % 3 == 0 else 1)
    width = gb * LANE
    lw = nblk * LANE
    sc = _largest_tile(s, 512, 2 * SCAN_CHUNK)
    assert col_x % width == 0 and col_g % width == 0
    bx0, bg0 = col_x // width, col_g // width
    vec = lambda: pl.BlockSpec((1, width), lambda bi, gi, si: (0, gi))
    mat = lambda: pl.BlockSpec((gb, LANE, LANE), lambda bi, gi, si: (gi, 0, 0))
    return pl.pallas_call(
        functools.partial(_rglru_kernel, sc=sc, gb=gb),
        grid=(b, nblk // gb, s // sc),
        in_specs=[pl.BlockSpec((None, sc, width), lambda bi, gi, si: (bi, si, bx0 + gi)),
                  pl.BlockSpec((None, sc, width), lambda bi, gi, si: (bi, si, bg0 + gi)),
                  pl.BlockSpec((SHORT_CONV, width), lambda bi, gi, si: (0, gi)),
                  vec(), mat(), vec(), mat(), vec(), vec(), vec()],
        out_specs=pl.BlockSpec((None, sc, width), lambda bi, gi, si: (bi, si, gi)),
        out_shape=jax.ShapeDtypeStruct((b, s, lw), BF16),
        scratch_shapes=[pltpu.VMEM((1, sc + HALO, width), F32),
                        pltpu.VMEM((sc, width), F32),
                        pltpu.VMEM((sc, width), F32),
                        pltpu.VMEM((1, width), F32)],
        compiler_params=_params("parallel", "parallel", "arbitrary"),
        name="rglru",
    )(proj3, proj3, conv_w, conv_b.reshape(1, lw), w_a, b_a.reshape(1, lw), w_x,
      b_x.reshape(1, lw), lam.reshape(1, lw), norm_w.reshape(1, lw))


def _sg_kernel(u_ref, v_ref, lnw_ref, lnb_ref, ws_ref, bst_ref, nw_ref, o_ref, *, sc,
               groups):
    t = SG_CHUNK
    tril = _iota((t, t), 0) >= _iota((t, t), 1)
    for ci in range(sc // t):
        rs = slice(ci * t, (ci + 1) * t)
        v = jax.nn.gelu(v_ref[rs, :])
        mu = jnp.mean(v, axis=-1, keepdims=True)
        vc = v - mu
        var = jnp.mean(vc * vc, axis=-1, keepdims=True)
        vn = vc * lax.rsqrt(var + EPS) * lnw_ref[...] + lnb_ref[...]
        for g in range(groups):
            cs = slice(g * LANE, (g + 1) * LANE)
            w_causal = jnp.where(tril, ws_ref[g], 0.0)
            z = _dot(w_causal, vn[:, cs]) + bst_ref[:, g:g + 1]
            y = jax.nn.gelu(u_ref[rs, cs]) * z
            y = y * lax.rsqrt(jnp.mean(y * y, axis=-1, keepdims=True) + EPS)
            o_ref[rs, cs] = (y * nw_ref[:, cs]).astype(o_ref.dtype)


def _spatial_gating(proj3, ln_w, ln_b, w_s, b_s, norm_w, col_u, col_v):
    b, s, _ = proj3.shape
    groups = w_s.shape[0]
    width = groups * LANE
    sc = _largest_tile(s, 512, SG_CHUNK)
    assert col_u % width == 0 and col_v % width == 0
    bu0, bv0 = col_u // width, col_v // width
    vec = lambda: pl.BlockSpec((1, width), lambda bi, si: (0, 0))
    return pl.pallas_call(
        functools.partial(_sg_kernel, sc=sc, groups=groups),
        grid=(b, s // sc),
        in_specs=[pl.BlockSpec((None, sc, width), lambda bi, si: (bi, si, bu0)),
                  pl.BlockSpec((None, sc, width), lambda bi, si: (bi, si, bv0)),
                  vec(), vec(),
                  pl.BlockSpec((groups, SG_CHUNK, SG_CHUNK), lambda bi, si: (0, 0, 0)),
                  pl.BlockSpec((SG_CHUNK, groups), lambda bi, si: (0, 0)),
                  vec()],
        out_specs=pl.BlockSpec((None, sc, width), lambda bi, si: (bi, si, 0)),
        out_shape=jax.ShapeDtypeStruct((b, s, width), BF16),
        compiler_params=_params("parallel", "parallel"),
        name="spatial_gating",
    )(proj3, proj3, ln_w.reshape(1, width), ln_b.reshape(1, width), w_s, b_s.T,
      norm_w.reshape(1, width))


def kernel(x, norm_mix, w_in, dn_conv_w, dn_a_log, dn_dt_bias, dn_norm_w, lru_conv_w, lru_conv_b, lru_w_a, lru_b_a, lru_w_x, lru_b_x, lru_lambda, lru_norm_w, sg_ln_w, sg_ln_b, sg_w_s, sg_b_s, sg_norm_w, w_out, norm_ffn, w_up, ffn_conv_w, ffn_conv_b, w_down, norm_final):
    bsz, seq, d = x.shape
    depth = w_in.shape[0]
    n_heads = dn_a_log.shape[1]
    dn_w = n_heads * HEAD_DIM
    lru_w = lru_lambda.shape[1]
    sg_w = sg_ln_w.shape[1]
    m = bsz * seq
    gate0 = 4 * dn_w
    main_cols = 4 * dn_w + 2 * lru_w + 2 * sg_w
    col_lx = 4 * dn_w
    col_lg = col_lx + lru_w
    col_su = col_lg + lru_w
    col_sv = col_su + sg_w

    x2 = x.reshape(m, d)
    for l in range(depth):
        w_main = jnp.concatenate(
            [w_in[l, :, :gate0], w_in[l, :, gate0 + 2 * n_heads:]], axis=1).astype(BF16)
        w_gate = jnp.pad(w_in[l, :, gate0:gate0 + 2 * n_heads],
                         ((0, 0), (0, LANE - 2 * n_heads))).astype(BF16)

        h = _rmsnorm(x2, norm_mix[l], BF16)
        proj = _matmul(h, w_main, name="in_proj")
        ba = _matmul(h, w_gate, name="in_proj_gates")
        proj3 = proj.reshape(bsz, seq, main_cols)
        ba3 = ba.reshape(bsz, seq, LANE)

        y_a = _deltanet(proj3, ba3, dn_conv_w[l], dn_a_log[l], dn_dt_bias[l], dn_norm_w[l],
                        n_heads, 0)
        y_b = _rglru(proj3, lru_conv_w[l], lru_conv_b[l], lru_w_a[l], lru_b_a[l],
                     lru_w_x[l], lru_b_x[l], lru_lambda[l], lru_norm_w[l], col_lx, col_lg)
        y_c = _spatial_gating(proj3, sg_ln_w[l], sg_ln_b[l], sg_w_s[l], sg_b_s[l],
                              sg_norm_w[l], col_su, col_sv)
        mix = jnp.concatenate([y_a, y_b, y_c], axis=-1).reshape(m, d)
        x2 = _matmul(mix, w_out[l].astype(BF16), residual=x2, name="out_proj")

        h = _rmsnorm(x2, norm_ffn[l], BF16)
        act = _ffn_up(h, w_up[l].astype(BF16), ffn_conv_w[l], ffn_conv_b[l], seq)
        x2 = _matmul(act, w_down[l].astype(BF16), residual=x2, tk_cap=5504,
                     name="down_proj")
    return _rmsnorm(x2, norm_final, F32).reshape(bsz, seq, d)
```

```python
import functools

import jax
import jax.numpy as jnp
from jax import lax
from jax.experimental import pallas as pl
from jax.experimental.pallas import tpu as pltpu

F32 = jnp.float32
BF16 = jnp.bfloat16
HIGHEST = lax.Precision.HIGHEST

EPS = 1e-6
LANE = 128
HALO = 8
HEAD_DIM = LANE
DN_CHUNK = 64
SG_CHUNK = 128
SCAN_CHUNK = 64
SHORT_CONV = 4
LRU_C = 8.0
VMEM_LIMIT_BYTES = 56 * 1024 * 1024


def _params(*semantics):
    return pltpu.CompilerParams(dimension_semantics=semantics,
                                vmem_limit_bytes=VMEM_LIMIT_BYTES)


def _largest_tile(n, cap, unit):
    best = None
    t = unit
    while t <= min(n, cap):
        if n % t == 0:
            best = t
        t += unit
    assert best is not None, (n, cap, unit)
    return best


def _dot(a, b):
    return jnp.dot(a.astype(BF16), b.astype(BF16), preferred_element_type=F32)


def _dot_nt(a, b):
    return lax.dot_general(a.astype(BF16), b.astype(BF16), (((1,), (1,)), ((), ())),
                           preferred_element_type=F32)


def _softplus(x):
    return jnp.maximum(x, 0.0) + jnp.log1p(jnp.exp(-jnp.abs(x)))


def _silu(x):
    return x * jax.nn.sigmoid(x)


def _iota(shape, dim):
    return lax.broadcasted_iota(jnp.int32, shape, dim)


def _rmsnorm_kernel(x_ref, w_ref, o_ref):
    x = x_ref[...]
    ms = jnp.mean(x * x, axis=-1, keepdims=True)
    o_ref[...] = (x * lax.rsqrt(ms + EPS) * w_ref[...]).astype(o_ref.dtype)


def _rmsnorm(x2, w, out_dtype):
    m, d = x2.shape
    tm = _largest_tile(m, 256, 16)
    return pl.pallas_call(
        _rmsnorm_kernel,
        grid=(m // tm,),
        in_specs=[pl.BlockSpec((tm, d), lambda i: (i, 0)),
                  pl.BlockSpec((1, d), lambda i: (0, 0))],
        out_specs=pl.BlockSpec((tm, d), lambda i: (i, 0)),
        out_shape=jax.ShapeDtypeStruct((m, d), out_dtype),
        compiler_params=_params("parallel"),
        name="rmsnorm",
    )(x2, w.reshape(1, d))


def _matmul_kernel(*refs, nk, has_res):
    if has_res:
        a_ref, w_ref, res_ref, o_ref = refs[:4]
    else:
        a_ref, w_ref, o_ref = refs[:3]
        res_ref = None
    part = jnp.dot(a_ref[...], w_ref[...], preferred_element_type=F32)

    def finish(acc):
        if res_ref is not None:
            acc = acc + res_ref[...]
        o_ref[...] = acc.astype(o_ref.dtype)

    if nk == 1:
        finish(part)
        return
    acc_ref = refs[-1]
    k = pl.program_id(2)

    @pl.when(k == 0)
    def _():
        acc_ref[...] = part

    @pl.when(k > 0)
    def _():
        acc_ref[...] += part

    @pl.when(k == nk - 1)
    def _():
        finish(acc_ref[...])


def _matmul(a, w, *, residual=None, out_dtype=F32, tm_cap=512, tn_cap=512, tk_cap=4096,
            name="matmul"):
    m, k = a.shape
    n = w.shape[1]
    tm = _largest_tile(m, tm_cap, 16)
    tn = _largest_tile(n, tn_cap, LANE)
    tk = _largest_tile(k, tk_cap, LANE)
    nk = k // tk
    in_specs = [pl.BlockSpec((tm, tk), lambda j, i, kk: (i, kk)),
                pl.BlockSpec((tk, tn), lambda j, i, kk: (kk, j))]
    args = [a, w]
    if residual is not None:
        in_specs.append(pl.BlockSpec((tm, tn), lambda j, i, kk: (i, j)))
        args.append(residual)
    scratch = [pltpu.VMEM((tm, tn), F32)] if nk > 1 else []
    return pl.pallas_call(
        functools.partial(_matmul_kernel, nk=nk, has_res=residual is not None),
        grid=(n // tn, m // tm, nk),
        in_specs=in_specs,
        out_specs=pl.BlockSpec((tm, tn), lambda j, i, kk: (i, j)),
        out_shape=jax.ShapeDtypeStruct((m, n), out_dtype),
        scratch_shapes=scratch,
        compiler_params=_params("parallel", "parallel", "arbitrary"),
        name=name,
    )(*args)


CAST_ROWS = 256


def _cast_rows(src_ref, dst_ref):
    def body(r, carry):
        rows = pl.ds(pl.multiple_of(r * CAST_ROWS, CAST_ROWS), CAST_ROWS)
        dst_ref[rows, :] = src_ref[rows, :].astype(dst_ref.dtype)
        return carry

    lax.fori_loop(0, src_ref.shape[0] // CAST_ROWS, body, 0)


def _weight_spec(w, layer, k, tn, col_block0=0):
    if layer is None:
        return pl.BlockSpec((k, tn), lambda j, i: (0, col_block0 + j))
    return pl.BlockSpec((None, k, tn), lambda j, i: (layer, 0, col_block0 + j))


def _matmul_wstat_kernel(*refs, has_res):
    if has_res:
        a_ref, w_ref, res_ref, o_ref, wb_ref = refs
    else:
        a_ref, w_ref, o_ref, wb_ref = refs
        res_ref = None

    @pl.when(pl.program_id(1) == 0)
    def _():
        _cast_rows(w_ref, wb_ref)

    acc = jnp.dot(a_ref[...], wb_ref[...], preferred_element_type=F32)
    if res_ref is not None:
        acc = acc + res_ref[...]
    o_ref[...] = acc


def _matmul_wstat(a, w, *, layer=None, n=None, residual=None, name="matmul"):
    m, k = a.shape
    n = w.shape[-1] if n is None else n
    assert k % CAST_ROWS == 0
    tm = _largest_tile(m, 1024, 16)
    tn = _largest_tile(n, 512, LANE)
    in_specs = [pl.BlockSpec((tm, k), lambda j, i: (i, 0)), _weight_spec(w, layer, k, tn)]
    args = [a, w]
    if residual is not None:
        in_specs.append(pl.BlockSpec((tm, tn), lambda j, i: (i, j)))
        args.append(residual)
    return pl.pallas_call(
        functools.partial(_matmul_wstat_kernel, has_res=residual is not None),
        grid=(n // tn, m // tm),
        in_specs=in_specs,
        out_specs=pl.BlockSpec((tm, tn), lambda j, i: (i, j)),
        out_shape=jax.ShapeDtypeStruct((m, n), F32),
        scratch_shapes=[pltpu.VMEM((k, tn), BF16)],
        compiler_params=_params("arbitrary", "arbitrary"),
        name=name,
    )(*args)


FFN_ROW_BLOCK = 256


def _ffn_up_kernel(a_ref, wg_ref, wu_ref, cwg_ref, cwu_ref, bg_ref, bu_ref, o_ref,
                   wgb, wub, gbuf, ubuf, *, tm, rb, tiles_per_seq):
    i = pl.program_id(1)
    taps = cwg_ref.shape[0]

    @pl.when(i == 0)
    def _():
        _cast_rows(wg_ref, wgb)
        _cast_rows(wu_ref, wub)

    first = (i % tiles_per_seq) == 0
    for buf in (gbuf, ubuf):
        @pl.when(first)
        def _():
            buf[0:HALO, :] = jnp.zeros((HALO, buf.shape[1]), F32)

        @pl.when(jnp.logical_not(first))
        def _():
            buf[0:HALO, :] = buf[tm:tm + HALO, :]

    def conv(buf, cw_ref, r0):
        acc = None
        for j in range(taps):
            term = cw_ref[j:j + 1, :] * buf[pl.ds(r0 + HALO - taps + 1 + j, rb), :]
            acc = term if acc is None else acc + term
        return acc

    for r in range(tm // rb):
        a = a_ref[r * rb:(r + 1) * rb, :]
        gbuf[HALO + r * rb:HALO + (r + 1) * rb, :] = jnp.dot(
            a, wgb[...], preferred_element_type=F32)
        ubuf[HALO + r * rb:HALO + (r + 1) * rb, :] = jnp.dot(
            a, wub[...], preferred_element_type=F32)
        g = conv(gbuf, cwg_ref, r * rb) + bg_ref[...]
        u = conv(ubuf, cwu_ref, r * rb) + bu_ref[...]
        o_ref[r * rb:(r + 1) * rb, :] = (_silu(g) * u).astype(o_ref.dtype)


def _ffn_up(h, w_up, layer, conv_w, conv_b, seq):
    m, k = h.shape
    f = w_up.shape[-1] // 2
    taps = conv_w.shape[0]
    tm = _largest_tile(seq, 1024, FFN_ROW_BLOCK)
    tf = _largest_tile(f, 256, LANE)
    nf = f // tf
    assert k % CAST_ROWS == 0
    conv_b = conv_b.reshape(1, 2 * f)
    return pl.pallas_call(
        functools.partial(_ffn_up_kernel, tm=tm, rb=FFN_ROW_BLOCK, tiles_per_seq=seq // tm),
        grid=(nf, m // tm),
        in_specs=[pl.BlockSpec((tm, k), lambda j, i: (i, 0)),
                  _weight_spec(w_up, layer, k, tf),
                  _weight_spec(w_up, layer, k, tf, nf),
                  pl.BlockSpec((taps, tf), lambda j, i: (0, j)),
                  pl.BlockSpec((taps, tf), lambda j, i: (0, nf + j)),
                  pl.BlockSpec((1, tf), lambda j, i: (0, j)),
                  pl.BlockSpec((1, tf), lambda j, i: (0, nf + j))],
        out_specs=pl.BlockSpec((tm, tf), lambda j, i: (i, j)),
        out_shape=jax.ShapeDtypeStruct((m, f), BF16),
        scratch_shapes=[pltpu.VMEM((k, tf), BF16),
                        pltpu.VMEM((k, tf), BF16),
                        pltpu.VMEM((tm + HALO, tf), F32),
                        pltpu.VMEM((tm + HALO, tf), F32)],
        compiler_params=_params("arbitrary", "arbitrary"),
        name="ffn_up_conv_gate",
    )(h, w_up, w_up, conv_w, conv_w, conv_b, conv_b)


def _load_with_halo(xpad, idx, src_ref, is_first, rows):
    width = xpad.shape[-1]

    @pl.when(is_first)
    def _():
        xpad[idx, 0:HALO, :] = jnp.zeros((HALO, width), F32)

    @pl.when(jnp.logical_not(is_first))
    def _():
        xpad[idx, 0:HALO, :] = xpad[idx, rows:rows + HALO, :]

    xpad[idx, HALO:, :] = src_ref[...]


def _short_conv(xpad, idx, cw_ref, r0, rows, cs=slice(None)):
    taps = cw_ref.shape[0]
    acc = None
    for j in range(taps):
        term = cw_ref[j:j + 1, cs] * xpad[idx, pl.ds(r0 + HALO - taps + 1 + j, rows), cs]
        acc = term if acc is None else acc + term
    return acc


def _deltanet_kernel(q_ref, k_ref, v_ref, z_ref, ba_ref, cwq_ref, cwk_ref, cwv_ref,
                     gp_ref, nw_ref, o_ref, xpad, qkvs, gct, state, *, sc, hb, n_heads):
    c = DN_CHUNK
    c2 = 2 * c
    npairs = hb // 2
    hblk = pl.program_id(1)
    is_first = pl.program_id(2) == 0

    @pl.when(is_first)
    def _():
        state[...] = jnp.zeros(state.shape, F32)

    for idx, src in enumerate((q_ref, k_ref, v_ref)):
        _load_with_halo(xpad, idx, src, is_first, sc)

    for idx, cw_ref in enumerate((cwq_ref, cwk_ref, cwv_ref)):
        for ci in range(sc // c):
            for i in range(hb):
                cs = slice(i * LANE, (i + 1) * LANE)
                seg = _silu(_short_conv(xpad, idx, cw_ref, ci * c, c, cs))
                if idx < 2:
                    seg = seg * lax.rsqrt(jnp.sum(seg * seg, axis=-1, keepdims=True) + EPS)
                qkvs[idx, i // 2, ci, (i % 2) * c:(i % 2 + 1) * c, :] = seg

    row = _iota((c2, c2), 0)
    col = _iota((c2, c2), 1)
    same_head = (row >= c) == (col >= c)
    causal = (row >= col) & same_head
    strict = (row > col) & same_head
    second = (row >= c).astype(jnp.int32)
    left_cols = col < c
    eye = (row == col).astype(F32)
    tril = (_iota((c, c), 0) >= _iota((c, c), 1)).astype(F32)
    lane_row = _iota((1, c2), 1)
    rows1 = _iota((c2, 1), 0)
    rows2 = _iota((2 * c2, 1), 0)
    first_head_rows = (rows2 % c2) < c
    scale = HEAD_DIM ** -0.5
    pairs = range(npairs)

    def chunk_body(ci, carry):
        r0 = pl.multiple_of(ci * c, c)
        ba = ba_ref[pl.ds(r0, c), :]
        beta_all = jax.nn.sigmoid(ba)
        g_all = -jnp.exp(gp_ref[0:1, :]) * _softplus(ba + gp_ref[1:2, :])
        gc_all = jnp.dot(tril, g_all, precision=HIGHEST,
                         preferred_element_type=F32)
        gc_stack = jnp.concatenate([gc_all, gc_all], axis=0)
        beta_stack = jnp.concatenate([beta_all, beta_all], axis=0)
        gct[...] = gc_stack.T
        h0 = [hblk * hb + 2 * p for p in pairs]
        bcol = [jnp.sum(jnp.where(col == h + second, beta_stack, 0.0), axis=-1,
                        keepdims=True) for h in h0]
        gcol = [jnp.sum(jnp.where(col == h + n_heads + second, gc_stack, 0.0), axis=-1,
                        keepdims=True) for h in h0]
        grow = [jnp.where(lane_row < c, gct[pl.ds(h + n_heads, 1), :],
                          gct[pl.ds(h + n_heads + 1, 1), :]) for h in h0]
        glast = [jnp.where(rows1 < c, g[c - 1:c, :], g[c2 - 1:c2, :]) for g in gcol]
        decay = [jnp.where(causal, jnp.exp(jnp.where(causal, gcol[p] - grow[p], 0.0)), 0.0)
                 for p in pairs]
        egc = [jnp.exp(g) for g in gcol]

        q = [qkvs[0, p, ci] * scale for p in pairs]
        k = [qkvs[1, p, ci] for p in pairs]
        v = [qkvs[2, p, ci] for p in pairs]
        kb = [k[p] * bcol[p] for p in pairs]
        kq = [_dot_nt(jnp.concatenate([kb[p], q[p]], axis=0), k[p]) for p in pairs]
        a_low = [jnp.where(strict, kq[p][:c2] * decay[p], 0.0) for p in pairs]
        attn = [kq[p][c2:] * decay[p] for p in pairs]

        t_inv = [eye - a for a in a_low]
        pw = a_low
        n = 2
        while n < c:
            pw = [_dot(x, x) for x in pw]
            t_inv = [t_inv[p] + _dot(t_inv[p], pw[p]) for p in pairs]
            n *= 2

        sol = [_dot(t_inv[p], jnp.concatenate([v[p] * bcol[p], kb[p] * egc[p]], axis=1))
               for p in pairs]
        k_dec = [k[p] * jnp.exp(glast[p] - gcol[p]) for p in pairs]
        s_prev = [state[p] for p in pairs]
        ws_qs = []
        for p in pairs:
            x = jnp.concatenate([sol[p][:, LANE:], q[p] * egc[p]], axis=0)
            x_wide = jnp.concatenate([jnp.where(first_head_rows, x, 0.0),
                                      jnp.where(first_head_rows, 0.0, x)], axis=1)
            ws_qs.append(_dot(x_wide, s_prev[p]))
        v_new = [sol[p][:, :LANE] - ws_qs[p][:c2] for p in pairs]
        o = [ws_qs[p][c2:] + _dot(attn[p], v_new[p]) for p in pairs]
        for p in pairs:
            kdt = k_dec[p].T
            kd_stack = jnp.concatenate([jnp.where(left_cols, kdt, 0.0),
                                        jnp.where(left_cols, 0.0, kdt)], axis=0)
            s_decay = jnp.where(rows2 < c2, jnp.exp(gcol[p][c - 1:c, :]),
                                jnp.exp(gcol[p][c2 - 1:c2, :]))
            state[p] = s_prev[p] * s_decay + _dot(kd_stack, v_new[p])
        for p in pairs:
            cs0 = slice((2 * p) * LANE, (2 * p + 1) * LANE)
            cs1 = slice((2 * p + 1) * LANE, (2 * p + 2) * LANE)
            zg = _silu(jnp.concatenate([z_ref[pl.ds(r0, c), cs0], z_ref[pl.ds(r0, c), cs1]],
                                       axis=0))
            op = o[p] * lax.rsqrt(jnp.mean(o[p] * o[p], axis=-1, keepdims=True) + EPS)
            op = (op * nw_ref[...] * zg).astype(o_ref.dtype)
            o_ref[pl.ds(r0, c), cs0] = op[:c]
            o_ref[pl.ds(r0, c), cs1] = op[c:]
        return carry

    lax.fori_loop(0, sc // c, chunk_body, 0)


def _deltanet(proj3, ba3, conv_w, a_log, dt_bias, norm_w, n_heads, col0):
    b, s, _ = proj3.shape
    assert n_heads % 2 == 0 and 2 * n_heads <= LANE
    hb = _largest_tile(n_heads, 12, 2)
    width = hb * LANE
    nhb = n_heads // hb
    sc = _largest_tile(s, 256, DN_CHUNK)
    base = col0 // width
    assert col0 % width == 0
    gp = jnp.zeros((2, LANE), F32)
    gp = gp.at[0, n_heads:2 * n_heads].set(a_log).at[1, n_heads:2 * n_heads].set(dt_bias)

    def col_spec(group):
        return pl.BlockSpec((None, sc, width),
                            lambda bi, hi, si, g=group: (bi, si, base + g * nhb + hi))

    def cw_spec(group):
        return pl.BlockSpec((SHORT_CONV, width), lambda bi, hi, si, g=group: (0, g * nhb + hi))

    return pl.pallas_call(
        functools.partial(_deltanet_kernel, sc=sc, hb=hb, n_heads=n_heads),
        grid=(b, nhb, s // sc),
        in_specs=[col_spec(0), col_spec(1), col_spec(2), col_spec(3),
                  pl.BlockSpec((None, sc, LANE), lambda bi, hi, si: (bi, si, 0)),
                  cw_spec(0), cw_spec(1), cw_spec(2),
                  pl.BlockSpec((2, LANE), lambda bi, hi, si: (0, 0)),
                  pl.BlockSpec((1, LANE), lambda bi, hi, si: (0, 0))],
        out_specs=pl.BlockSpec((None, sc, width), lambda bi, hi, si: (bi, si, hi)),
        out_shape=jax.ShapeDtypeStruct((b, s, n_heads * LANE), BF16),
        scratch_shapes=[pltpu.VMEM((3, sc + HALO, width), F32),
                        pltpu.VMEM((3, hb // 2, sc // DN_CHUNK, 2 * DN_CHUNK, LANE), F32),
                        pltpu.VMEM((LANE, LANE), F32),
                        pltpu.VMEM((hb // 2, 2 * HEAD_DIM, HEAD_DIM), F32)],
        compiler_params=_params("parallel", "parallel", "arbitrary"),
        name="gated_deltanet",
    )(proj3, proj3, proj3, proj3, ba3, conv_w, conv_w, conv_w, gp, norm_w.reshape(1, LANE))


def _rglru_kernel(x_ref, g_ref, cw_ref, cb_ref, wa_ref, ba_ref, wx_ref, bx_ref, lam_ref,
                  nw_ref, o_ref, xpad, abuf, bbuf, hcarry, *, sc, gb):
    c = SCAN_CHUNK
    is_first = pl.program_id(2) == 0

    @pl.when(is_first)
    def _():
        hcarry[...] = jnp.zeros(hcarry.shape, F32)

    _load_with_halo(xpad, 0, x_ref, is_first, sc)

    rows = 2 * c
    for ci in range(sc // rows):
        xc = _short_conv(xpad, 0, cw_ref, ci * rows, rows) + cb_ref[...]
        for g in range(gb):
            cs = slice(g * LANE, (g + 1) * LANE)
            xg = xc[:, cs]
            r = jax.nn.sigmoid(_dot(xg, wa_ref[g]) + ba_ref[:, cs])
            ig = jax.nn.sigmoid(_dot(xg, wx_ref[g]) + bx_ref[:, cs])
            a = jnp.exp(-LRU_C * r * _softplus(-lam_ref[:, cs]))
            abuf[ci * rows:(ci + 1) * rows, cs] = a
            bbuf[ci * rows:(ci + 1) * rows, cs] = (
                jnp.sqrt((1.0 - a) * (1.0 + a)) * (ig * xg))

    row = _iota((c, LANE), 0)

    def chunk_body(ci, carry):
        r0 = pl.multiple_of(ci * c, c)
        for g in range(gb):
            cs = slice(g * LANE, (g + 1) * LANE)
            a = abuf[pl.ds(r0, c), cs]
            bb = bbuf[pl.ds(r0, c), cs]
            d = 1
            while d < c:
                keep = row >= d
                a_sh = jnp.where(keep, pltpu.roll(a, d, 0), 1.0)
                b_sh = jnp.where(keep, pltpu.roll(bb, d, 0), 0.0)
                bb = a * b_sh + bb
                a = a * a_sh
                d *= 2
            h = bb + a * hcarry[:, cs]
            hcarry[:, cs] = h[c - 1:c, :]
            y = h * jax.nn.gelu(g_ref[pl.ds(r0, c), cs])
            y = y * lax.rsqrt(jnp.mean(y * y, axis=-1, keepdims=True) + EPS)
            o_ref[pl.ds(r0, c), cs] = (y * nw_ref[:, cs]).astype(o_ref.dtype)
        return carry

    lax.fori_loop(0, sc // c, chunk_body, 0)


def _rglru(proj3, conv_w, conv_b, w_a, b_a, w_x, b_x, lam, norm_w, col_x, col_g):
    b, s, _ = proj3.shape
    nblk = w_a.shape[0]
    gb = 4 if nblk % 4 == 0 else (3 if nblk % 3 == 0 else 1)
    width = gb * LANE
    lw = nblk * LANE
    sc = _largest_tile(s, 512, 2 * SCAN_CHUNK)
    assert col_x % width == 0 and col_g % width == 0
    bx0, bg0 = col_x // width, col_g // width
    vec = lambda: pl.BlockSpec((1, width), lambda bi, gi, si: (0, gi))
    mat = lambda: pl.BlockSpec((gb, LANE, LANE), lambda bi, gi, si: (gi, 0, 0))
    return pl.pallas_call(
        functools.partial(_rglru_kernel, sc=sc, gb=gb),
        grid=(b, nblk // gb, s // sc),
        in_specs=[pl.BlockSpec((None, sc, width), lambda bi, gi, si: (bi, si, bx0 + gi)),
                  pl.BlockSpec((None, sc, width), lambda bi, gi, si: (bi, si, bg0 + gi)),
                  pl.BlockSpec((SHORT_CONV, width), lambda bi, gi, si: (0, gi)),
                  vec(), mat(), vec(), mat(), vec(), vec(), vec()],
        out_specs=pl.BlockSpec((None, sc, width), lambda bi, gi, si: (bi, si, gi)),
        out_shape=jax.ShapeDtypeStruct((b, s, lw), BF16),
        scratch_shapes=[pltpu.VMEM((1, sc + HALO, width), F32),
                        pltpu.VMEM((sc, width), F32),
                        pltpu.VMEM((sc, width), F32),
                        pltpu.VMEM((1, width), F32)],
        compiler_params=_params("parallel", "parallel", "arbitrary"),
        name="rglru",
    )(proj3, proj3, conv_w, conv_b.reshape(1, lw), w_a, b_a.reshape(1, lw), w_x,
      b_x.reshape(1, lw), lam.reshape(1, lw), norm_w.reshape(1, lw))


def _sg_kernel(u_ref, v_ref, lnw_ref, lnb_ref, ws_ref, bst_ref, nw_ref, o_ref, *, sc,
               groups):
    t = SG_CHUNK
    tril = _iota((t, t), 0) >= _iota((t, t), 1)
    for ci in range(sc // t):
        rs = slice(ci * t, (ci + 1) * t)
        v = jax.nn.gelu(v_ref[rs, :])
        mu = jnp.mean(v, axis=-1, keepdims=True)
        vc = v - mu
        var = jnp.mean(vc * vc, axis=-1, keepdims=True)
        vn = vc * lax.rsqrt(var + EPS) * lnw_ref[...] + lnb_ref[...]
        for g in range(groups):
            cs = slice(g * LANE, (g + 1) * LANE)
            w_causal = jnp.where(tril, ws_ref[g], 0.0)
            z = _dot(w_causal, vn[:, cs]) + bst_ref[:, g:g + 1]
            y = jax.nn.gelu(u_ref[rs, cs]) * z
            y = y * lax.rsqrt(jnp.mean(y * y, axis=-1, keepdims=True) + EPS)
            o_ref[rs, cs] = (y * nw_ref[:, cs]).astype(o_ref.dtype)


def _spatial_gating(proj3, ln_w, ln_b, w_s, b_s, norm_w, col_u, col_v):
    b, s, _ = proj3.shape
    groups = w_s.shape[0]
    width = groups * LANE
    sc = _largest_tile(s, 512, SG_CHUNK)
    assert col_u % width == 0 and col_v % width == 0
    bu0, bv0 = col_u // width, col_v // width
    vec = lambda: pl.BlockSpec((1, width), lambda bi, si: (0, 0))
    return pl.pallas_call(
        functools.partial(_sg_kernel, sc=sc, groups=groups),
        grid=(b, s // sc),
        in_specs=[pl.BlockSpec((None, sc, width), lambda bi, si: (bi, si, bu0)),
                  pl.BlockSpec((None, sc, width), lambda bi, si: (bi, si, bv0)),
                  vec(), vec(),
                  pl.BlockSpec((groups, SG_CHUNK, SG_CHUNK), lambda bi, si: (0, 0, 0)),
                  pl.BlockSpec((SG_CHUNK, groups), lambda bi, si: (0, 0)),
                  vec()],
        out_specs=pl.BlockSpec((None, sc, width), lambda bi, si: (bi, si, 0)),
        out_shape=jax.ShapeDtypeStruct((b, s, width), BF16),
        compiler_params=_params("parallel", "parallel"),
        name="spatial_gating",
    )(proj3, proj3, ln_w.reshape(1, width), ln_b.reshape(1, width), w_s, b_s.T,
      norm_w.reshape(1, width))


def kernel(x, norm_mix, w_in, dn_conv_w, dn_a_log, dn_dt_bias, dn_norm_w, lru_conv_w, lru_conv_b, lru_w_a, lru_b_a, lru_w_x, lru_b_x, lru_lambda, lru_norm_w, sg_ln_w, sg_ln_b, sg_w_s, sg_b_s, sg_norm_w, w_out, norm_ffn, w_up, ffn_conv_w, ffn_conv_b, w_down, norm_final):
    bsz, seq, d = x.shape
    depth = w_in.shape[0]
    n_heads = dn_a_log.shape[1]
    dn_w = n_heads * HEAD_DIM
    lru_w = lru_lambda.shape[1]
    sg_w = sg_ln_w.shape[1]
    m = bsz * seq
    gate0 = 4 * dn_w
    tail0 = gate0 + 2 * n_heads
    tail_cols = 2 * lru_w + 2 * sg_w

    x2 = x.reshape(m, d)
    for l in range(depth):
        w_tail = w_in[l, :, tail0:]
        w_gate = jnp.pad(w_in[l, :, gate0:tail0], ((0, 0), (0, LANE - 2 * n_heads)))

        h = _rmsnorm(x2, norm_mix[l], BF16)
        proj_a = _matmul_wstat(h, w_in, layer=l, n=gate0, name="in_proj_qkvz")
        proj_b = _matmul_wstat(h, w_tail, name="in_proj_tail")
        ba = _matmul_wstat(h, w_gate, name="in_proj_gates")
        proj_a3 = proj_a.reshape(bsz, seq, gate0)
        proj_b3 = proj_b.reshape(bsz, seq, tail_cols)
        ba3 = ba.reshape(bsz, seq, LANE)

        y_a = _deltanet(proj_a3, ba3, dn_conv_w[l], dn_a_log[l], dn_dt_bias[l],
                        dn_norm_w[l], n_heads, 0)
        y_b = _rglru(proj_b3, lru_conv_w[l], lru_conv_b[l], lru_w_a[l], lru_b_a[l],
                     lru_w_x[l], lru_b_x[l], lru_lambda[l], lru_norm_w[l], 0, lru_w)
        y_c = _spatial_gating(proj_b3, sg_ln_w[l], sg_ln_b[l], sg_w_s[l], sg_b_s[l],
                              sg_norm_w[l], 2 * lru_w, 2 * lru_w + sg_w)
        mix = jnp.concatenate([y_a, y_b, y_c], axis=-1).reshape(m, d)
        x2 = _matmul_wstat(mix, w_out, layer=l, residual=x2, name="out_proj")

        h = _rmsnorm(x2, norm_ffn[l], BF16)
        act = _ffn_up(h, w_up, l, ffn_conv_w[l], ffn_conv_b[l], seq)
        x2 = _matmul(act, w_down[l].astype(BF16), residual=x2, tm_cap=1024, tk_cap=5504,
                     name="down_proj")
    return _rmsnorm(x2, norm_final, F32).reshape(bsz, seq, d)
```

```python
import functools

import jax
import jax.numpy as jnp
from jax import lax
from jax.experimental import pallas as pl
from jax.experimental.pallas import tpu as pltpu

F32 = jnp.float32
BF16 = jnp.bfloat16
HIGHEST = lax.Precision.HIGHEST

EPS = 1e-6
LANE = 128
HALO = 8
HEAD_DIM = LANE
DN_CHUNK = 64
SG_CHUNK = 128
SCAN_CHUNK = 64
SHORT_CONV = 4
LRU_C = 8.0
VMEM_LIMIT_BYTES = 56 * 1024 * 1024


def _params(*semantics):
    return pltpu.CompilerParams(dimension_semantics=semantics,
                                vmem_limit_bytes=VMEM_LIMIT_BYTES)


def _largest_tile(n, cap, unit):
    best = None
    t = unit
    while t <= min(n, cap):
        if n % t == 0:
            best = t
        t += unit
    assert best is not None, (n, cap, unit)
    return best


def _dot(a, b):
    return jnp.dot(a.astype(BF16), b.astype(BF16), preferred_element_type=F32)


def _dot_nt(a, b):
    return lax.dot_general(a.astype(BF16), b.astype(BF16), (((1,), (1,)), ((), ())),
                           preferred_element_type=F32)


def _softplus(x):
    return jnp.maximum(x, 0.0) + jnp.log1p(jnp.exp(-jnp.abs(x)))


def _silu(x):
    return x * jax.nn.sigmoid(x)


def _iota(shape, dim):
    return lax.broadcasted_iota(jnp.int32, shape, dim)


def _rmsnorm_kernel(x_ref, w_ref, o_ref):
    x = x_ref[...]
    ms = jnp.mean(x * x, axis=-1, keepdims=True)
    o_ref[...] = (x * lax.rsqrt(ms + EPS) * w_ref[...]).astype(o_ref.dtype)


def _rmsnorm(x2, w, out_dtype):
    m, d = x2.shape
    tm = _largest_tile(m, 256, 16)
    return pl.pallas_call(
        _rmsnorm_kernel,
        grid=(m // tm,),
        in_specs=[pl.BlockSpec((tm, d), lambda i: (i, 0)),
                  pl.BlockSpec((1, d), lambda i: (0, 0))],
        out_specs=pl.BlockSpec((tm, d), lambda i: (i, 0)),
        out_shape=jax.ShapeDtypeStruct((m, d), out_dtype),
        compiler_params=_params("parallel"),
        name="rmsnorm",
    )(x2, w.reshape(1, d))


def _matmul_kernel(*refs, nk, has_res):
    if has_res:
        a_ref, w_ref, res_ref, o_ref = refs[:4]
    else:
        a_ref, w_ref, o_ref = refs[:3]
        res_ref = None
    part = jnp.dot(a_ref[...], w_ref[...], preferred_element_type=F32)

    def finish(acc):
        if res_ref is not None:
            acc = acc + res_ref[...]
        o_ref[...] = acc.astype(o_ref.dtype)

    if nk == 1:
        finish(part)
        return
    acc_ref = refs[-1]
    k = pl.program_id(2)

    @pl.when(k == 0)
    def _():
        acc_ref[...] = part

    @pl.when(k > 0)
    def _():
        acc_ref[...] += part

    @pl.when(k == nk - 1)
    def _():
        finish(acc_ref[...])


def _matmul(a, w, *, residual=None, out_dtype=F32, tm_cap=512, tn_cap=512, tk_cap=4096,
            name="matmul"):
    m, k = a.shape
    n = w.shape[1]
    tm = _largest_tile(m, tm_cap, 16)
    tn = _largest_tile(n, tn_cap, LANE)
    tk = _largest_tile(k, tk_cap, LANE)
    nk = k // tk
    in_specs = [pl.BlockSpec((tm, tk), lambda j, i, kk: (i, kk)),
                pl.BlockSpec((tk, tn), lambda j, i, kk: (kk, j))]
    args = [a, w]
    if residual is not None:
        in_specs.append(pl.BlockSpec((tm, tn), lambda j, i, kk: (i, j)))
        args.append(residual)
    scratch = [pltpu.VMEM((tm, tn), F32)] if nk > 1 else []
    return pl.pallas_call(
        functools.partial(_matmul_kernel, nk=nk, has_res=residual is not None),
        grid=(n // tn, m // tm, nk),
        in_specs=in_specs,
        out_specs=pl.BlockSpec((tm, tn), lambda j, i, kk: (i, j)),
        out_shape=jax.ShapeDtypeStruct((m, n), out_dtype),
        scratch_shapes=scratch,
        compiler_params=_params("parallel", "parallel", "arbitrary"),
        name=name,
    )(*args)


CAST_ROWS = 256


def _cast_rows(src_ref, dst_ref):
    def body(r, carry):
        rows = pl.ds(pl.multiple_of(r * CAST_ROWS, CAST_ROWS), CAST_ROWS)
        dst_ref[rows, :] = src_ref[rows, :].astype(dst_ref.dtype)
        return carry

    lax.fori_loop(0, src_ref.shape[0] // CAST_ROWS, body, 0)


def _weight_spec(w, layer, k, tn, col_block0=0):
    if layer is None:
        return pl.BlockSpec((k, tn), lambda j, i: (0, col_block0 + j))
    return pl.BlockSpec((None, k, tn), lambda j, i: (layer, 0, col_block0 + j))


def _matmul_wstat_kernel(*refs, has_res):
    if has_res:
        a_ref, w_ref, res_ref, o_ref, wb_ref = refs
    else:
        a_ref, w_ref, o_ref, wb_ref = refs
        res_ref = None

    @pl.when(pl.program_id(1) == 0)
    def _():
        _cast_rows(w_ref, wb_ref)

    acc = jnp.dot(a_ref[...], wb_ref[...], preferred_element_type=F32)
    if res_ref is not None:
        acc = acc + res_ref[...]
    o_ref[...] = acc


def _matmul_wstat(a, w, *, layer=None, n=None, residual=None, name="matmul"):
    m, k = a.shape
    n = w.shape[-1] if n is None else n
    assert k % CAST_ROWS == 0
    tm = _largest_tile(m, 1024, 16)
    tn = _largest_tile(n, 512, LANE)
    in_specs = [pl.BlockSpec((tm, k), lambda j, i: (i, 0)), _weight_spec(w, layer, k, tn)]
    args = [a, w]
    if residual is not None:
        in_specs.append(pl.BlockSpec((tm, tn), lambda j, i: (i, j)))
        args.append(residual)
    return pl.pallas_call(
        functools.partial(_matmul_wstat_kernel, has_res=residual is not None),
        grid=(n // tn, m // tm),
        in_specs=in_specs,
        out_specs=pl.BlockSpec((tm, tn), lambda j, i: (i, j)),
        out_shape=jax.ShapeDtypeStruct((m, n), F32),
        scratch_shapes=[pltpu.VMEM((k, tn), BF16)],
        compiler_params=_params("arbitrary", "arbitrary"),
        name=name,
    )(*args)


FFN_ROW_BLOCK = 512


def _ffn_up_kernel(a_ref, wg_ref, wu_ref, cwg_ref, cwu_ref, bg_ref, bu_ref, wd_ref,
                   o_ref, wdb_ref, wgb, wub, gbuf, ubuf, *, tm, rb, tiles_per_seq):
    i = pl.program_id(1)
    taps = cwg_ref.shape[0]

    wdb_ref[...] = wd_ref[...].astype(wdb_ref.dtype)

    @pl.when(i == 0)
    def _():
        _cast_rows(wg_ref, wgb)
        _cast_rows(wu_ref, wub)

    first = (i % tiles_per_seq) == 0
    for buf in (gbuf, ubuf):
        @pl.when(first)
        def _():
            buf[0:HALO, :] = jnp.zeros((HALO, buf.shape[1]), F32)

        @pl.when(jnp.logical_not(first))
        def _():
            buf[0:HALO, :] = buf[tm:tm + HALO, :]

    def conv(buf, cw_ref, r0):
        acc = None
        for j in range(taps):
            term = cw_ref[j:j + 1, :] * buf[pl.ds(r0 + HALO - taps + 1 + j, rb), :]
            acc = term if acc is None else acc + term
        return acc

    for r in range(tm // rb):
        a = a_ref[r * rb:(r + 1) * rb, :]
        gbuf[HALO + r * rb:HALO + (r + 1) * rb, :] = jnp.dot(
            a, wgb[...], preferred_element_type=F32)
        ubuf[HALO + r * rb:HALO + (r + 1) * rb, :] = jnp.dot(
            a, wub[...], preferred_element_type=F32)
        g = conv(gbuf, cwg_ref, r * rb) + bg_ref[...]
        u = conv(ubuf, cwu_ref, r * rb) + bu_ref[...]
        o_ref[r * rb:(r + 1) * rb, :] = (_silu(g) * u).astype(o_ref.dtype)


def _ffn_up(h, w_up, w_down, layer, conv_w, conv_b, seq):
    m, k = h.shape
    f = w_up.shape[-1] // 2
    d_out = w_down.shape[-1]
    taps = conv_w.shape[0]
    tm = _largest_tile(seq, 1024, FFN_ROW_BLOCK)
    tf = _largest_tile(f, 256, LANE)
    nf = f // tf
    row_tiles = m // tm
    slab = f // (nf * row_tiles)
    assert k % CAST_ROWS == 0 and slab * nf * row_tiles == f and slab % 16 == 0
    conv_b = conv_b.reshape(1, 2 * f)
    return pl.pallas_call(
        functools.partial(_ffn_up_kernel, tm=tm, rb=FFN_ROW_BLOCK, tiles_per_seq=seq // tm),
        grid=(nf, row_tiles),
        in_specs=[pl.BlockSpec((tm, k), lambda j, i: (i, 0)),
                  _weight_spec(w_up, layer, k, tf),
                  _weight_spec(w_up, layer, k, tf, nf),
                  pl.BlockSpec((taps, tf), lambda j, i: (0, j)),
                  pl.BlockSpec((taps, tf), lambda j, i: (0, nf + j)),
                  pl.BlockSpec((1, tf), lambda j, i: (0, j)),
                  pl.BlockSpec((1, tf), lambda j, i: (0, nf + j)),
                  pl.BlockSpec((None, slab, d_out),
                               lambda j, i: (layer, j * row_tiles + i, 0))],
        out_specs=[pl.BlockSpec((tm, tf), lambda j, i: (i, j)),
                   pl.BlockSpec((slab, d_out), lambda j, i: (j * row_tiles + i, 0))],
        out_shape=[jax.ShapeDtypeStruct((m, f), BF16),
                   jax.ShapeDtypeStruct((f, d_out), BF16)],
        scratch_shapes=[pltpu.VMEM((k, tf), BF16),
                        pltpu.VMEM((k, tf), BF16),
                        pltpu.VMEM((tm + HALO, tf), F32),
                        pltpu.VMEM((tm + HALO, tf), F32)],
        compiler_params=_params("arbitrary", "arbitrary"),
        name="ffn_up_conv_gate",
    )(h, w_up, w_up, conv_w, conv_w, conv_b, conv_b, w_down)


def _load_with_halo(xpad, idx, src_ref, is_first, rows):
    width = xpad.shape[-1]

    @pl.when(is_first)
    def _():
        xpad[idx, 0:HALO, :] = jnp.zeros((HALO, width), F32)

    @pl.when(jnp.logical_not(is_first))
    def _():
        xpad[idx, 0:HALO, :] = xpad[idx, rows:rows + HALO, :]

    xpad[idx, HALO:, :] = src_ref[...]


def _short_conv(xpad, idx, cw_ref, r0, rows, cs=slice(None)):
    taps = cw_ref.shape[0]
    acc = None
    for j in range(taps):
        term = cw_ref[j:j + 1, cs] * xpad[idx, pl.ds(r0 + HALO - taps + 1 + j, rows), cs]
        acc = term if acc is None else acc + term
    return acc


def _deltanet_kernel(q_ref, k_ref, v_ref, z_ref, ba_ref, cwq_ref, cwk_ref, cwv_ref,
                     gp_ref, nw_ref, o_ref, xpad, qkvs, gct, state, *, sc, hb, n_heads):
    c = DN_CHUNK
    c2 = 2 * c
    npairs = hb // 2
    hblk = pl.program_id(1)
    is_first = pl.program_id(2) == 0

    @pl.when(is_first)
    def _():
        state[...] = jnp.zeros(state.shape, F32)

    for idx, src in enumerate((q_ref, k_ref, v_ref)):
        _load_with_halo(xpad, idx, src, is_first, sc)

    for idx, cw_ref in enumerate((cwq_ref, cwk_ref, cwv_ref)):
        for ci in range(sc // c):
            for i in range(hb):
                cs = slice(i * LANE, (i + 1) * LANE)
                seg = _silu(_short_conv(xpad, idx, cw_ref, ci * c, c, cs))
                if idx < 2:
                    seg = seg * lax.rsqrt(jnp.sum(seg * seg, axis=-1, keepdims=True) + EPS)
                qkvs[idx, i // 2, ci, (i % 2) * c:(i % 2 + 1) * c, :] = seg

    row = _iota((c2, c2), 0)
    col = _iota((c2, c2), 1)
    same_head = (row >= c) == (col >= c)
    causal = (row >= col) & same_head
    strict = (row > col) & same_head
    second = (row >= c).astype(jnp.int32)
    left_cols = col < c
    eye = (row == col).astype(F32)
    tril = (_iota((c, c), 0) >= _iota((c, c), 1)).astype(F32)
    lane_row = _iota((1, c2), 1)
    rows1 = _iota((c2, 1), 0)
    rows2 = _iota((2 * c2, 1), 0)
    first_head_rows = (rows2 % c2) < c
    scale = HEAD_DIM ** -0.5
    pairs = range(npairs)

    def chunk_body(ci, carry):
        r0 = pl.multiple_of(ci * c, c)
        ba = ba_ref[pl.ds(r0, c), :]
        beta_all = jax.nn.sigmoid(ba)
        g_all = -jnp.exp(gp_ref[0:1, :]) * _softplus(ba + gp_ref[1:2, :])
        gc_all = jnp.dot(tril, g_all, precision=HIGHEST,
                         preferred_element_type=F32)
        gc_stack = jnp.concatenate([gc_all, gc_all], axis=0)
        beta_stack = jnp.concatenate([beta_all, beta_all], axis=0)
        gct[...] = gc_stack.T
        h0 = [hblk * hb + 2 * p for p in pairs]
        bcol = [jnp.sum(jnp.where(col == h + second, beta_stack, 0.0), axis=-1,
                        keepdims=True) for h in h0]
        gcol = [jnp.sum(jnp.where(col == h + n_heads + second, gc_stack, 0.0), axis=-1,
                        keepdims=True) for h in h0]
        grow = [jnp.where(lane_row < c, gct[pl.ds(h + n_heads, 1), :],
                          gct[pl.ds(h + n_heads + 1, 1), :]) for h in h0]
        glast = [jnp.where(rows1 < c, g[c - 1:c, :], g[c2 - 1:c2, :]) for g in gcol]
        decay = [jnp.where(causal, jnp.exp(jnp.where(causal, gcol[p] - grow[p], 0.0)), 0.0)
                 for p in pairs]
        egc = [jnp.exp(g) for g in gcol]

        q = [qkvs[0, p, ci] * scale for p in pairs]
        k = [qkvs[1, p, ci] for p in pairs]
        v = [qkvs[2, p, ci] for p in pairs]
        kb = [k[p] * bcol[p] for p in pairs]
        kq = [_dot_nt(jnp.concatenate([kb[p], q[p]], axis=0), k[p]) for p in pairs]
        a_low = [jnp.where(strict, kq[p][:c2] * decay[p], 0.0) for p in pairs]
        attn = [kq[p][c2:] * decay[p] for p in pairs]

        t_inv = [eye - a for a in a_low]
        pw = a_low
        n = 2
        while n < c:
            pw = [_dot(x, x) for x in pw]
            t_inv = [t_inv[p] + _dot(t_inv[p], pw[p]) for p in pairs]
            n *= 2

        sol = [_dot(t_inv[p], jnp.concatenate([v[p] * bcol[p], kb[p] * egc[p]], axis=1))
               for p in pairs]
        k_dec = [k[p] * jnp.exp(glast[p] - gcol[p]) for p in pairs]
        s_prev = [state[p] for p in pairs]
        ws_qs = []
        for p in pairs:
            x = jnp.concatenate([sol[p][:, LANE:], q[p] * egc[p]], axis=0)
            x_wide = jnp.concatenate([jnp.where(first_head_rows, x, 0.0),
                                      jnp.where(first_head_rows, 0.0, x)], axis=1)
            ws_qs.append(_dot(x_wide, s_prev[p]))
        v_new = [sol[p][:, :LANE] - ws_qs[p][:c2] for p in pairs]
        o = [ws_qs[p][c2:] + _dot(attn[p], v_new[p]) for p in pairs]
        for p in pairs:
            kdt = k_dec[p].T
            kd_stack = jnp.concatenate([jnp.where(left_cols, kdt, 0.0),
                                        jnp.where(left_cols, 0.0, kdt)], axis=0)
            s_decay = jnp.where(rows2 < c2, jnp.exp(gcol[p][c - 1:c, :]),
                                jnp.exp(gcol[p][c2 - 1:c2, :]))
            state[p] = s_prev[p] * s_decay + _dot(kd_stack, v_new[p])
        for p in pairs:
            cs0 = slice((2 * p) * LANE, (2 * p + 1) * LANE)
            cs1 = slice((2 * p + 1) * LANE, (2 * p + 2) * LANE)
            zg = _silu(jnp.concatenate([z_ref[pl.ds(r0, c), cs0], z_ref[pl.ds(r0, c), cs1]],
                                       axis=0))
            op = o[p] * lax.rsqrt(jnp.mean(o[p] * o[p], axis=-1, keepdims=True) + EPS)
            op = (op * nw_ref[...] * zg).astype(o_ref.dtype)
            o_ref[pl.ds(r0, c), cs0] = op[:c]
            o_ref[pl.ds(r0, c), cs1] = op[c:]
        return carry

    lax.fori_loop(0, sc // c, chunk_body, 0)


def _deltanet(proj3, ba3, conv_w, a_log, dt_bias, norm_w, n_heads, col0):
    b, s, _ = proj3.shape
    assert n_heads % 2 == 0 and 2 * n_heads <= LANE
    hb = _largest_tile(n_heads, 12, 2)
    width = hb * LANE
    nhb = n_heads // hb
    sc = _largest_tile(s, 256, DN_CHUNK)
    base = col0 // width
    assert col0 % width == 0
    gp = jnp.zeros((2, LANE), F32)
    gp = gp.at[0, n_heads:2 * n_heads].set(a_log).at[1, n_heads:2 * n_heads].set(dt_bias)

    def col_spec(group):
        return pl.BlockSpec((None, sc, width),
                            lambda bi, hi, si, g=group: (bi, si, base + g * nhb + hi))

    def cw_spec(group):
        return pl.BlockSpec((SHORT_CONV, width), lambda bi, hi, si, g=group: (0, g * nhb + hi))

    return pl.pallas_call(
        functools.partial(_deltanet_kernel, sc=sc, hb=hb, n_heads=n_heads),
        grid=(b, nhb, s // sc),
        in_specs=[col_spec(0), col_spec(1), col_spec(2), col_spec(3),
                  pl.BlockSpec((None, sc, LANE), lambda bi, hi, si: (bi, si, 0)),
                  cw_spec(0), cw_spec(1), cw_spec(2),
                  pl.BlockSpec((2, LANE), lambda bi, hi, si: (0, 0)),
                  pl.BlockSpec((1, LANE), lambda bi, hi, si: (0, 0))],
        out_specs=pl.BlockSpec((None, sc, width), lambda bi, hi, si: (bi, si, hi)),
        out_shape=jax.ShapeDtypeStruct((b, s, n_heads * LANE), BF16),
        scratch_shapes=[pltpu.VMEM((3, sc + HALO, width), F32),
                        pltpu.VMEM((3, hb // 2, sc // DN_CHUNK, 2 * DN_CHUNK, LANE), F32),
                        pltpu.VMEM((LANE, LANE), F32),
                        pltpu.VMEM((hb // 2, 2 * HEAD_DIM, HEAD_DIM), F32)],
        compiler_params=_params("parallel", "parallel", "arbitrary"),
        name="gated_deltanet",
    )(proj3, proj3, proj3, proj3, ba3, conv_w, conv_w, conv_w, gp, norm_w.reshape(1, LANE))


def _rglru_kernel(x_ref, g_ref, cw_ref, cb_ref, wa_ref, ba_ref, wx_ref, bx_ref, lam_ref,
                  nw_ref, o_ref, xpad, abuf, bbuf, hcarry, *, sc, gb):
    c = SCAN_CHUNK
    is_first = pl.program_id(2) == 0

    @pl.when(is_first)
    def _():
        hcarry[...] = jnp.zeros(hcarry.shape, F32)

    _load_with_halo(xpad, 0, x_ref, is_first, sc)

    rows = 2 * c
    for ci in range(sc // rows):
        xc = _short_conv(xpad, 0, cw_ref, ci * rows, rows) + cb_ref[...]
        for g in range(gb):
            cs = slice(g * LANE, (g + 1) * LANE)
            xg = xc[:, cs]
            r = jax.nn.sigmoid(_dot(xg, wa_ref[g]) + ba_ref[:, cs])
            ig = jax.nn.sigmoid(_dot(xg, wx_ref[g]) + bx_ref[:, cs])
            a = jnp.exp(-LRU_C * r * _softplus(-lam_ref[:, cs]))
            abuf[ci * rows:(ci + 1) * rows, cs] = a
            bbuf[ci * rows:(ci + 1) * rows, cs] = (
                jnp.sqrt((1.0 - a) * (1.0 + a)) * (ig * xg))

    row = _iota((HALO, LANE), 0)

    def chunk_body(ci, carry):
        r0 = pl.multiple_of(ci * c, c)
        for g in range(gb):
            cs = slice(g * LANE, (g + 1) * LANE)
            a = abuf[pl.ds(r0, c), cs]
            bb = bbuf[pl.ds(r0, c), cs]
            last = jnp.broadcast_to(hcarry[:, cs], (HALO, LANE))
            tiles = []
            for v in range(c // HALO):
                av = a[v * HALO:(v + 1) * HALO]
                bv = bb[v * HALO:(v + 1) * HALO]
                d = 1
                while d < HALO:
                    keep = row >= d
                    a_sh = jnp.where(keep, pltpu.roll(av, d, 0), 1.0)
                    b_sh = jnp.where(keep, pltpu.roll(bv, d, 0), 0.0)
                    bv = av * b_sh + bv
                    av = av * a_sh
                    d *= 2
                hv = bv + av * last
                last = jnp.broadcast_to(hv[HALO - 1:HALO, :], (HALO, LANE))
                tiles.append(hv)
            h = jnp.concatenate(tiles, axis=0)
            hcarry[:, cs] = last[0:1, :]
            y = h * jax.nn.gelu(g_ref[pl.ds(r0, c), cs])
            y = y * lax.rsqrt(jnp.mean(y * y, axis=-1, keepdims=True) + EPS)
            o_ref[pl.ds(r0, c), cs] = (y * nw_ref[:, cs]).astype(o_ref.dtype)
        return carry

    lax.fori_loop(0, sc // c, chunk_body, 0)


def _rglru(proj3, conv_w, conv_b, w_a, b_a, w_x, b_x, lam, norm_w, col_x, col_g):
    b, s, _ = proj3.shape
    nblk = w_a.shape[0]
    gb = 4 if nblk % 4 == 0 else (3 if nblk % 3 == 0 else 1)
    width = gb * LANE
    lw = nblk * LANE
    sc = _largest_tile(s, 512, 2 * SCAN_CHUNK)
    assert col_x % width == 0 and col_g % width == 0
    bx0, bg0 = col_x // width, col_g // width
    vec = lambda: pl.BlockSpec((1, width), lambda bi, gi, si: (0, gi))
    mat = lambda: pl.BlockSpec((gb, LANE, LANE), lambda bi, gi, si: (gi, 0, 0))
    return pl.pallas_call(
        functools.partial(_rglru_kernel, sc=sc, gb=gb),
        grid=(b, nblk // gb, s // sc),
        in_specs=[pl.BlockSpec((None, sc, width), lambda bi, gi, si: (bi, si, bx0 + gi)),
                  pl.BlockSpec((None, sc, width), lambda bi, gi, si: (bi, si, bg0 + gi)),
                  pl.BlockSpec((SHORT_CONV, width), lambda bi, gi, si: (0, gi)),
                  vec(), mat(), vec(), mat(), vec(), vec(), vec()],
        out_specs=pl.BlockSpec((None, sc, width), lambda bi, gi, si: (bi, si, gi)),
        out_shape=jax.ShapeDtypeStruct((b, s, lw), BF16),
        scratch_shapes=[pltpu.VMEM((1, sc + HALO, width), F32),
                        pltpu.VMEM((sc, width), F32),
                        pltpu.VMEM((sc, width), F32),
                        pltpu.VMEM((1, width), F32)],
        compiler_params=_params("parallel", "parallel", "arbitrary"),
        name="rglru",
    )(proj3, proj3, conv_w, conv_b.reshape(1, lw), w_a, b_a.reshape(1, lw), w_x,
      b_x.reshape(1, lw), lam.reshape(1, lw), norm_w.reshape(1, lw))


def _sg_kernel(u_ref, v_ref, lnw_ref, lnb_ref, ws_ref, bst_ref, nw_ref, o_ref, *, sc,
               groups):
    t = SG_CHUNK
    tril = _iota((t, t), 0) >= _iota((t, t), 1)
    for ci in range(sc // t):
        rs = slice(ci * t, (ci + 1) * t)
        v = jax.nn.gelu(v_ref[rs, :])
        mu = jnp.mean(v, axis=-1, keepdims=True)
        vc = v - mu
        var = jnp.mean(vc * vc, axis=-1, keepdims=True)
        vn = vc * lax.rsqrt(var + EPS) * lnw_ref[...] + lnb_ref[...]
        for g in range(groups):
            cs = slice(g * LANE, (g + 1) * LANE)
            w_causal = jnp.where(tril, ws_ref[g], 0.0)
            z = _dot(w_causal, vn[:, cs]) + bst_ref[:, g:g + 1]
            y = jax.nn.gelu(u_ref[rs, cs]) * z
            y = y * lax.rsqrt(jnp.mean(y * y, axis=-1, keepdims=True) + EPS)
            o_ref[rs, cs] = (y * nw_ref[:, cs]).astype(o_ref.dtype)


def _spatial_gating(proj3, ln_w, ln_b, w_s, b_s, norm_w, col_u, col_v):
    b, s, _ = proj3.shape
    groups = w_s.shape[0]
    width = groups * LANE
    sc = _largest_tile(s, 512, SG_CHUNK)
    assert col_u % width == 0 and col_v % width == 0
    bu0, bv0 = col_u // width, col_v // width
    vec = lambda: pl.BlockSpec((1, width), lambda bi, si: (0, 0))
    return pl.pallas_call(
        functools.partial(_sg_kernel, sc=sc, groups=groups),
        grid=(b, s // sc),
        in_specs=[pl.BlockSpec((None, sc, width), lambda bi, si: (bi, si, bu0)),
                  pl.BlockSpec((None, sc, width), lambda bi, si: (bi, si, bv0)),
                  vec(), vec(),
                  pl.BlockSpec((groups, SG_CHUNK, SG_CHUNK), lambda bi, si: (0, 0, 0)),
                  pl.BlockSpec((SG_CHUNK, groups), lambda bi, si: (0, 0)),
                  vec()],
        out_specs=pl.BlockSpec((None, sc, width), lambda bi, si: (bi, si, 0)),
        out_shape=jax.ShapeDtypeStruct((b, s, width), BF16),
        compiler_params=_params("parallel", "parallel"),
        name="spatial_gating",
    )(proj3, proj3, ln_w.reshape(1, width), ln_b.reshape(1, width), w_s, b_s.T,
      norm_w.reshape(1, width))


def kernel(x, norm_mix, w_in, dn_conv_w, dn_a_log, dn_dt_bias, dn_norm_w, lru_conv_w, lru_conv_b, lru_w_a, lru_b_a, lru_w_x, lru_b_x, lru_lambda, lru_norm_w, sg_ln_w, sg_ln_b, sg_w_s, sg_b_s, sg_norm_w, w_out, norm_ffn, w_up, ffn_conv_w, ffn_conv_b, w_down, norm_final):
    bsz, seq, d = x.shape
    depth = w_in.shape[0]
    n_heads = dn_a_log.shape[1]
    dn_w = n_heads * HEAD_DIM
    lru_w = lru_lambda.shape[1]
    sg_w = sg_ln_w.shape[1]
    m = bsz * seq
    gate0 = 4 * dn_w
    tail0 = gate0 + 2 * n_heads
    tail_cols = 2 * lru_w + 2 * sg_w

    x2 = x.reshape(m, d)
    for l in range(depth):
        w_tail = w_in[l, :, tail0:]
        w_gate = jnp.pad(w_in[l, :, gate0:tail0], ((0, 0), (0, LANE - 2 * n_heads)))

        h = _rmsnorm(x2, norm_mix[l], BF16)
        proj_a = _matmul_wstat(h, w_in, layer=l, n=gate0, name="in_proj_qkvz")
        proj_b = _matmul_wstat(h, w_tail, name="in_proj_tail")
        ba = _matmul_wstat(h, w_gate, name="in_proj_gates")
        proj_a3 = proj_a.reshape(bsz, seq, gate0)
        proj_b3 = proj_b.reshape(bsz, seq, tail_cols)
        ba3 = ba.reshape(bsz, seq, LANE)

        y_a = _deltanet(proj_a3, ba3, dn_conv_w[l], dn_a_log[l], dn_dt_bias[l],
                        dn_norm_w[l], n_heads, 0)
        y_b = _rglru(proj_b3, lru_conv_w[l], lru_conv_b[l], lru_w_a[l], lru_b_a[l],
                     lru_w_x[l], lru_b_x[l], lru_lambda[l], lru_norm_w[l], 0, lru_w)
        y_c = _spatial_gating(proj_b3, sg_ln_w[l], sg_ln_b[l], sg_w_s[l], sg_b_s[l],
                              sg_norm_w[l], 2 * lru_w, 2 * lru_w + sg_w)
        mix = jnp.concatenate([y_a, y_b, y_c], axis=-1).reshape(m, d)
        x2 = _matmul_wstat(mix, w_out, layer=l, residual=x2, name="out_proj")

        h = _rmsnorm(x2, norm_ffn[l], BF16)
        act, w_down_bf16 = _ffn_up(h, w_up, w_down, l, ffn_conv_w[l], ffn_conv_b[l], seq)
        x2 = _matmul(act, w_down_bf16, residual=x2, tm_cap=1024, tk_cap=5504,
                     name="down_proj")
    return _rmsnorm(x2, norm_final, F32).reshape(bsz, seq, d)
```

```python
import functools

import jax
import jax.numpy as jnp
from jax import lax
from jax.experimental import pallas as pl
from jax.experimental.pallas import tpu as pltpu

F32 = jnp.float32
BF16 = jnp.bfloat16
HIGHEST = lax.Precision.HIGHEST

EPS = 1e-6
LANE = 128
HALO = 8
HEAD_DIM = LANE
DN_CHUNK = 64
DN_SEQ_TILE = 128
SG_CHUNK = 128
SCAN_CHUNK = 64
SHORT_CONV = 4
LRU_C = 8.0
VMEM_LIMIT_BYTES = 56 * 1024 * 1024


def _params(*semantics):
    return pltpu.CompilerParams(dimension_semantics=semantics,
                                vmem_limit_bytes=VMEM_LIMIT_BYTES)


def _largest_tile(n, cap, unit):
    best = None
    t = unit
    while t <= min(n, cap):
        if n % t == 0:
            best = t
        t += unit
    assert best is not None, (n, cap, unit)
    return best


def _dot(a, b):
    return jnp.dot(a.astype(BF16), b.astype(BF16), preferred_element_type=F32)


def _dot_nt(a, b):
    return lax.dot_general(a.astype(BF16), b.astype(BF16), (((1,), (1,)), ((), ())),
                           preferred_element_type=F32)


def _softplus(x):
    return jnp.maximum(x, 0.0) + jnp.log1p(jnp.exp(-jnp.abs(x)))


def _silu(x):
    return x * jax.nn.sigmoid(x)


def _iota(shape, dim):
    return lax.broadcasted_iota(jnp.int32, shape, dim)


def _rmsnorm_kernel(x_ref, w_ref, o_ref):
    x = x_ref[...]
    ms = jnp.mean(x * x, axis=-1, keepdims=True)
    o_ref[...] = (x * lax.rsqrt(ms + EPS) * w_ref[...]).astype(o_ref.dtype)


def _rmsnorm(x2, w, out_dtype):
    m, d = x2.shape
    tm = _largest_tile(m, 256, 16)
    return pl.pallas_call(
        _rmsnorm_kernel,
        grid=(m // tm,),
        in_specs=[pl.BlockSpec((tm, d), lambda i: (i, 0)),
                  pl.BlockSpec((1, d), lambda i: (0, 0))],
        out_specs=pl.BlockSpec((tm, d), lambda i: (i, 0)),
        out_shape=jax.ShapeDtypeStruct((m, d), out_dtype),
        compiler_params=_params("parallel"),
        name="rmsnorm",
    )(x2, w.reshape(1, d))


def _matmul_kernel(*refs, nk, has_res):
    if has_res:
        a_ref, w_ref, res_ref, o_ref = refs[:4]
    else:
        a_ref, w_ref, o_ref = refs[:3]
        res_ref = None
    part = jnp.dot(a_ref[...], w_ref[...], preferred_element_type=F32)

    def finish(acc):
        if res_ref is not None:
            acc = acc + res_ref[...]
        o_ref[...] = acc.astype(o_ref.dtype)

    if nk == 1:
        finish(part)
        return
    acc_ref = refs[-1]
    k = pl.program_id(2)

    @pl.when(k == 0)
    def _():
        acc_ref[...] = part

    @pl.when(k > 0)
    def _():
        acc_ref[...] += part

    @pl.when(k == nk - 1)
    def _():
        finish(acc_ref[...])


def _matmul(a, w, *, residual=None, out_dtype=F32, tm_cap=512, tn_cap=512, tk_cap=4096,
            name="matmul"):
    m, k = a.shape
    n = w.shape[1]
    tm = _largest_tile(m, tm_cap, 16)
    tn = _largest_tile(n, tn_cap, LANE)
    tk = _largest_tile(k, tk_cap, LANE)
    nk = k // tk
    in_specs = [pl.BlockSpec((tm, tk), lambda j, i, kk: (i, kk)),
                pl.BlockSpec((tk, tn), lambda j, i, kk: (kk, j))]
    args = [a, w]
    if residual is not None:
        in_specs.append(pl.BlockSpec((tm, tn), lambda j, i, kk: (i, j)))
        args.append(residual)
    scratch = [pltpu.VMEM((tm, tn), F32)] if nk > 1 else []
    return pl.pallas_call(
        functools.partial(_matmul_kernel, nk=nk, has_res=residual is not None),
        grid=(n // tn, m // tm, nk),
        in_specs=in_specs,
        out_specs=pl.BlockSpec((tm, tn), lambda j, i, kk: (i, j)),
        out_shape=jax.ShapeDtypeStruct((m, n), out_dtype),
        scratch_shapes=scratch,
        compiler_params=_params("parallel", "parallel", "arbitrary"),
        name=name,
    )(*args)


CAST_ROWS = 256
IN_PROJ_TILE = 512


def _cast_rows(src_ref, dst_ref):
    step = min(CAST_ROWS, src_ref.shape[0])
    assert src_ref.shape[0] % step == 0

    def body(r, carry):
        rows = pl.ds(pl.multiple_of(r * step, step), step)
        dst_ref[rows, :] = src_ref[rows, :].astype(dst_ref.dtype)
        return carry

    lax.fori_loop(0, src_ref.shape[0] // step, body, 0)


def _weight_spec(w, layer, k, tn, col_block0=0):
    if layer is None:
        return pl.BlockSpec((k, tn), lambda j, i: (0, col_block0 + j))
    return pl.BlockSpec((None, k, tn), lambda j, i: (layer, 0, col_block0 + j))


def _matmul_wstat_kernel(*refs, has_res):
    if has_res:
        a_ref, w_ref, res_ref, o_ref, wb_ref = refs
    else:
        a_ref, w_ref, o_ref, wb_ref = refs
        res_ref = None

    @pl.when(pl.program_id(1) == 0)
    def _():
        _cast_rows(w_ref, wb_ref)

    acc = jnp.dot(a_ref[...], wb_ref[...], preferred_element_type=F32)
    if res_ref is not None:
        acc = acc + res_ref[...]
    o_ref[...] = acc


def _matmul_wstat(a, w, *, layer=None, n=None, residual=None, name="matmul"):
    m, k = a.shape
    n = w.shape[-1] if n is None else n
    assert k % CAST_ROWS == 0
    tm = _largest_tile(m, 1024, 16)
    tn = _largest_tile(n, 512, LANE)
    in_specs = [pl.BlockSpec((tm, k), lambda j, i: (i, 0)), _weight_spec(w, layer, k, tn)]
    args = [a, w]
    if residual is not None:
        in_specs.append(pl.BlockSpec((tm, tn), lambda j, i: (i, j)))
        args.append(residual)
    return pl.pallas_call(
        functools.partial(_matmul_wstat_kernel, has_res=residual is not None),
        grid=(n // tn, m // tm),
        in_specs=in_specs,
        out_specs=pl.BlockSpec((tm, tn), lambda j, i: (i, j)),
        out_shape=jax.ShapeDtypeStruct((m, n), F32),
        scratch_shapes=[pltpu.VMEM((k, tn), BF16)],
        compiler_params=_params("arbitrary", "arbitrary"),
        name=name,
    )(*args)


def _in_proj_kernel(a_ref, wt_ref, o_ref, wb_ref):
    @pl.when(pl.program_id(1) == 0)
    def _():
        _cast_rows(wt_ref.at[0], wb_ref)

    o_ref[...] = lax.dot_general(a_ref[...], wb_ref[...], (((1,), (1,)), ((), ())),
                                 preferred_element_type=F32)


def _in_proj(a, w_t, layer, row_start, n, tn, name):
    m, k = a.shape
    assert n % tn == 0
    tm = _largest_tile(m, 1024, 16)
    return pl.pallas_call(
        _in_proj_kernel,
        grid=(n // tn, m // tm),
        in_specs=[pl.BlockSpec((tm, k), lambda j, i: (i, 0)),
                  pl.BlockSpec((pl.Element(1), pl.Element(tn), pl.Element(k)),
                               lambda j, i: (layer, row_start(j), 0))],
        out_specs=pl.BlockSpec((tm, tn), lambda j, i: (i, j)),
        out_shape=jax.ShapeDtypeStruct((m, n), F32),
        scratch_shapes=[pltpu.VMEM((tn, k), BF16)],
        compiler_params=_params("arbitrary", "arbitrary"),
        name=name,
    )(a, w_t)


FFN_ROW_BLOCK = 512


def _ffn_up_kernel(a_ref, wg_ref, wu_ref, cwg_ref, cwu_ref, bg_ref, bu_ref, wd_ref,
                   o_ref, wdb_ref, wgb, wub, gbuf, ubuf, *, tm, rb, tiles_per_seq):
    i = pl.program_id(1)
    taps = cwg_ref.shape[0]

    wdb_ref[...] = wd_ref[...].astype(wdb_ref.dtype)

    @pl.when(i == 0)
    def _():
        _cast_rows(wg_ref, wgb)
        _cast_rows(wu_ref, wub)

    first = (i % tiles_per_seq) == 0
    for buf in (gbuf, ubuf):
        @pl.when(first)
        def _():
            buf[0:HALO, :] = jnp.zeros((HALO, buf.shape[1]), F32)

        @pl.when(jnp.logical_not(first))
        def _():
            buf[0:HALO, :] = buf[tm:tm + HALO, :]

    def conv(buf, cw_ref, r0):
        acc = None
        for j in range(taps):
            term = cw_ref[j:j + 1, :] * buf[pl.ds(r0 + HALO - taps + 1 + j, rb), :]
            acc = term if acc is None else acc + term
        return acc

    for r in range(tm // rb):
        a = a_ref[r * rb:(r + 1) * rb, :]
        gbuf[HALO + r * rb:HALO + (r + 1) * rb, :] = jnp.dot(
            a, wgb[...], preferred_element_type=F32)
        ubuf[HALO + r * rb:HALO + (r + 1) * rb, :] = jnp.dot(
            a, wub[...], preferred_element_type=F32)
        g = conv(gbuf, cwg_ref, r * rb) + bg_ref[...]
        u = conv(ubuf, cwu_ref, r * rb) + bu_ref[...]
        o_ref[r * rb:(r + 1) * rb, :] = (_silu(g) * u).astype(o_ref.dtype)


def _ffn_up(h, w_up, w_down, layer, conv_w, conv_b, seq):
    m, k = h.shape
    f = w_up.shape[-1] // 2
    d_out = w_down.shape[-1]
    taps = conv_w.shape[0]
    tm = _largest_tile(seq, 1024, FFN_ROW_BLOCK)
    tf = _largest_tile(f, 256, LANE)
    nf = f // tf
    row_tiles = m // tm
    slab = f // (nf * row_tiles)
    assert k % CAST_ROWS == 0 and slab * nf * row_tiles == f and slab % 16 == 0
    conv_b = conv_b.reshape(1, 2 * f)
    return pl.pallas_call(
        functools.partial(_ffn_up_kernel, tm=tm, rb=FFN_ROW_BLOCK, tiles_per_seq=seq // tm),
        grid=(nf, row_tiles),
        in_specs=[pl.BlockSpec((tm, k), lambda j, i: (i, 0)),
                  _weight_spec(w_up, layer, k, tf),
                  _weight_spec(w_up, layer, k, tf, nf),
                  pl.BlockSpec((taps, tf), lambda j, i: (0, j)),
                  pl.BlockSpec((taps, tf), lambda j, i: (0, nf + j)),
                  pl.BlockSpec((1, tf), lambda j, i: (0, j)),
                  pl.BlockSpec((1, tf), lambda j, i: (0, nf + j)),
                  pl.BlockSpec((None, slab, d_out),
                               lambda j, i: (layer, j * row_tiles + i, 0))],
        out_specs=[pl.BlockSpec((tm, tf), lambda j, i: (i, j)),
                   pl.BlockSpec((slab, d_out), lambda j, i: (j * row_tiles + i, 0))],
        out_shape=[jax.ShapeDtypeStruct((m, f), BF16),
                   jax.ShapeDtypeStruct((f, d_out), BF16)],
        scratch_shapes=[pltpu.VMEM((k, tf), BF16),
                        pltpu.VMEM((k, tf), BF16),
                        pltpu.VMEM((tm + HALO, tf), F32),
                        pltpu.VMEM((tm + HALO, tf), F32)],
        compiler_params=_params("arbitrary", "arbitrary"),
        name="ffn_up_conv_gate",
    )(h, w_up, w_up, conv_w, conv_w, conv_b, conv_b, w_down)


def _load_with_halo(xpad, idx, src_ref, is_first, rows):
    width = xpad.shape[-1]

    @pl.when(is_first)
    def _():
        xpad[idx, 0:HALO, :] = jnp.zeros((HALO, width), F32)

    @pl.when(jnp.logical_not(is_first))
    def _():
        xpad[idx, 0:HALO, :] = xpad[idx, rows:rows + HALO, :]

    xpad[idx, HALO:, :] = src_ref[...]


def _short_conv(xpad, idx, cw_ref, r0, rows, cs=slice(None)):
    taps = cw_ref.shape[0]
    acc = None
    for j in range(taps):
        term = cw_ref[j:j + 1, cs] * xpad[idx, pl.ds(r0 + HALO - taps + 1 + j, rows), cs]
        acc = term if acc is None else acc + term
    return acc


def _deltanet_kernel(q_ref, k_ref, v_ref, z_ref, ba_ref, cwq_ref, cwk_ref, cwv_ref,
                     gp_ref, nw_ref, o_ref, xpad, qkvs, gct, state, *, sc, hb, nb, n_heads):
    c = DN_CHUNK
    c2 = 2 * c
    ppb = hb // 2
    npairs = nb * ppb
    hblk = pl.program_id(0)
    is_first = pl.program_id(1) == 0

    @pl.when(is_first)
    def _():
        state[...] = jnp.zeros(state.shape, F32)

    for idx, src in enumerate((q_ref, k_ref, v_ref)):
        for b in range(nb):
            _load_with_halo(xpad, idx * nb + b, src.at[b], is_first, sc)

    for idx, cw_ref in enumerate((cwq_ref, cwk_ref, cwv_ref)):
        for b in range(nb):
            for ci in range(sc // c):
                for i in range(hb):
                    cs = slice(i * LANE, (i + 1) * LANE)
                    seg = _silu(_short_conv(xpad, idx * nb + b, cw_ref, ci * c, c, cs))
                    if idx < 2:
                        seg = seg * lax.rsqrt(
                            jnp.sum(seg * seg, axis=-1, keepdims=True) + EPS)
                    qkvs[idx, b * ppb + i // 2, ci, (i % 2) * c:(i % 2 + 1) * c, :] = seg

    row = _iota((c2, c2), 0)
    col = _iota((c2, c2), 1)
    same_head = (row >= c) == (col >= c)
    causal = (row >= col) & same_head
    strict = (row > col) & same_head
    second = (row >= c).astype(jnp.int32)
    left_cols = col < c
    eye = (row == col).astype(F32)
    tril = (_iota((c, c), 0) >= _iota((c, c), 1)).astype(F32)
    lane_row = _iota((1, c2), 1)
    rows1 = _iota((c2, 1), 0)
    rows2 = _iota((2 * c2, 1), 0)
    first_head_rows = (rows2 % c2) < c
    scale = HEAD_DIM ** -0.5
    pairs = range(npairs)

    def chunk_body(ci, carry):
        r0 = pl.multiple_of(ci * c, c)
        gc_stack, beta_stack = [], []
        for b in range(nb):
            ba = ba_ref[b, pl.ds(r0, c), :]
            beta_all = jax.nn.sigmoid(ba)
            g_all = -jnp.exp(gp_ref[0:1, :]) * _softplus(ba + gp_ref[1:2, :])
            gc_all = jnp.dot(tril, g_all, precision=HIGHEST,
                             preferred_element_type=F32)
            gc_stack.append(jnp.concatenate([gc_all, gc_all], axis=0))
            beta_stack.append(jnp.concatenate([beta_all, beta_all], axis=0))
            gct[b] = gc_stack[b].T
        bof = [u // ppb for u in pairs]
        h0 = [hblk * hb + 2 * (u % ppb) for u in pairs]
        bcol = [jnp.sum(jnp.where(col == h0[u] + second, beta_stack[bof[u]], 0.0), axis=-1,
                        keepdims=True) for u in pairs]
        gcol = [jnp.sum(jnp.where(col == h0[u] + n_heads + second, gc_stack[bof[u]], 0.0),
                        axis=-1, keepdims=True) for u in pairs]
        grow = [jnp.where(lane_row < c, gct[bof[u], pl.ds(h0[u] + n_heads, 1), :],
                          gct[bof[u], pl.ds(h0[u] + n_heads + 1, 1), :])
                for u in pairs]
        glast = [jnp.where(rows1 < c, g[c - 1:c, :], g[c2 - 1:c2, :]) for g in gcol]
        decay = [jnp.where(causal, jnp.exp(jnp.where(causal, gcol[p] - grow[p], 0.0)), 0.0)
                 for p in pairs]
        egc = [jnp.exp(g) for g in gcol]

        q = [qkvs[0, p, ci] * scale for p in pairs]
        k = [qkvs[1, p, ci] for p in pairs]
        v = [qkvs[2, p, ci] for p in pairs]
        kb = [k[p] * bcol[p] for p in pairs]
        kq = [_dot_nt(jnp.concatenate([kb[p], q[p]], axis=0), k[p]) for p in pairs]
        a_low = [jnp.where(strict, kq[p][:c2] * decay[p], 0.0) for p in pairs]
        attn = [kq[p][c2:] * decay[p] for p in pairs]

        t_inv = [eye - a for a in a_low]
        pw = a_low
        n = 2
        while n < c:
            pw = [_dot(x, x) for x in pw]
            t_inv = [t_inv[p] + _dot(t_inv[p], pw[p]) for p in pairs]
            n *= 2

        sol = [_dot(t_inv[p], jnp.concatenate([v[p] * bcol[p], kb[p] * egc[p]], axis=1))
               for p in pairs]
        k_dec = [k[p] * jnp.exp(glast[p] - gcol[p]) for p in pairs]
        s_prev = [state[p] for p in pairs]
        ws_qs = []
        for p in pairs:
            x = jnp.concatenate([sol[p][:, LANE:], q[p] * egc[p]], axis=0)
            x_wide = jnp.concatenate([jnp.where(first_head_rows, x, 0.0),
                                      jnp.where(first_head_rows, 0.0, x)], axis=1)
            ws_qs.append(_dot(x_wide, s_prev[p]))
        v_new = [sol[p][:, :LANE] - ws_qs[p][:c2] for p in pairs]
        o = [ws_qs[p][c2:] + _dot(attn[p], v_new[p]) for p in pairs]
        for p in pairs:
            kdt = k_dec[p].T
            kd_stack = jnp.concatenate([jnp.where(left_cols, kdt, 0.0),
                                        jnp.where(left_cols, 0.0, kdt)], axis=0)
            s_decay = jnp.where(rows2 < c2, jnp.exp(gcol[p][c - 1:c, :]),
                                jnp.exp(gcol[p][c2 - 1:c2, :]))
            state[p] = s_prev[p] * s_decay + _dot(kd_stack, v_new[p])
        for p in pairs:
            b, hp = bof[p], p % ppb
            cs0 = slice((2 * hp) * LANE, (2 * hp + 1) * LANE)
            cs1 = slice((2 * hp + 1) * LANE, (2 * hp + 2) * LANE)
            zg = _silu(jnp.concatenate([z_ref[b, pl.ds(r0, c), cs0],
                                        z_ref[b, pl.ds(r0, c), cs1]], axis=0))
            op = o[p] * lax.rsqrt(jnp.mean(o[p] * o[p], axis=-1, keepdims=True) + EPS)
            op = (op * nw_ref[...] * zg).astype(o_ref.dtype)
            o_ref[b, pl.ds(r0, c), cs0] = op[:c]
            o_ref[b, pl.ds(r0, c), cs1] = op[c:]
        return carry

    lax.fori_loop(0, sc // c, chunk_body, 0)


def _deltanet(proj3, ba3, conv_w, a_log, dt_bias, norm_w, n_heads, col0):
    b, s, _ = proj3.shape
    assert n_heads % 2 == 0 and 2 * n_heads <= LANE
    hb = _largest_tile(n_heads, 12, 2)
    width = hb * LANE
    nhb = n_heads // hb
    sc = _largest_tile(s, DN_SEQ_TILE, DN_CHUNK)
    base = col0 // width
    assert col0 % width == 0
    npairs = b * hb // 2
    gp = jnp.zeros((2, LANE), F32)
    gp = gp.at[0, n_heads:2 * n_heads].set(a_log).at[1, n_heads:2 * n_heads].set(dt_bias)

    def col_spec(group):
        return pl.BlockSpec((b, sc, width),
                            lambda hi, si, g=group: (0, si, base + g * nhb + hi))

    def cw_spec(group):
        return pl.BlockSpec((SHORT_CONV, width), lambda hi, si, g=group: (0, g * nhb + hi))

    return pl.pallas_call(
        functools.partial(_deltanet_kernel, sc=sc, hb=hb, nb=b, n_heads=n_heads),
        grid=(nhb, s // sc),
        in_specs=[col_spec(0), col_spec(1), col_spec(2), col_spec(3),
                  pl.BlockSpec((b, sc, LANE), lambda hi, si: (0, si, 0)),
                  cw_spec(0), cw_spec(1), cw_spec(2),
                  pl.BlockSpec((2, LANE), lambda hi, si: (0, 0)),
                  pl.BlockSpec((1, LANE), lambda hi, si: (0, 0))],
        out_specs=pl.BlockSpec((b, sc, width), lambda hi, si: (0, si, hi)),
        out_shape=jax.ShapeDtypeStruct((b, s, n_heads * LANE), BF16),
        scratch_shapes=[pltpu.VMEM((3 * b, sc + HALO, width), F32),
                        pltpu.VMEM((3, npairs, sc // DN_CHUNK, 2 * DN_CHUNK, LANE), F32),
                        pltpu.VMEM((b, LANE, LANE), F32),
                        pltpu.VMEM((npairs, 2 * HEAD_DIM, HEAD_DIM), F32)],
        compiler_params=_params("parallel", "arbitrary"),
        name="gated_deltanet",
    )(proj3, proj3, proj3, proj3, ba3, conv_w, conv_w, conv_w, gp, norm_w.reshape(1, LANE))


def _rglru_kernel(x_ref, g_ref, cw_ref, cb_ref, wa_ref, ba_ref, wx_ref, bx_ref, lam_ref,
                  nw_ref, o_ref, xpad, abuf, bbuf, hcarry, *, sc, gb):
    c = SCAN_CHUNK
    is_first = pl.program_id(2) == 0

    @pl.when(is_first)
    def _():
        hcarry[...] = jnp.zeros(hcarry.shape, F32)

    _load_with_halo(xpad, 0, x_ref, is_first, sc)

    rows = 2 * c
    for ci in range(sc // rows):
        xc = _short_conv(xpad, 0, cw_ref, ci * rows, rows) + cb_ref[...]
        for g in range(gb):
            cs = slice(g * LANE, (g + 1) * LANE)
            xg = xc[:, cs]
            r = jax.nn.sigmoid(_dot(xg, wa_ref[g]) + ba_ref[:, cs])
            ig = jax.nn.sigmoid(_dot(xg, wx_ref[g]) + bx_ref[:, cs])
            a = jnp.exp(-LRU_C * r * _softplus(-lam_ref[:, cs]))
            abuf[ci * rows:(ci + 1) * rows, cs] = a
            bbuf[ci * rows:(ci + 1) * rows, cs] = (
                jnp.sqrt((1.0 - a) * (1.0 + a)) * (ig * xg))

    row = _iota((HALO, LANE), 0)

    def chunk_body(ci, carry):
        r0 = pl.multiple_of(ci * c, c)
        for g in range(gb):
            cs = slice(g * LANE, (g + 1) * LANE)
            a = abuf[pl.ds(r0, c), cs]
            bb = bbuf[pl.ds(r0, c), cs]
            last = jnp.broadcast_to(hcarry[:, cs], (HALO, LANE))
            tiles = []
            for v in range(c // HALO):
                av = a[v * HALO:(v + 1) * HALO]
                bv = bb[v * HALO:(v + 1) * HALO]
                d = 1
                while d < HALO:
                    keep = row >= d
                    a_sh = jnp.where(keep, pltpu.roll(av, d, 0), 1.0)
                    b_sh = jnp.where(keep, pltpu.roll(bv, d, 0), 0.0)
                    bv = av * b_sh + bv
                    av = av * a_sh
                    d *= 2
                hv = bv + av * last
                last = jnp.broadcast_to(hv[HALO - 1:HALO, :], (HALO, LANE))
                tiles.append(hv)
            h = jnp.concatenate(tiles, axis=0)
            hcarry[:, cs] = last[0:1, :]
            y = h * jax.nn.gelu(g_ref[pl.ds(r0, c), cs])
            y = y * lax.rsqrt(jnp.mean(y * y, axis=-1, keepdims=True) + EPS)
            o_ref[pl.ds(r0, c), cs] = (y * nw_ref[:, cs]).astype(o_ref.dtype)
        return carry

    lax.fori_loop(0, sc // c, chunk_body, 0)


def _rglru(proj3, conv_w, conv_b, w_a, b_a, w_x, b_x, lam, norm_w, col_x, col_g):
    b, s, _ = proj3.shape
    nblk = w_a.shape[0]
    gb = 4 if nblk % 4 == 0 else (3 if nblk % 3 == 0 else 1)
    width = gb * LANE
    lw = nblk * LANE
    sc = _largest_tile(s, 512, 2 * SCAN_CHUNK)
    assert col_x % width == 0 and col_g % width == 0
    bx0, bg0 = col_x // width, col_g // width
    vec = lambda: pl.BlockSpec((1, width), lambda bi, gi, si: (0, gi))
    mat = lambda: pl.BlockSpec((gb, LANE, LANE), lambda bi, gi, si: (gi, 0, 0))
    return pl.pallas_call(
        functools.partial(_rglru_kernel, sc=sc, gb=gb),
        grid=(b, nblk // gb, s // sc),
        in_specs=[pl.BlockSpec((None, sc, width), lambda bi, gi, si: (bi, si, bx0 + gi)),
                  pl.BlockSpec((None, sc, width), lambda bi, gi, si: (bi, si, bg0 + gi)),
                  pl.BlockSpec((SHORT_CONV, width), lambda bi, gi, si: (0, gi)),
                  vec(), mat(), vec(), mat(), vec(), vec(), vec()],
        out_specs=pl.BlockSpec((None, sc, width), lambda bi, gi, si: (bi, si, gi)),
        out_shape=jax.ShapeDtypeStruct((b, s, lw), BF16),
        scratch_shapes=[pltpu.VMEM((1, sc + HALO, width), F32),
                        pltpu.VMEM((sc, width), F32),
                        pltpu.VMEM((sc, width), F32),
                        pltpu.VMEM((1, width), F32)],
        compiler_params=_params("parallel", "parallel", "arbitrary"),
        name="rglru",
    )(proj3, proj3, conv_w, conv_b.reshape(1, lw), w_a, b_a.reshape(1, lw), w_x,
      b_x.reshape(1, lw), lam.reshape(1, lw), norm_w.reshape(1, lw))


def _sg_kernel(u_ref, v_ref, lnw_ref, lnb_ref, ws_ref, bst_ref, nw_ref, o_ref, *, sc,
               groups):
    t = SG_CHUNK
    tril = _iota((t, t), 0) >= _iota((t, t), 1)
    for ci in range(sc // t):
        rs = slice(ci * t, (ci + 1) * t)
        v = jax.nn.gelu(v_ref[rs, :])
        mu = jnp.mean(v, axis=-1, keepdims=True)
        vc = v - mu
        var = jnp.mean(vc * vc, axis=-1, keepdims=True)
        vn = vc * lax.rsqrt(var + EPS) * lnw_ref[...] + lnb_ref[...]
        for g in range(groups):
            cs = slice(g * LANE, (g + 1) * LANE)
            w_causal = jnp.where(tril, ws_ref[g], 0.0)
            z = _dot(w_causal, vn[:, cs]) + bst_ref[:, g:g + 1]
            y = jax.nn.gelu(u_ref[rs, cs]) * z
            y = y * lax.rsqrt(jnp.mean(y * y, axis=-1, keepdims=True) + EPS)
            o_ref[rs, cs] = (y * nw_ref[:, cs]).astype(o_ref.dtype)


def _spatial_gating(proj3, ln_w, ln_b, w_s, b_s, norm_w, col_u, col_v):
    b, s, _ = proj3.shape
    groups = w_s.shape[0]
    width = groups * LANE
    sc = _largest_tile(s, 512, SG_CHUNK)
    assert col_u % width == 0 and col_v % width == 0
    bu0, bv0 = col_u // width, col_v // width
    vec = lambda: pl.BlockSpec((1, width), lambda bi, si: (0, 0))
    return pl.pallas_call(
        functools.partial(_sg_kernel, sc=sc, groups=groups),
        grid=(b, s // sc),
        in_specs=[pl.BlockSpec((None, sc, width), lambda bi, si: (bi, si, bu0)),
                  pl.BlockSpec((None, sc, width), lambda bi, si: (bi, si, bv0)),
                  vec(), vec(),
                  pl.BlockSpec((groups, SG_CHUNK, SG_CHUNK), lambda bi, si: (0, 0, 0)),
                  pl.BlockSpec((SG_CHUNK, groups), lambda bi, si: (0, 0)),
                  vec()],
        out_specs=pl.BlockSpec((None, sc, width), lambda bi, si: (bi, si, 0)),
        out_shape=jax.ShapeDtypeStruct((b, s, width), BF16),
        compiler_params=_params("parallel", "parallel"),
        name="spatial_gating",
    )(proj3, proj3, ln_w.reshape(1, width), ln_b.reshape(1, width), w_s, b_s.T,
      norm_w.reshape(1, width))


def kernel(x, norm_mix, w_in, dn_conv_w, dn_a_log, dn_dt_bias, dn_norm_w, lru_conv_w, lru_conv_b, lru_w_a, lru_b_a, lru_w_x, lru_b_x, lru_lambda, lru_norm_w, sg_ln_w, sg_ln_b, sg_w_s, sg_b_s, sg_norm_w, w_out, norm_ffn, w_up, ffn_conv_w, ffn_conv_b, w_down, norm_final):
    bsz, seq, d = x.shape
    depth = w_in.shape[0]
    n_heads = dn_a_log.shape[1]
    dn_w = n_heads * HEAD_DIM
    lru_w = lru_lambda.shape[1]
    sg_w = sg_ln_w.shape[1]
    m = bsz * seq
    gate0 = 4 * dn_w
    tail0 = gate0 + 2 * n_heads
    main_cols = gate0 + 2 * lru_w + 2 * sg_w
    tn = IN_PROJ_TILE
    assert gate0 % tn == 0 and tail0 % HALO == 0
    w_in_t = jnp.swapaxes(w_in, 1, 2)

    def main_rows(j):
        skip = jnp.where(j < gate0 // tn, 0, (tail0 - gate0) // HALO)
        return HALO * (j * (tn // HALO) + skip)

    x2 = x.reshape(m, d)
    for l in range(depth):
        h = _rmsnorm(x2, norm_mix[l], BF16)
        proj = _in_proj(h, w_in_t, l, main_rows, main_cols, tn, "in_proj")
        ba = _in_proj(h, w_in_t, l, lambda j: gate0, LANE, LANE, "in_proj_gates")
        proj3 = proj.reshape(bsz, seq, main_cols)
        ba3 = ba.reshape(bsz, seq, LANE)

        y_a = _deltanet(proj3, ba3, dn_conv_w[l], dn_a_log[l], dn_dt_bias[l],
                        dn_norm_w[l], n_heads, 0)
        y_b = _rglru(proj3, lru_conv_w[l], lru_conv_b[l], lru_w_a[l], lru_b_a[l],
                     lru_w_x[l], lru_b_x[l], lru_lambda[l], lru_norm_w[l],
                     gate0, gate0 + lru_w)
        y_c = _spatial_gating(proj3, sg_ln_w[l], sg_ln_b[l], sg_w_s[l], sg_b_s[l],
                              sg_norm_w[l], gate0 + 2 * lru_w, gate0 + 2 * lru_w + sg_w)
        mix = jnp.concatenate([y_a, y_b, y_c], axis=-1).reshape(m, d)
        x2 = _matmul_wstat(mix, w_out, layer=l, residual=x2, name="out_proj")

        h = _rmsnorm(x2, norm_ffn[l], BF16)
        act, w_down_bf16 = _ffn_up(h, w_up, w_down, l, ffn_conv_w[l], ffn_conv_b[l], seq)
        x2 = _matmul(act, w_down_bf16, residual=x2, tm_cap=1024, tk_cap=5504,
                     name="down_proj")
    return _rmsnorm(x2, norm_final, F32).reshape(bsz, seq, d)
```

```python
import functools

import jax
import jax.numpy as jnp
from jax import lax
from jax.experimental import pallas as pl
from jax.experimental.pallas import tpu as pltpu

F32 = jnp.float32
BF16 = jnp.bfloat16
HIGHEST = lax.Precision.HIGHEST

EPS = 1e-6
LANE = 128
HALO = 8
HEAD_DIM = LANE
DN_CHUNK = 64
DN_SEQ_TILE = 128
SG_CHUNK = 128
SCAN_CHUNK = 64
SHORT_CONV = 4
LRU_C = 8.0
VMEM_LIMIT_BYTES = 56 * 1024 * 1024


def _params(*semantics):
    return pltpu.CompilerParams(dimension_semantics=semantics,
                                vmem_limit_bytes=VMEM_LIMIT_BYTES)


def _largest_tile(n, cap, unit):
    best = None
    t = unit
    while t <= min(n, cap):
        if n % t == 0:
            best = t
        t += unit
    assert best is not None, (n, cap, unit)
    return best


def _dot(a, b):
    return jnp.dot(a.astype(BF16), b.astype(BF16), preferred_element_type=F32)


def _dot_nt(a, b):
    return lax.dot_general(a.astype(BF16), b.astype(BF16), (((1,), (1,)), ((), ())),
                           preferred_element_type=F32)


def _softplus(x):
    return jnp.maximum(x, 0.0) + jnp.log1p(jnp.exp(-jnp.abs(x)))


def _silu(x):
    return x * jax.nn.sigmoid(x)


def _iota(shape, dim):
    return lax.broadcasted_iota(jnp.int32, shape, dim)


def _rmsnorm_kernel(x_ref, w_ref, o_ref):
    x = x_ref[...]
    ms = jnp.mean(x * x, axis=-1, keepdims=True)
    o_ref[...] = (x * lax.rsqrt(ms + EPS) * w_ref[...]).astype(o_ref.dtype)


def _rmsnorm(x2, w, out_dtype):
    m, d = x2.shape
    tm = _largest_tile(m, 256, 16)
    return pl.pallas_call(
        _rmsnorm_kernel,
        grid=(m // tm,),
        in_specs=[pl.BlockSpec((tm, d), lambda i: (i, 0)),
                  pl.BlockSpec((1, d), lambda i: (0, 0))],
        out_specs=pl.BlockSpec((tm, d), lambda i: (i, 0)),
        out_shape=jax.ShapeDtypeStruct((m, d), out_dtype),
        compiler_params=_params("parallel"),
        name="rmsnorm",
    )(x2, w.reshape(1, d))


def _matmul_kernel(*refs, nk, has_res):
    if has_res:
        a_ref, w_ref, res_ref, o_ref = refs[:4]
    else:
        a_ref, w_ref, o_ref = refs[:3]
        res_ref = None
    part = jnp.dot(a_ref[...], w_ref[...], preferred_element_type=F32)

    def finish(acc):
        if res_ref is not None:
            acc = acc + res_ref[...]
        o_ref[...] = acc.astype(o_ref.dtype)

    if nk == 1:
        finish(part)
        return
    acc_ref = refs[-1]
    k = pl.program_id(2)

    @pl.when(k == 0)
    def _():
        acc_ref[...] = part

    @pl.when(k > 0)
    def _():
        acc_ref[...] += part

    @pl.when(k == nk - 1)
    def _():
        finish(acc_ref[...])


def _matmul(a, w, *, residual=None, out_dtype=F32, tm_cap=512, tn_cap=512, tk_cap=4096,
            name="matmul"):
    m, k = a.shape
    n = w.shape[1]
    tm = _largest_tile(m, tm_cap, 16)
    tn = _largest_tile(n, tn_cap, LANE)
    tk = _largest_tile(k, tk_cap, LANE)
    nk = k // tk
    in_specs = [pl.BlockSpec((tm, tk), lambda j, i, kk: (i, kk)),
                pl.BlockSpec((tk, tn), lambda j, i, kk: (kk, j))]
    args = [a, w]
    if residual is not None:
        in_specs.append(pl.BlockSpec((tm, tn), lambda j, i, kk: (i, j)))
        args.append(residual)
    scratch = [pltpu.VMEM((tm, tn), F32)] if nk > 1 else []
    return pl.pallas_call(
        functools.partial(_matmul_kernel, nk=nk, has_res=residual is not None),
        grid=(n // tn, m // tm, nk),
        in_specs=in_specs,
        out_specs=pl.BlockSpec((tm, tn), lambda j, i, kk: (i, j)),
        out_shape=jax.ShapeDtypeStruct((m, n), out_dtype),
        scratch_shapes=scratch,
        compiler_params=_params("parallel", "parallel", "arbitrary"),
        name=name,
    )(*args)


CAST_ROWS = 256
IN_PROJ_TILE = 512


def _cast_rows(src_ref, dst_ref):
    step = min(CAST_ROWS, src_ref.shape[0])
    assert src_ref.shape[0] % step == 0

    def body(r, carry):
        rows = pl.ds(pl.multiple_of(r * step, step), step)
        dst_ref[rows, :] = src_ref[rows, :].astype(dst_ref.dtype)
        return carry

    lax.fori_loop(0, src_ref.shape[0] // step, body, 0)


def _weight_spec(w, layer, k, tn, col_block0=0):
    if layer is None:
        return pl.BlockSpec((k, tn), lambda j, i: (0, col_block0 + j))
    return pl.BlockSpec((None, k, tn), lambda j, i: (layer, 0, col_block0 + j))


def _out_proj_kernel(*refs, widths):
    a_refs = refs[:len(widths)]
    w_ref, res_ref, o_ref, wb_ref = refs[len(widths):]

    @pl.when(pl.program_id(1) == 0)
    def _():
        _cast_rows(w_ref, wb_ref)

    acc = res_ref[...]
    k0 = 0
    for a_ref, wk in zip(a_refs, widths):
        acc = acc + jnp.dot(a_ref[...], wb_ref[k0:k0 + wk, :], preferred_element_type=F32)
        k0 += wk
    o_ref[...] = acc


def _out_proj(a_parts, w, layer, residual, name):
    m = a_parts[0].shape[0]
    widths = tuple(a.shape[1] for a in a_parts)
    k = sum(widths)
    n = w.shape[-1]
    assert k == w.shape[-2]
    tm = _largest_tile(m, 1024, 16)
    tn = _largest_tile(n, 512, LANE)
    in_specs = [pl.BlockSpec((tm, wk), lambda j, i: (i, 0)) for wk in widths]
    in_specs += [_weight_spec(w, layer, k, tn), pl.BlockSpec((tm, tn), lambda j, i: (i, j))]
    return pl.pallas_call(
        functools.partial(_out_proj_kernel, widths=widths),
        grid=(n // tn, m // tm),
        in_specs=in_specs,
        out_specs=pl.BlockSpec((tm, tn), lambda j, i: (i, j)),
        out_shape=jax.ShapeDtypeStruct((m, n), F32),
        scratch_shapes=[pltpu.VMEM((k, tn), BF16)],
        compiler_params=_params("arbitrary", "arbitrary"),
        name=name,
    )(*a_parts, w, residual)


def _in_proj_kernel(a_ref, wt_ref, o_ref, wb_ref):
    @pl.when(pl.program_id(1) == 0)
    def _():
        _cast_rows(wt_ref.at[0], wb_ref)

    o_ref[...] = lax.dot_general(a_ref[...], wb_ref[...], (((1,), (1,)), ((), ())),
                                 preferred_element_type=F32)


def _in_proj(a, w_t, layer, row_start, n, tn, name):
    m, k = a.shape
    assert n % tn == 0
    tm = _largest_tile(m, 1024, 16)
    return pl.pallas_call(
        _in_proj_kernel,
        grid=(n // tn, m // tm),
        in_specs=[pl.BlockSpec((tm, k), lambda j, i: (i, 0)),
                  pl.BlockSpec((pl.Element(1), pl.Element(tn), pl.Element(k)),
                               lambda j, i: (layer, row_start(j), 0))],
        out_specs=pl.BlockSpec((tm, tn), lambda j, i: (i, j)),
        out_shape=jax.ShapeDtypeStruct((m, n), F32),
        scratch_shapes=[pltpu.VMEM((tn, k), BF16)],
        compiler_params=_params("arbitrary", "arbitrary"),
        name=name,
    )(a, w_t)


FFN_ROW_BLOCK = 512


def _ffn_up_kernel(a_ref, wg_ref, wu_ref, cwg_ref, cwu_ref, bg_ref, bu_ref, wd_ref,
                   o_ref, wdb_ref, wgb, wub, gbuf, ubuf, *, tm, rb, tiles_per_seq):
    i = pl.program_id(1)
    taps = cwg_ref.shape[0]

    wdb_ref[...] = wd_ref[...].astype(wdb_ref.dtype)

    @pl.when(i == 0)
    def _():
        _cast_rows(wg_ref, wgb)
        _cast_rows(wu_ref, wub)

    first = (i % tiles_per_seq) == 0
    for buf in (gbuf, ubuf):
        @pl.when(first)
        def _():
            buf[0:HALO, :] = jnp.zeros((HALO, buf.shape[1]), F32)

        @pl.when(jnp.logical_not(first))
        def _():
            buf[0:HALO, :] = buf[tm:tm + HALO, :]

    def conv(buf, cw_ref, r0):
        p = buf[pl.ds(r0, rb + HALO), :]
        acc = cw_ref[0:1, :] * p
        for j in range(1, taps):
            acc = cw_ref[j:j + 1, :] * p + pltpu.roll(acc, 1, 0)
        return acc[HALO:]

    for r in range(tm // rb):
        a = a_ref[r * rb:(r + 1) * rb, :]
        gbuf[HALO + r * rb:HALO + (r + 1) * rb, :] = jnp.dot(
            a, wgb[...], preferred_element_type=F32)
        ubuf[HALO + r * rb:HALO + (r + 1) * rb, :] = jnp.dot(
            a, wub[...], preferred_element_type=F32)
        g = conv(gbuf, cwg_ref, r * rb) + bg_ref[...]
        u = conv(ubuf, cwu_ref, r * rb) + bu_ref[...]
        o_ref[r * rb:(r + 1) * rb, :] = (_silu(g) * u).astype(o_ref.dtype)


def _ffn_up(h, w_up, w_down, layer, conv_w, conv_b, seq):
    m, k = h.shape
    f = w_up.shape[-1] // 2
    d_out = w_down.shape[-1]
    taps = conv_w.shape[0]
    tm = _largest_tile(seq, 1024, FFN_ROW_BLOCK)
    tf = _largest_tile(f, 256, LANE)
    nf = f // tf
    row_tiles = m // tm
    slab = f // (nf * row_tiles)
    assert k % CAST_ROWS == 0 and slab * nf * row_tiles == f and slab % 16 == 0
    conv_b = conv_b.reshape(1, 2 * f)
    return pl.pallas_call(
        functools.partial(_ffn_up_kernel, tm=tm, rb=FFN_ROW_BLOCK, tiles_per_seq=seq // tm),
        grid=(nf, row_tiles),
        in_specs=[pl.BlockSpec((tm, k), lambda j, i: (i, 0)),
                  _weight_spec(w_up, layer, k, tf),
                  _weight_spec(w_up, layer, k, tf, nf),
                  pl.BlockSpec((taps, tf), lambda j, i: (0, j)),
                  pl.BlockSpec((taps, tf), lambda j, i: (0, nf + j)),
                  pl.BlockSpec((1, tf), lambda j, i: (0, j)),
                  pl.BlockSpec((1, tf), lambda j, i: (0, nf + j)),
                  pl.BlockSpec((None, slab, d_out),
                               lambda j, i: (layer, j * row_tiles + i, 0))],
        out_specs=[pl.BlockSpec((tm, tf), lambda j, i: (i, j)),
                   pl.BlockSpec((slab, d_out), lambda j, i: (j * row_tiles + i, 0))],
        out_shape=[jax.ShapeDtypeStruct((m, f), BF16),
                   jax.ShapeDtypeStruct((f, d_out), BF16)],
        scratch_shapes=[pltpu.VMEM((k, tf), BF16),
                        pltpu.VMEM((k, tf), BF16),
                        pltpu.VMEM((tm + HALO, tf), F32),
                        pltpu.VMEM((tm + HALO, tf), F32)],
        compiler_params=_params("arbitrary", "arbitrary"),
        name="ffn_up_conv_gate",
    )(h, w_up, w_up, conv_w, conv_w, conv_b, conv_b, w_down)


def _load_with_halo(xpad, idx, src_ref, is_first, rows):
    width = xpad.shape[-1]

    @pl.when(is_first)
    def _():
        xpad[idx, 0:HALO, :] = jnp.zeros((HALO, width), F32)

    @pl.when(jnp.logical_not(is_first))
    def _():
        xpad[idx, 0:HALO, :] = xpad[idx, rows:rows + HALO, :]

    xpad[idx, HALO:, :] = src_ref[...]


def _short_conv(xpad, idx, cw_ref, r0, rows, cs=slice(None)):
    p = xpad[idx, pl.ds(r0, rows + HALO), cs]
    acc = cw_ref[0:1, cs] * p
    for j in range(1, cw_ref.shape[0]):
        acc = cw_ref[j:j + 1, cs] * p + pltpu.roll(acc, 1, 0)
    return acc[HALO:]


def _deltanet_kernel(q_ref, k_ref, v_ref, z_ref, ba_ref, cwq_ref, cwk_ref, cwv_ref,
                     gp_ref, nw_ref, o_ref, xpad, qkvs, gct, state, *, sc, hb, nb, n_heads):
    c = DN_CHUNK
    c2 = 2 * c
    ppb = hb // 2
    npairs = nb * ppb
    hblk = pl.program_id(0)
    is_first = pl.program_id(1) == 0

    @pl.when(is_first)
    def _():
        state[...] = jnp.zeros(state.shape, F32)

    for idx, src in enumerate((q_ref, k_ref, v_ref)):
        for b in range(nb):
            _load_with_halo(xpad, idx * nb + b, src.at[b], is_first, sc)

    for idx, cw_ref in enumerate((cwq_ref, cwk_ref, cwv_ref)):
        for b in range(nb):
            for ci in range(sc // c):
                for i in range(hb):
                    cs = slice(i * LANE, (i + 1) * LANE)
                    seg = _silu(_short_conv(xpad, idx * nb + b, cw_ref, ci * c, c, cs))
                    if idx < 2:
                        seg = seg * lax.rsqrt(
                            jnp.sum(seg * seg, axis=-1, keepdims=True) + EPS)
                    qkvs[idx, b * ppb + i // 2, ci, (i % 2) * c:(i % 2 + 1) * c, :] = seg

    row = _iota((c2, c2), 0)
    col = _iota((c2, c2), 1)
    same_head = (row >= c) == (col >= c)
    causal = (row >= col) & same_head
    strict = (row > col) & same_head
    second = (row >= c).astype(jnp.int32)
    left_cols = col < c
    eye = (row == col).astype(F32)
    tril = (_iota((c, c), 0) >= _iota((c, c), 1)).astype(F32)
    lane_row = _iota((1, c2), 1)
    rows1 = _iota((c2, 1), 0)
    rows2 = _iota((2 * c2, 1), 0)
    first_head_rows = (rows2 % c2) < c
    scale = HEAD_DIM ** -0.5
    pairs = range(npairs)

    def chunk_body(ci, carry):
        r0 = pl.multiple_of(ci * c, c)
        gc_stack, beta_stack = [], []
        for b in range(nb):
            ba = ba_ref[b, pl.ds(r0, c), :]
            beta_all = jax.nn.sigmoid(ba)
            g_all = -jnp.exp(gp_ref[0:1, :]) * _softplus(ba + gp_ref[1:2, :])
            gc_all = jnp.dot(tril, g_all, precision=HIGHEST,
                             preferred_element_type=F32)
            gc_stack.append(jnp.concatenate([gc_all, gc_all], axis=0))
            beta_stack.append(jnp.concatenate([beta_all, beta_all], axis=0))
            gct[b] = gc_stack[b].T
        bof = [u // ppb for u in pairs]
        h0 = [hblk * hb + 2 * (u % ppb) for u in pairs]
        bcol = [jnp.sum(jnp.where(col == h0[u] + second, beta_stack[bof[u]], 0.0), axis=-1,
                        keepdims=True) for u in pairs]
        gcol = [jnp.sum(jnp.where(col == h0[u] + n_heads + second, gc_stack[bof[u]], 0.0),
                        axis=-1, keepdims=True) for u in pairs]
        grow = [jnp.where(lane_row < c, gct[bof[u], pl.ds(h0[u] + n_heads, 1), :],
                          gct[bof[u], pl.ds(h0[u] + n_heads + 1, 1), :])
                for u in pairs]
        glast = [jnp.where(rows1 < c, g[c - 1:c, :], g[c2 - 1:c2, :]) for g in gcol]
        decay = [jnp.where(causal, jnp.exp(jnp.where(causal, gcol[p] - grow[p], 0.0)), 0.0)
                 for p in pairs]
        egc = [jnp.exp(g) for g in gcol]

        q = [qkvs[0, p, ci] * scale for p in pairs]
        k = [qkvs[1, p, ci] for p in pairs]
        v = [qkvs[2, p, ci] for p in pairs]
        kb = [k[p] * bcol[p] for p in pairs]
        kq = [_dot_nt(jnp.concatenate([kb[p], q[p]], axis=0), k[p]) for p in pairs]
        a_low = [jnp.where(strict, kq[p][:c2] * decay[p], 0.0) for p in pairs]
        attn = [kq[p][c2:] * decay[p] for p in pairs]

        t_inv = [eye - a for a in a_low]
        pw = a_low
        n = 2
        while n < c:
            pw = [_dot(x, x) for x in pw]
            t_inv = [t_inv[p] + _dot(t_inv[p], pw[p]) for p in pairs]
            n *= 2

        sol = [_dot(t_inv[p], jnp.concatenate([v[p] * bcol[p], kb[p] * egc[p]], axis=1))
               for p in pairs]
        k_dec = [k[p] * jnp.exp(glast[p] - gcol[p]) for p in pairs]
        s_prev = [state[p] for p in pairs]
        ws_qs = []
        for p in pairs:
            x = jnp.concatenate([sol[p][:, LANE:], q[p] * egc[p]], axis=0)
            x_wide = jnp.concatenate([jnp.where(first_head_rows, x, 0.0),
                                      jnp.where(first_head_rows, 0.0, x)], axis=1)
            ws_qs.append(_dot(x_wide, s_prev[p]))
        v_new = [sol[p][:, :LANE] - ws_qs[p][:c2] for p in pairs]
        o = [ws_qs[p][c2:] + _dot(attn[p], v_new[p]) for p in pairs]
        for p in pairs:
            kdt = k_dec[p].T
            kd_stack = jnp.concatenate([jnp.where(left_cols, kdt, 0.0),
                                        jnp.where(left_cols, 0.0, kdt)], axis=0)
            s_decay = jnp.where(rows2 < c2, jnp.exp(gcol[p][c - 1:c, :]),
                                jnp.exp(gcol[p][c2 - 1:c2, :]))
            state[p] = s_prev[p] * s_decay + _dot(kd_stack, v_new[p])
        for p in pairs:
            b, hp = bof[p], p % ppb
            cs0 = slice((2 * hp) * LANE, (2 * hp + 1) * LANE)
            cs1 = slice((2 * hp + 1) * LANE, (2 * hp + 2) * LANE)
            zg = _silu(jnp.concatenate([z_ref[b, pl.ds(r0, c), cs0],
                                        z_ref[b, pl.ds(r0, c), cs1]], axis=0))
            op = o[p] * lax.rsqrt(jnp.mean(o[p] * o[p], axis=-1, keepdims=True) + EPS)
            op = (op * nw_ref[...] * zg).astype(o_ref.dtype)
            o_ref[b, pl.ds(r0, c), cs0] = op[:c]
            o_ref[b, pl.ds(r0, c), cs1] = op[c:]
        return carry

    lax.fori_loop(0, sc // c, chunk_body, 0)


def _deltanet(proj3, ba3, conv_w, a_log, dt_bias, norm_w, n_heads, col0):
    b, s, _ = proj3.shape
    assert n_heads % 2 == 0 and 2 * n_heads <= LANE
    hb = _largest_tile(n_heads, 12, 2)
    width = hb * LANE
    nhb = n_heads // hb
    sc = _largest_tile(s, DN_SEQ_TILE, DN_CHUNK)
    base = col0 // width
    assert col0 % width == 0
    npairs = b * hb // 2
    gp = jnp.zeros((2, LANE), F32)
    gp = gp.at[0, n_heads:2 * n_heads].set(a_log).at[1, n_heads:2 * n_heads].set(dt_bias)

    def col_spec(group):
        return pl.BlockSpec((b, sc, width),
                            lambda hi, si, g=group: (0, si, base + g * nhb + hi))

    def cw_spec(group):
        return pl.BlockSpec((SHORT_CONV, width), lambda hi, si, g=group: (0, g * nhb + hi))

    return pl.pallas_call(
        functools.partial(_deltanet_kernel, sc=sc, hb=hb, nb=b, n_heads=n_heads),
        grid=(nhb, s // sc),
        in_specs=[col_spec(0), col_spec(1), col_spec(2), col_spec(3),
                  pl.BlockSpec((b, sc, LANE), lambda hi, si: (0, si, 0)),
                  cw_spec(0), cw_spec(1), cw_spec(2),
                  pl.BlockSpec((2, LANE), lambda hi, si: (0, 0)),
                  pl.BlockSpec((1, LANE), lambda hi, si: (0, 0))],
        out_specs=pl.BlockSpec((b, sc, width), lambda hi, si: (0, si, hi)),
        out_shape=jax.ShapeDtypeStruct((b, s, n_heads * LANE), BF16),
        scratch_shapes=[pltpu.VMEM((3 * b, sc + HALO, width), F32),
                        pltpu.VMEM((3, npairs, sc // DN_CHUNK, 2 * DN_CHUNK, LANE), F32),
                        pltpu.VMEM((b, LANE, LANE), F32),
                        pltpu.VMEM((npairs, 2 * HEAD_DIM, HEAD_DIM), F32)],
        compiler_params=_params("parallel", "arbitrary"),
        name="gated_deltanet",
    )(proj3, proj3, proj3, proj3, ba3, conv_w, conv_w, conv_w, gp, norm_w.reshape(1, LANE))


def _rglru_kernel(x_ref, g_ref, cw_ref, cb_ref, wa_ref, ba_ref, wx_ref, bx_ref, lam_ref,
                  nw_ref, o_ref, xpad, abuf, bbuf, hcarry, *, sc, gb):
    c = SCAN_CHUNK
    is_first = pl.program_id(2) == 0

    @pl.when(is_first)
    def _():
        hcarry[...] = jnp.zeros(hcarry.shape, F32)

    _load_with_halo(xpad, 0, x_ref, is_first, sc)

    rows = 2 * c
    for ci in range(sc // rows):
        xc = _short_conv(xpad, 0, cw_ref, ci * rows, rows) + cb_ref[...]
        for g in range(gb):
            cs = slice(g * LANE, (g + 1) * LANE)
            xg = xc[:, cs]
            r = jax.nn.sigmoid(_dot(xg, wa_ref[g]) + ba_ref[:, cs])
            ig = jax.nn.sigmoid(_dot(xg, wx_ref[g]) + bx_ref[:, cs])
            a = jnp.exp(-LRU_C * r * _softplus(-lam_ref[:, cs]))
            abuf[ci * rows:(ci + 1) * rows, cs] = a
            bbuf[ci * rows:(ci + 1) * rows, cs] = (
                jnp.sqrt((1.0 - a) * (1.0 + a)) * (ig * xg))

    row = _iota((HALO, LANE), 0)

    def chunk_body(ci, carry):
        r0 = pl.multiple_of(ci * c, c)
        for g in range(gb):
            cs = slice(g * LANE, (g + 1) * LANE)
            a = abuf[pl.ds(r0, c), cs]
            bb = bbuf[pl.ds(r0, c), cs]
            last = jnp.broadcast_to(hcarry[:, cs], (HALO, LANE))
            tiles = []
            for v in range(c // HALO):
                av = a[v * HALO:(v + 1) * HALO]
                bv = bb[v * HALO:(v + 1) * HALO]
                d = 1
                while d < HALO:
                    keep = row >= d
                    a_sh = jnp.where(keep, pltpu.roll(av, d, 0), 1.0)
                    b_sh = jnp.where(keep, pltpu.roll(bv, d, 0), 0.0)
                    bv = av * b_sh + bv
                    av = av * a_sh
                    d *= 2
                hv = bv + av * last
                last = jnp.broadcast_to(hv[HALO - 1:HALO, :], (HALO, LANE))
                tiles.append(hv)
            h = jnp.concatenate(tiles, axis=0)
            hcarry[:, cs] = last[0:1, :]
            y = h * jax.nn.gelu(g_ref[pl.ds(r0, c), cs])
            y = y * lax.rsqrt(jnp.mean(y * y, axis=-1, keepdims=True) + EPS)
            o_ref[pl.ds(r0, c), cs] = (y * nw_ref[:, cs]).astype(o_ref.dtype)
        return carry

    lax.fori_loop(0, sc // c, chunk_body, 0)


def _rglru(proj3, conv_w, conv_b, w_a, b_a, w_x, b_x, lam, norm_w, col_x, col_g):
    b, s, _ = proj3.shape
    nblk = w_a.shape[0]
    gb = 4 if nblk % 4 == 0 else (3 if nblk % 3 == 0 else 1)
    width = gb * LANE
    lw = nblk * LANE
    sc = _largest_tile(s, 512, 2 * SCAN_CHUNK)
    assert col_x % width == 0 and col_g % width == 0
    bx0, bg0 = col_x // width, col_g // width
    vec = lambda: pl.BlockSpec((1, width), lambda bi, gi, si: (0, gi))
    mat = lambda: pl.BlockSpec((gb, LANE, LANE), lambda bi, gi, si: (gi, 0, 0))
    return pl.pallas_call(
        functools.partial(_rglru_kernel, sc=sc, gb=gb),
        grid=(b, nblk // gb, s // sc),
        in_specs=[pl.BlockSpec((None, sc, width), lambda bi, gi, si: (bi, si, bx0 + gi)),
                  pl.BlockSpec((None, sc, width), lambda bi, gi, si: (bi, si, bg0 + gi)),
                  pl.BlockSpec((SHORT_CONV, width), lambda bi, gi, si: (0, gi)),
                  vec(), mat(), vec(), mat(), vec(), vec(), vec()],
        out_specs=pl.BlockSpec((None, sc, width), lambda bi, gi, si: (bi, si, gi)),
        out_shape=jax.ShapeDtypeStruct((b, s, lw), BF16),
        scratch_shapes=[pltpu.VMEM((1, sc + HALO, width), F32),
                        pltpu.VMEM((sc, width), F32),
                        pltpu.VMEM((sc, width), F32),
                        pltpu.VMEM((1, width), F32)],
        compiler_params=_params("parallel", "parallel", "arbitrary"),
        name="rglru",
    )(proj3, proj3, conv_w, conv_b.reshape(1, lw), w_a, b_a.reshape(1, lw), w_x,
      b_x.reshape(1, lw), lam.reshape(1, lw), norm_w.reshape(1, lw))


def _sg_kernel(u_ref, v_ref, lnw_ref, lnb_ref, ws_ref, bst_ref, nw_ref, o_ref, *, sc,
               groups):
    t = SG_CHUNK
    tril = _iota((t, t), 0) >= _iota((t, t), 1)
    for ci in range(sc // t):
        rs = slice(ci * t, (ci + 1) * t)
        v = jax.nn.gelu(v_ref[rs, :])
        mu = jnp.mean(v, axis=-1, keepdims=True)
        vc = v - mu
        var = jnp.mean(vc * vc, axis=-1, keepdims=True)
        vn = vc * lax.rsqrt(var + EPS) * lnw_ref[...] + lnb_ref[...]
        for g in range(groups):
            cs = slice(g * LANE, (g + 1) * LANE)
            w_causal = jnp.where(tril, ws_ref[g], 0.0)
            z = _dot(w_causal, vn[:, cs]) + bst_ref[:, g:g + 1]
            y = jax.nn.gelu(u_ref[rs, cs]) * z
            y = y * lax.rsqrt(jnp.mean(y * y, axis=-1, keepdims=True) + EPS)
            o_ref[rs, cs] = (y * nw_ref[:, cs]).astype(o_ref.dtype)


def _spatial_gating(proj3, ln_w, ln_b, w_s, b_s, norm_w, col_u, col_v):
    b, s, _ = proj3.shape
    groups = w_s.shape[0]
    width = groups * LANE
    sc = _largest_tile(s, 512, SG_CHUNK)
    assert col_u % width == 0 and col_v % width == 0
    bu0, bv0 = col_u // width, col_v // width
    vec = lambda: pl.BlockSpec((1, width), lambda bi, si: (0, 0))
    return pl.pallas_call(
        functools.partial(_sg_kernel, sc=sc, groups=groups),
        grid=(b, s // sc),
        in_specs=[pl.BlockSpec((None, sc, width), lambda bi, si: (bi, si, bu0)),
                  pl.BlockSpec((None, sc, width), lambda bi, si: (bi, si, bv0)),
                  vec(), vec(),
                  pl.BlockSpec((groups, SG_CHUNK, SG_CHUNK), lambda bi, si: (0, 0, 0)),
                  pl.BlockSpec((SG_CHUNK, groups), lambda bi, si: (0, 0)),
                  vec()],
        out_specs=pl.BlockSpec((None, sc, width), lambda bi, si: (bi, si, 0)),
        out_shape=jax.ShapeDtypeStruct((b, s, width), BF16),
        compiler_params=_params("parallel", "parallel"),
        name="spatial_gating",
    )(proj3, proj3, ln_w.reshape(1, width), ln_b.reshape(1, width), w_s, b_s.T,
      norm_w.reshape(1, width))


def kernel(x, norm_mix, w_in, dn_conv_w, dn_a_log, dn_dt_bias, dn_norm_w, lru_conv_w, lru_conv_b, lru_w_a, lru_b_a, lru_w_x, lru_b_x, lru_lambda, lru_norm_w, sg_ln_w, sg_ln_b, sg_w_s, sg_b_s, sg_norm_w, w_out, norm_ffn, w_up, ffn_conv_w, ffn_conv_b, w_down, norm_final):
    bsz, seq, d = x.shape
    depth = w_in.shape[0]
    n_heads = dn_a_log.shape[1]
    dn_w = n_heads * HEAD_DIM
    lru_w = lru_lambda.shape[1]
    sg_w = sg_ln_w.shape[1]
    m = bsz * seq
    gate0 = 4 * dn_w
    tail0 = gate0 + 2 * n_heads
    main_cols = gate0 + 2 * lru_w + 2 * sg_w
    tn = IN_PROJ_TILE
    assert gate0 % tn == 0 and tail0 % HALO == 0
    w_in_t = jnp.swapaxes(w_in, 1, 2)

    def main_rows(j):
        skip = jnp.where(j < gate0 // tn, 0, (tail0 - gate0) // HALO)
        return HALO * (j * (tn // HALO) + skip)

    x2 = x.reshape(m, d)
    for l in range(depth):
        h = _rmsnorm(x2, norm_mix[l], BF16)
        proj = _in_proj(h, w_in_t, l, main_rows, main_cols, tn, "in_proj")
        ba = _in_proj(h, w_in_t, l, lambda j: gate0, LANE, LANE, "in_proj_gates")
        proj3 = proj.reshape(bsz, seq, main_cols)
        ba3 = ba.reshape(bsz, seq, LANE)

        y_a = _deltanet(proj3, ba3, dn_conv_w[l], dn_a_log[l], dn_dt_bias[l],
                        dn_norm_w[l], n_heads, 0)
        y_b = _rglru(proj3, lru_conv_w[l], lru_conv_b[l], lru_w_a[l], lru_b_a[l],
                     lru_w_x[l], lru_b_x[l], lru_lambda[l], lru_norm_w[l],
                     gate0, gate0 + lru_w)
        y_c = _spatial_gating(proj3, sg_ln_w[l], sg_ln_b[l], sg_w_s[l], sg_b_s[l],
                              sg_norm_w[l], gate0 + 2 * lru_w, gate0 + 2 * lru_w + sg_w)
        mix_parts = [y.reshape(m, y.shape[-1]) for y in (y_a, y_b, y_c)]
        x2 = _out_proj(mix_parts, w_out, l, x2, "out_proj")

        h = _rmsnorm(x2, norm_ffn[l], BF16)
        act, w_down_bf16 = _ffn_up(h, w_up, w_down, l, ffn_conv_w[l], ffn_conv_b[l], seq)
        x2 = _matmul(act, w_down_bf16, residual=x2, tk_cap=w_down_bf16.shape[0],
                     name="down_proj")
    return _rmsnorm(x2, norm_final, F32).reshape(bsz, seq, d)
```

```python
import functools

import jax
import jax.numpy as jnp
from jax import lax
from jax.experimental import pallas as pl
from jax.experimental.pallas import tpu as pltpu

F32 = jnp.float32
BF16 = jnp.bfloat16
HIGHEST = lax.Precision.HIGHEST

EPS = 1e-6
LANE = 128
HALO = 8
HEAD_DIM = LANE
DN_CHUNK = 64
DN_SEQ_TILE = 128
SG_CHUNK = 128
SCAN_CHUNK = 64
SHORT_CONV = 4
LRU_C = 8.0
VMEM_LIMIT_BYTES = 58 * 1024 * 1024


def _params(*semantics):
    return pltpu.CompilerParams(dimension_semantics=semantics,
                                vmem_limit_bytes=VMEM_LIMIT_BYTES)


def _largest_tile(n, cap, unit):
    best = None
    t = unit
    while t <= min(n, cap):
        if n % t == 0:
            best = t
        t += unit
    assert best is not None, (n, cap, unit)
    return best


def _dot(a, b):
    return jnp.dot(a.astype(BF16), b.astype(BF16), preferred_element_type=F32)


def _dot_nt(a, b):
    return lax.dot_general(a.astype(BF16), b.astype(BF16), (((1,), (1,)), ((), ())),
                           preferred_element_type=F32)


def _softplus(x):
    return jnp.maximum(x, 0.0) + jnp.log1p(jnp.exp(-jnp.abs(x)))


def _silu(x):
    return x * jax.nn.sigmoid(x)


def _iota(shape, dim):
    return lax.broadcasted_iota(jnp.int32, shape, dim)


def _rmsnorm_kernel(x_ref, w_ref, o_ref):
    x = x_ref[...]
    ms = jnp.mean(x * x, axis=-1, keepdims=True)
    o_ref[...] = (x * lax.rsqrt(ms + EPS) * w_ref[...]).astype(o_ref.dtype)


def _rmsnorm(x2, w, out_dtype):
    m, d = x2.shape
    tm = _largest_tile(m, 512, 16)
    return pl.pallas_call(
        _rmsnorm_kernel,
        grid=(m // tm,),
        in_specs=[pl.BlockSpec((tm, d), lambda i: (i, 0)),
                  pl.BlockSpec((1, d), lambda i: (0, 0))],
        out_specs=pl.BlockSpec((tm, d), lambda i: (i, 0)),
        out_shape=jax.ShapeDtypeStruct((m, d), out_dtype),
        compiler_params=_params("parallel"),
        name="rmsnorm",
    )(x2, w.reshape(1, d))


def _matmul_kernel(*refs, nk, has_res):
    if has_res:
        a_ref, w_ref, res_ref, o_ref = refs[:4]
    else:
        a_ref, w_ref, o_ref = refs[:3]
        res_ref = None
    part = jnp.dot(a_ref[...], w_ref[...], preferred_element_type=F32)

    def finish(acc):
        if res_ref is not None:
            acc = acc + res_ref[...]
        o_ref[...] = acc.astype(o_ref.dtype)

    if nk == 1:
        finish(part)
        return
    acc_ref = refs[-1]
    k = pl.program_id(2)

    @pl.when(k == 0)
    def _():
        acc_ref[...] = part

    @pl.when(k > 0)
    def _():
        acc_ref[...] += part

    @pl.when(k == nk - 1)
    def _():
        finish(acc_ref[...])


def _matmul(a, w, *, residual=None, out_dtype=F32, tm_cap=512, tn_cap=512, tk_cap=4096,
            name="matmul"):
    m, k = a.shape
    n = w.shape[1]
    tm = _largest_tile(m, tm_cap, 16)
    tn = _largest_tile(n, tn_cap, LANE)
    tk = _largest_tile(k, tk_cap, LANE)
    nk = k // tk
    in_specs = [pl.BlockSpec((tm, tk), lambda j, i, kk: (i, kk)),
                pl.BlockSpec((tk, tn), lambda j, i, kk: (kk, j))]
    args = [a, w]
    if residual is not None:
        in_specs.append(pl.BlockSpec((tm, tn), lambda j, i, kk: (i, j)))
        args.append(residual)
    scratch = [pltpu.VMEM((tm, tn), F32)] if nk > 1 else []
    return pl.pallas_call(
        functools.partial(_matmul_kernel, nk=nk, has_res=residual is not None),
        grid=(n // tn, m // tm, nk),
        in_specs=in_specs,
        out_specs=pl.BlockSpec((tm, tn), lambda j, i, kk: (i, j)),
        out_shape=jax.ShapeDtypeStruct((m, n), out_dtype),
        scratch_shapes=scratch,
        compiler_params=_params("parallel", "parallel", "arbitrary"),
        name=name,
    )(*args)


CAST_ROWS = 256
IN_PROJ_TILE = 1024
IN_PROJ_ROWS = 512


def _cast_rows(src_ref, dst_ref):
    step = min(CAST_ROWS, src_ref.shape[0])
    assert src_ref.shape[0] % step == 0

    def body(r, carry):
        rows = pl.ds(pl.multiple_of(r * step, step), step)
        dst_ref[rows, :] = src_ref[rows, :].astype(dst_ref.dtype)
        return carry

    lax.fori_loop(0, src_ref.shape[0] // step, body, 0)


def _weight_spec(w, layer, k, tn, col_block0=0):
    if layer is None:
        return pl.BlockSpec((k, tn), lambda j, i: (0, col_block0 + j))
    return pl.BlockSpec((None, k, tn), lambda j, i: (layer, 0, col_block0 + j))


def _out_proj_kernel(*refs, widths):
    a_refs = refs[:len(widths)]
    w_ref, res_ref, o_ref, wb_ref = refs[len(widths):]

    @pl.when(pl.program_id(1) == 0)
    def _():
        _cast_rows(w_ref, wb_ref)

    acc = res_ref[...]
    k0 = 0
    for a_ref, wk in zip(a_refs, widths):
        acc = acc + jnp.dot(a_ref[...], wb_ref[k0:k0 + wk, :], preferred_element_type=F32)
        k0 += wk
    o_ref[...] = acc


def _out_proj(a_parts, w, layer, residual, name):
    m = a_parts[0].shape[0]
    widths = tuple(a.shape[1] for a in a_parts)
    k = sum(widths)
    n = w.shape[-1]
    assert k == w.shape[-2]
    tm = _largest_tile(m, 1024, 16)
    tn = _largest_tile(n, 512, LANE)
    in_specs = [pl.BlockSpec((tm, wk), lambda j, i: (i, 0)) for wk in widths]
    in_specs += [_weight_spec(w, layer, k, tn), pl.BlockSpec((tm, tn), lambda j, i: (i, j))]
    return pl.pallas_call(
        functools.partial(_out_proj_kernel, widths=widths),
        grid=(n // tn, m // tm),
        in_specs=in_specs,
        out_specs=pl.BlockSpec((tm, tn), lambda j, i: (i, j)),
        out_shape=jax.ShapeDtypeStruct((m, n), F32),
        scratch_shapes=[pltpu.VMEM((k, tn), BF16)],
        compiler_params=_params("arbitrary", "arbitrary"),
        name=name,
    )(*a_parts, w, residual)


def _in_proj_kernel(a_ref, wt_ref, o_ref, wb_ref):
    @pl.when(pl.program_id(1) == 0)
    def _():
        _cast_rows(wt_ref.at[0], wb_ref)

    o_ref[...] = lax.dot_general(a_ref[...], wb_ref[...], (((1,), (1,)), ((), ())),
                                 preferred_element_type=F32)


def _in_proj(a, w_t, layer, row_start, n, tn, name):
    m, k = a.shape
    assert n % tn == 0
    tm = _largest_tile(m, IN_PROJ_ROWS, 16)
    return pl.pallas_call(
        _in_proj_kernel,
        grid=(n // tn, m // tm),
        in_specs=[pl.BlockSpec((tm, k), lambda j, i: (i, 0)),
                  pl.BlockSpec((pl.Element(1), pl.Element(tn), pl.Element(k)),
                               lambda j, i: (layer, row_start(j), 0))],
        out_specs=pl.BlockSpec((tm, tn), lambda j, i: (i, j)),
        out_shape=jax.ShapeDtypeStruct((m, n), F32),
        scratch_shapes=[pltpu.VMEM((tn, k), BF16)],
        compiler_params=_params("arbitrary", "arbitrary"),
        name=name,
    )(a, w_t)


FFN_ROW_BLOCK = 512


def _ffn_up_kernel(a_ref, wg_ref, wu_ref, cwg_ref, cwu_ref, bg_ref, bu_ref, wd_ref,
                   o_ref, wdb_ref, wgb, wub, gbuf, ubuf, *, tm, rb, tiles_per_seq):
    i = pl.program_id(1)
    taps = cwg_ref.shape[0]

    wdb_ref[...] = wd_ref[...].astype(wdb_ref.dtype)

    @pl.when(i == 0)
    def _():
        _cast_rows(wg_ref, wgb)
        _cast_rows(wu_ref, wub)

    first = (i % tiles_per_seq) == 0
    for buf in (gbuf, ubuf):
        @pl.when(first)
        def _():
            buf[0:HALO, :] = jnp.zeros((HALO, buf.shape[1]), F32)

        @pl.when(jnp.logical_not(first))
        def _():
            buf[0:HALO, :] = buf[tm:tm + HALO, :]

    def conv(buf, cw_ref, r0):
        p = buf[pl.ds(r0, rb + HALO), :]
        acc = cw_ref[0:1, :] * p
        for j in range(1, taps):
            acc = cw_ref[j:j + 1, :] * p + pltpu.roll(acc, 1, 0)
        return acc[HALO:]

    for r in range(tm // rb):
        a = a_ref[r * rb:(r + 1) * rb, :]
        gbuf[HALO + r * rb:HALO + (r + 1) * rb, :] = jnp.dot(
            a, wgb[...], preferred_element_type=F32)
        ubuf[HALO + r * rb:HALO + (r + 1) * rb, :] = jnp.dot(
            a, wub[...], preferred_element_type=F32)
        g = conv(gbuf, cwg_ref, r * rb) + bg_ref[...]
        u = conv(ubuf, cwu_ref, r * rb) + bu_ref[...]
        o_ref[r * rb:(r + 1) * rb, :] = (_silu(g) * u).astype(o_ref.dtype)


def _ffn_up(h, w_up, w_down, layer, conv_w, conv_b, seq):
    m, k = h.shape
    f = w_up.shape[-1] // 2
    d_out = w_down.shape[-1]
    taps = conv_w.shape[0]
    tm = _largest_tile(seq, 1024, FFN_ROW_BLOCK)
    tf = _largest_tile(f, 256, LANE)
    nf = f // tf
    row_tiles = m // tm
    slab = f // (nf * row_tiles)
    assert k % CAST_ROWS == 0 and slab * nf * row_tiles == f and slab % 16 == 0
    conv_b = conv_b.reshape(1, 2 * f)
    return pl.pallas_call(
        functools.partial(_ffn_up_kernel, tm=tm, rb=FFN_ROW_BLOCK, tiles_per_seq=seq // tm),
        grid=(nf, row_tiles),
        in_specs=[pl.BlockSpec((tm, k), lambda j, i: (i, 0)),
                  _weight_spec(w_up, layer, k, tf),
                  _weight_spec(w_up, layer, k, tf, nf),
                  pl.BlockSpec((taps, tf), lambda j, i: (0, j)),
                  pl.BlockSpec((taps, tf), lambda j, i: (0, nf + j)),
                  pl.BlockSpec((1, tf), lambda j, i: (0, j)),
                  pl.BlockSpec((1, tf), lambda j, i: (0, nf + j)),
                  pl.BlockSpec((None, slab, d_out),
                               lambda j, i: (layer, j * row_tiles + i, 0))],
        out_specs=[pl.BlockSpec((tm, tf), lambda j, i: (i, j)),
                   pl.BlockSpec((slab, d_out), lambda j, i: (j * row_tiles + i, 0))],
        out_shape=[jax.ShapeDtypeStruct((m, f), BF16),
                   jax.ShapeDtypeStruct((f, d_out), BF16)],
        scratch_shapes=[pltpu.VMEM((k, tf), BF16),
                        pltpu.VMEM((k, tf), BF16),
                        pltpu.VMEM((tm + HALO, tf), F32),
                        pltpu.VMEM((tm + HALO, tf), F32)],
        compiler_params=_params("arbitrary", "arbitrary"),
        name="ffn_up_conv_gate",
    )(h, w_up, w_up, conv_w, conv_w, conv_b, conv_b, w_down)


def _load_with_halo(xpad, idx, src_ref, is_first, rows):
    width = xpad.shape[-1]

    @pl.when(is_first)
    def _():
        xpad[idx, 0:HALO, :] = jnp.zeros((HALO, width), F32)

    @pl.when(jnp.logical_not(is_first))
    def _():
        xpad[idx, 0:HALO, :] = xpad[idx, rows:rows + HALO, :]

    xpad[idx, HALO:, :] = src_ref[...]


def _short_conv(xpad, idx, cw_ref, r0, rows, cs=slice(None)):
    p = xpad[idx, pl.ds(r0, rows + HALO), cs]
    acc = cw_ref[0:1, cs] * p
    for j in range(1, cw_ref.shape[0]):
        acc = cw_ref[j:j + 1, cs] * p + pltpu.roll(acc, 1, 0)
    return acc[HALO:]


def _deltanet_kernel(q_ref, k_ref, v_ref, z_ref, ba_ref, cwq_ref, cwk_ref, cwv_ref,
                     gp_ref, nw_ref, o_ref, xpad, qkvs, gct, state, *, sc, hb, nb, n_heads):
    c = DN_CHUNK
    c2 = 2 * c
    ppb = hb // 2
    npairs = nb * ppb
    hblk = pl.program_id(0)
    is_first = pl.program_id(1) == 0

    @pl.when(is_first)
    def _():
        state[...] = jnp.zeros(state.shape, F32)

    for idx, src in enumerate((q_ref, k_ref, v_ref)):
        for b in range(nb):
            _load_with_halo(xpad, idx * nb + b, src.at[b], is_first, sc)

    for idx, cw_ref in enumerate((cwq_ref, cwk_ref, cwv_ref)):
        for b in range(nb):
            for ci in range(sc // c):
                for i in range(hb):
                    cs = slice(i * LANE, (i + 1) * LANE)
                    seg = _silu(_short_conv(xpad, idx * nb + b, cw_ref, ci * c, c, cs))
                    if idx < 2:
                        seg = seg * lax.rsqrt(
                            jnp.sum(seg * seg, axis=-1, keepdims=True) + EPS)
                    qkvs[idx, b * ppb + i // 2, ci, (i % 2) * c:(i % 2 + 1) * c, :] = seg

    row = _iota((c2, c2), 0)
    col = _iota((c2, c2), 1)
    same_head = (row >= c) == (col >= c)
    causal = (row >= col) & same_head
    strict = (row > col) & same_head
    second = (row >= c).astype(jnp.int32)
    left_cols = col < c
    eye = (row == col).astype(F32)
    tril = (_iota((c, c), 0) >= _iota((c, c), 1)).astype(F32)
    lane_row = _iota((1, c2), 1)
    rows1 = _iota((c2, 1), 0)
    rows2 = _iota((2 * c2, 1), 0)
    first_head_rows = (rows2 % c2) < c
    scale = HEAD_DIM ** -0.5
    pairs = range(npairs)

    def chunk_body(ci, carry):
        r0 = pl.multiple_of(ci * c, c)
        gc_stack, beta_stack = [], []
        for b in range(nb):
            ba = ba_ref[b, pl.ds(r0, c), :]
            beta_all = jax.nn.sigmoid(ba)
            g_all = -jnp.exp(gp_ref[0:1, :]) * _softplus(ba + gp_ref[1:2, :])
            gc_all = jnp.dot(tril, g_all, precision=HIGHEST,
                             preferred_element_type=F32)
            gc_stack.append(jnp.concatenate([gc_all, gc_all], axis=0))
            beta_stack.append(jnp.concatenate([beta_all, beta_all], axis=0))
            gct[b] = gc_stack[b].T
        bof = [u // ppb for u in pairs]
        h0 = [hblk * hb + 2 * (u % ppb) for u in pairs]
        bcol = [jnp.sum(jnp.where(col == h0[u] + second, beta_stack[bof[u]], 0.0), axis=-1,
                        keepdims=True) for u in pairs]
        gcol = [jnp.sum(jnp.where(col == h0[u] + n_heads + second, gc_stack[bof[u]], 0.0),
                        axis=-1, keepdims=True) for u in pairs]
        grow = [jnp.where(lane_row < c, gct[bof[u], pl.ds(h0[u] + n_heads, 1), :],
                          gct[bof[u], pl.ds(h0[u] + n_heads + 1, 1), :])
                for u in pairs]
        glast = [jnp.where(rows1 < c, g[c - 1:c, :], g[c2 - 1:c2, :]) for g in gcol]
        decay = [jnp.where(causal, jnp.exp(jnp.where(causal, gcol[p] - grow[p], 0.0)), 0.0)
                 for p in pairs]
        egc = [jnp.exp(g) for g in gcol]

        q = [qkvs[0, p, ci] * scale for p in pairs]
        k = [qkvs[1, p, ci] for p in pairs]
        v = [qkvs[2, p, ci] for p in pairs]
        kb = [k[p] * bcol[p] for p in pairs]
        kq = [_dot_nt(jnp.concatenate([kb[p], q[p]], axis=0), k[p]) for p in pairs]
        a_low = [jnp.where(strict, kq[p][:c2] * decay[p], 0.0) for p in pairs]
        attn = [kq[p][c2:] * decay[p] for p in pairs]

        t_inv = [eye - a for a in a_low]
        pw = a_low
        n = 2
        while n < c:
            pw = [_dot(x, x) for x in pw]
            t_inv = [t_inv[p] + _dot(t_inv[p], pw[p]) for p in pairs]
            n *= 2

        sol = [_dot(t_inv[p], jnp.concatenate([v[p] * bcol[p], kb[p] * egc[p]], axis=1))
               for p in pairs]
        k_dec = [k[p] * jnp.exp(glast[p] - gcol[p]) for p in pairs]
        s_prev = [state[p] for p in pairs]
        ws_qs = []
        for p in pairs:
            x = jnp.concatenate([sol[p][:, LANE:], q[p] * egc[p]], axis=0)
            x_wide = jnp.concatenate([jnp.where(first_head_rows, x, 0.0),
                                      jnp.where(first_head_rows, 0.0, x)], axis=1)
            ws_qs.append(_dot(x_wide, s_prev[p]))
        v_new = [sol[p][:, :LANE] - ws_qs[p][:c2] for p in pairs]
        o = [ws_qs[p][c2:] + _dot(attn[p], v_new[p]) for p in pairs]
        for p in pairs:
            kdt = k_dec[p].T
            kd_stack = jnp.concatenate([jnp.where(left_cols, kdt, 0.0),
                                        jnp.where(left_cols, 0.0, kdt)], axis=0)
            s_decay = jnp.where(rows2 < c2, jnp.exp(gcol[p][c - 1:c, :]),
                                jnp.exp(gcol[p][c2 - 1:c2, :]))
            state[p] = s_prev[p] * s_decay + _dot(kd_stack, v_new[p])
        for p in pairs:
            b, hp = bof[p], p % ppb
            cs0 = slice((2 * hp) * LANE, (2 * hp + 1) * LANE)
            cs1 = slice((2 * hp + 1) * LANE, (2 * hp + 2) * LANE)
            zg = _silu(jnp.concatenate([z_ref[b, pl.ds(r0, c), cs0],
                                        z_ref[b, pl.ds(r0, c), cs1]], axis=0))
            op = o[p] * lax.rsqrt(jnp.mean(o[p] * o[p], axis=-1, keepdims=True) + EPS)
            op = (op * nw_ref[...] * zg).astype(o_ref.dtype)
            o_ref[b, pl.ds(r0, c), cs0] = op[:c]
            o_ref[b, pl.ds(r0, c), cs1] = op[c:]
        return carry

    lax.fori_loop(0, sc // c, chunk_body, 0)


def _deltanet(proj3, ba3, conv_w, a_log, dt_bias, norm_w, n_heads, col0):
    b, s, _ = proj3.shape
    assert n_heads % 2 == 0 and 2 * n_heads <= LANE
    hb = _largest_tile(n_heads, 12, 2)
    width = hb * LANE
    nhb = n_heads // hb
    sc = _largest_tile(s, DN_SEQ_TILE, DN_CHUNK)
    base = col0 // width
    assert col0 % width == 0
    npairs = b * hb // 2
    gp = jnp.zeros((2, LANE), F32)
    gp = gp.at[0, n_heads:2 * n_heads].set(a_log).at[1, n_heads:2 * n_heads].set(dt_bias)

    def col_spec(group):
        return pl.BlockSpec((b, sc, width),
                            lambda hi, si, g=group: (0, si, base + g * nhb + hi))

    def cw_spec(group):
        return pl.BlockSpec((SHORT_CONV, width), lambda hi, si, g=group: (0, g * nhb + hi))

    return pl.pallas_call(
        functools.partial(_deltanet_kernel, sc=sc, hb=hb, nb=b, n_heads=n_heads),
        grid=(nhb, s // sc),
        in_specs=[col_spec(0), col_spec(1), col_spec(2), col_spec(3),
                  pl.BlockSpec((b, sc, LANE), lambda hi, si: (0, si, 0)),
                  cw_spec(0), cw_spec(1), cw_spec(2),
                  pl.BlockSpec((2, LANE), lambda hi, si: (0, 0)),
                  pl.BlockSpec((1, LANE), lambda hi, si: (0, 0))],
        out_specs=pl.BlockSpec((b, sc, width), lambda hi, si: (0, si, hi)),
        out_shape=jax.ShapeDtypeStruct((b, s, n_heads * LANE), BF16),
        scratch_shapes=[pltpu.VMEM((3 * b, sc + HALO, width), F32),
                        pltpu.VMEM((3, npairs, sc // DN_CHUNK, 2 * DN_CHUNK, LANE), F32),
                        pltpu.VMEM((b, LANE, LANE), F32),
                        pltpu.VMEM((npairs, 2 * HEAD_DIM, HEAD_DIM), F32)],
        compiler_params=_params("parallel", "arbitrary"),
        name="gated_deltanet",
    )(proj3, proj3, proj3, proj3, ba3, conv_w, conv_w, conv_w, gp, norm_w.reshape(1, LANE))


def _rglru_kernel(x_ref, g_ref, cw_ref, cb_ref, wa_ref, ba_ref, wx_ref, bx_ref, lam_ref,
                  nw_ref, o_ref, xpad, abuf, bbuf, hcarry, *, sc, gb):
    c = SCAN_CHUNK
    is_first = pl.program_id(2) == 0

    @pl.when(is_first)
    def _():
        hcarry[...] = jnp.zeros(hcarry.shape, F32)

    _load_with_halo(xpad, 0, x_ref, is_first, sc)

    rows = 2 * c
    for ci in range(sc // rows):
        xc = _short_conv(xpad, 0, cw_ref, ci * rows, rows) + cb_ref[...]
        for g in range(gb):
            cs = slice(g * LANE, (g + 1) * LANE)
            xg = xc[:, cs]
            r = jax.nn.sigmoid(_dot(xg, wa_ref[g]) + ba_ref[:, cs])
            ig = jax.nn.sigmoid(_dot(xg, wx_ref[g]) + bx_ref[:, cs])
            a = jnp.exp(-LRU_C * r * _softplus(-lam_ref[:, cs]))
            abuf[ci * rows:(ci + 1) * rows, cs] = a
            bbuf[ci * rows:(ci + 1) * rows, cs] = (
                jnp.sqrt((1.0 - a) * (1.0 + a)) * (ig * xg))

    row = _iota((HALO, LANE), 0)

    def chunk_body(ci, carry):
        r0 = pl.multiple_of(ci * c, c)
        for g in range(gb):
            cs = slice(g * LANE, (g + 1) * LANE)
            a = abuf[pl.ds(r0, c), cs]
            bb = bbuf[pl.ds(r0, c), cs]
            last = jnp.broadcast_to(hcarry[:, cs], (HALO, LANE))
            tiles = []
            for v in range(c // HALO):
                av = a[v * HALO:(v + 1) * HALO]
                bv = bb[v * HALO:(v + 1) * HALO]
                d = 1
                while d < HALO:
                    keep = row >= d
                    a_sh = jnp.where(keep, pltpu.roll(av, d, 0), 1.0)
                    b_sh = jnp.where(keep, pltpu.roll(bv, d, 0), 0.0)
                    bv = av * b_sh + bv
                    av = av * a_sh
                    d *= 2
                hv = bv + av * last
                last = jnp.broadcast_to(hv[HALO - 1:HALO, :], (HALO, LANE))
                tiles.append(hv)
            h = jnp.concatenate(tiles, axis=0)
            hcarry[:, cs] = last[0:1, :]
            y = h * jax.nn.gelu(g_ref[pl.ds(r0, c), cs])
            y = y * lax.rsqrt(jnp.mean(y * y, axis=-1, keepdims=True) + EPS)
            o_ref[pl.ds(r0, c), cs] = (y * nw_ref[:, cs]).astype(o_ref.dtype)
        return carry

    lax.fori_loop(0, sc // c, chunk_body, 0)


def _rglru(proj3, conv_w, conv_b, w_a, b_a, w_x, b_x, lam, norm_w, col_x, col_g):
    b, s, _ = proj3.shape
    nblk = w_a.shape[0]
    gb = 4 if nblk % 4 == 0 else (3 if nblk % 3 == 0 else 1)
    width = gb * LANE
    lw = nblk * LANE
    sc = _largest_tile(s, 512, 2 * SCAN_CHUNK)
    assert col_x % width == 0 and col_g % width == 0
    bx0, bg0 = col_x // width, col_g // width
    vec = lambda: pl.BlockSpec((1, width), lambda bi, gi, si: (0, gi))
    mat = lambda: pl.BlockSpec((gb, LANE, LANE), lambda bi, gi, si: (gi, 0, 0))
    return pl.pallas_call(
        functools.partial(_rglru_kernel, sc=sc, gb=gb),
        grid=(b, nblk // gb, s // sc),
        in_specs=[pl.BlockSpec((None, sc, width), lambda bi, gi, si: (bi, si, bx0 + gi)),
                  pl.BlockSpec((None, sc, width), lambda bi, gi, si: (bi, si, bg0 + gi)),
                  pl.BlockSpec((SHORT_CONV, width), lambda bi, gi, si: (0, gi)),
                  vec(), mat(), vec(), mat(), vec(), vec(), vec()],
        out_specs=pl.BlockSpec((None, sc, width), lambda bi, gi, si: (bi, si, gi)),
        out_shape=jax.ShapeDtypeStruct((b, s, lw), BF16),
        scratch_shapes=[pltpu.VMEM((1, sc + HALO, width), F32),
                        pltpu.VMEM((sc, width), F32),
                        pltpu.VMEM((sc, width), F32),
                        pltpu.VMEM((1, width), F32)],
        compiler_params=_params("parallel", "parallel", "arbitrary"),
        name="rglru",
    )(proj3, proj3, conv_w, conv_b.reshape(1, lw), w_a, b_a.reshape(1, lw), w_x,
      b_x.reshape(1, lw), lam.reshape(1, lw), norm_w.reshape(1, lw))


def _sg_kernel(u_ref, v_ref, lnw_ref, lnb_ref, ws_ref, bst_ref, nw_ref, o_ref, *, sc,
               groups):
    t = SG_CHUNK
    tril = _iota((t, t), 0) >= _iota((t, t), 1)
    for ci in range(sc // t):
        rs = slice(ci * t, (ci + 1) * t)
        v = jax.nn.gelu(v_ref[rs, :])
        mu = jnp.mean(v, axis=-1, keepdims=True)
        vc = v - mu
        var = jnp.mean(vc * vc, axis=-1, keepdims=True)
        vn = vc * lax.rsqrt(var + EPS) * lnw_ref[...] + lnb_ref[...]
        for g in range(groups):
            cs = slice(g * LANE, (g + 1) * LANE)
            w_causal = jnp.where(tril, ws_ref[g], 0.0)
            z = _dot(w_causal, vn[:, cs]) + bst_ref[:, g:g + 1]
            y = jax.nn.gelu(u_ref[rs, cs]) * z
            y = y * lax.rsqrt(jnp.mean(y * y, axis=-1, keepdims=True) + EPS)
            o_ref[rs, cs] = (y * nw_ref[:, cs]).astype(o_ref.dtype)


def _spatial_gating(proj3, ln_w, ln_b, w_s, b_s, norm_w, col_u, col_v):
    b, s, _ = proj3.shape
    groups = w_s.shape[0]
    width = groups * LANE
    sc = _largest_tile(s, 512, SG_CHUNK)
    assert col_u % width == 0 and col_v % width == 0
    bu0, bv0 = col_u // width, col_v // width
    vec = lambda: pl.BlockSpec((1, width), lambda bi, si: (0, 0))
    return pl.pallas_call(
        functools.partial(_sg_kernel, sc=sc, groups=groups),
        grid=(b, s // sc),
        in_specs=[pl.BlockSpec((None, sc, width), lambda bi, si: (bi, si, bu0)),
                  pl.BlockSpec((None, sc, width), lambda bi, si: (bi, si, bv0)),
                  vec(), vec(),
                  pl.BlockSpec((groups, SG_CHUNK, SG_CHUNK), lambda bi, si: (0, 0, 0)),
                  pl.BlockSpec((SG_CHUNK, groups), lambda bi, si: (0, 0)),
                  vec()],
        out_specs=pl.BlockSpec((None, sc, width), lambda bi, si: (bi, si, 0)),
        out_shape=jax.ShapeDtypeStruct((b, s, width), BF16),
        compiler_params=_params("parallel", "parallel"),
        name="spatial_gating",
    )(proj3, proj3, ln_w.reshape(1, width), ln_b.reshape(1, width), w_s, b_s.T,
      norm_w.reshape(1, width))


def kernel(x, norm_mix, w_in, dn_conv_w, dn_a_log, dn_dt_bias, dn_norm_w, lru_conv_w, lru_conv_b, lru_w_a, lru_b_a, lru_w_x, lru_b_x, lru_lambda, lru_norm_w, sg_ln_w, sg_ln_b, sg_w_s, sg_b_s, sg_norm_w, w_out, norm_ffn, w_up, ffn_conv_w, ffn_conv_b, w_down, norm_final):
    bsz, seq, d = x.shape
    depth = w_in.shape[0]
    n_heads = dn_a_log.shape[1]
    dn_w = n_heads * HEAD_DIM
    lru_w = lru_lambda.shape[1]
    sg_w = sg_ln_w.shape[1]
    m = bsz * seq
    gate0 = 4 * dn_w
    tail0 = gate0 + 2 * n_heads
    main_cols = gate0 + 2 * lru_w + 2 * sg_w
    tn = IN_PROJ_TILE
    assert gate0 % tn == 0 and tail0 % HALO == 0
    w_in_t = jnp.swapaxes(w_in, 1, 2)

    def main_rows(j):
        skip = jnp.where(j < gate0 // tn, 0, (tail0 - gate0) // HALO)
        return HALO * (j * (tn // HALO) + skip)

    x2 = x.reshape(m, d)
    for l in range(depth):
        h = _rmsnorm(x2, norm_mix[l], BF16)
        proj = _in_proj(h, w_in_t, l, main_rows, main_cols, tn, "in_proj")
        ba = _in_proj(h, w_in_t, l, lambda j: gate0, LANE, LANE, "in_proj_gates")
        proj3 = proj.reshape(bsz, seq, main_cols)
        ba3 = ba.reshape(bsz, seq, LANE)

        y_a = _deltanet(proj3, ba3, dn_conv_w[l], dn_a_log[l], dn_dt_bias[l],
                        dn_norm_w[l], n_heads, 0)
        y_b = _rglru(proj3, lru_conv_w[l], lru_conv_b[l], lru_w_a[l], lru_b_a[l],
                     lru_w_x[l], lru_b_x[l], lru_lambda[l], lru_norm_w[l],
                     gate0, gate0 + lru_w)
        y_c = _spatial_gating(proj3, sg_ln_w[l], sg_ln_b[l], sg_w_s[l], sg_b_s[l],
                              sg_norm_w[l], gate0 + 2 * lru_w, gate0 + 2 * lru_w + sg_w)
        mix_parts = [y.reshape(m, y.shape[-1]) for y in (y_a, y_b, y_c)]
        x2 = _out_proj(mix_parts, w_out, l, x2, "out_proj")

        h = _rmsnorm(x2, norm_ffn[l], BF16)
        act, w_down_bf16 = _ffn_up(h, w_up, w_down, l, ffn_conv_w[l], ffn_conv_b[l], seq)
        x2 = _matmul(act, w_down_bf16, residual=x2, tk_cap=w_down_bf16.shape[0],
                     name="down_proj")
    return _rmsnorm(x2, norm_final, F32).reshape(bsz, seq, d)
```

```python
import functools

import jax
import jax.numpy as jnp
from jax import lax
from jax.experimental import pallas as pl
from jax.experimental.pallas import tpu as pltpu

F32 = jnp.float32
BF16 = jnp.bfloat16
HIGHEST = lax.Precision.HIGHEST

EPS = 1e-6
LANE = 128
HALO = 8
HEAD_DIM = LANE
DN_CHUNK = 64
DN_SEQ_TILE = 256
SG_CHUNK = 128
SCAN_CHUNK = 64
SHORT_CONV = 4
LRU_C = 8.0
VMEM_LIMIT_BYTES = 60 * 1024 * 1024


def _params(*semantics):
    return pltpu.CompilerParams(dimension_semantics=semantics,
                                vmem_limit_bytes=VMEM_LIMIT_BYTES)


def _largest_tile(n, cap, unit):
    best = None
    t = unit
    while t <= min(n, cap):
        if n % t == 0:
            best = t
        t += unit
    assert best is not None, (n, cap, unit)
    return best


def _dot(a, b):
    return jnp.dot(a.astype(BF16), b.astype(BF16), preferred_element_type=F32)


def _dot_nt(a, b):
    return lax.dot_general(a.astype(BF16), b.astype(BF16), (((1,), (1,)), ((), ())),
                           preferred_element_type=F32)


def _softplus(x):
    return jnp.maximum(x, 0.0) + jnp.log1p(jnp.exp(-jnp.abs(x)))


def _silu(x):
    return x * jax.nn.sigmoid(x)


def _iota(shape, dim):
    return lax.broadcasted_iota(jnp.int32, shape, dim)


def _rmsnorm_kernel(x_ref, w_ref, o_ref):
    x = x_ref[...]
    ms = jnp.mean(x * x, axis=-1, keepdims=True)
    o_ref[...] = (x * lax.rsqrt(ms + EPS) * w_ref[...]).astype(o_ref.dtype)


def _rmsnorm(x2, w, out_dtype):
    m, d = x2.shape
    tm = _largest_tile(m, 512, 16)
    return pl.pallas_call(
        _rmsnorm_kernel,
        grid=(m // tm,),
        in_specs=[pl.BlockSpec((tm, d), lambda i: (i, 0)),
                  pl.BlockSpec((1, d), lambda i: (0, 0))],
        out_specs=pl.BlockSpec((tm, d), lambda i: (i, 0)),
        out_shape=jax.ShapeDtypeStruct((m, d), out_dtype),
        compiler_params=_params("parallel"),
        name="rmsnorm",
    )(x2, w.reshape(1, d))


def _matmul_kernel(*refs, nk, has_res):
    if has_res:
        a_ref, w_ref, res_ref, o_ref = refs[:4]
    else:
        a_ref, w_ref, o_ref = refs[:3]
        res_ref = None
    part = jnp.dot(a_ref[...], w_ref[...], preferred_element_type=F32)

    def finish(acc):
        if res_ref is not None:
            acc = acc + res_ref[...]
        o_ref[...] = acc.astype(o_ref.dtype)

    if nk == 1:
        finish(part)
        return
    acc_ref = refs[-1]
    k = pl.program_id(2)

    @pl.when(k == 0)
    def _():
        acc_ref[...] = part

    @pl.when(k > 0)
    def _():
        acc_ref[...] += part

    @pl.when(k == nk - 1)
    def _():
        finish(acc_ref[...])


def _matmul(a, w, *, residual=None, out_dtype=F32, tm_cap=512, tn_cap=512, tk_cap=4096,
            name="matmul"):
    m, k = a.shape
    n = w.shape[1]
    tm = _largest_tile(m, tm_cap, 16)
    tn = _largest_tile(n, tn_cap, LANE)
    tk = _largest_tile(k, tk_cap, LANE)
    nk = k // tk
    in_specs = [pl.BlockSpec((tm, tk), lambda j, i, kk: (i, kk)),
                pl.BlockSpec((tk, tn), lambda j, i, kk: (kk, j))]
    args = [a, w]
    if residual is not None:
        in_specs.append(pl.BlockSpec((tm, tn), lambda j, i, kk: (i, j)))
        args.append(residual)
    scratch = [pltpu.VMEM((tm, tn), F32)] if nk > 1 else []
    return pl.pallas_call(
        functools.partial(_matmul_kernel, nk=nk, has_res=residual is not None),
        grid=(n // tn, m // tm, nk),
        in_specs=in_specs,
        out_specs=pl.BlockSpec((tm, tn), lambda j, i, kk: (i, j)),
        out_shape=jax.ShapeDtypeStruct((m, n), out_dtype),
        scratch_shapes=scratch,
        compiler_params=_params("parallel", "parallel", "arbitrary"),
        name=name,
    )(*args)


CAST_ROWS = 256
IN_PROJ_TILE = 1024
IN_PROJ_ROWS = 512


def _cast_rows(src_ref, dst_ref):
    step = min(CAST_ROWS, src_ref.shape[0])
    assert src_ref.shape[0] % step == 0

    def body(r, carry):
        rows = pl.ds(pl.multiple_of(r * step, step), step)
        dst_ref[rows, :] = src_ref[rows, :].astype(dst_ref.dtype)
        return carry

    lax.fori_loop(0, src_ref.shape[0] // step, body, 0)


def _weight_spec(w, layer, k, tn, col_block0=0):
    if layer is None:
        return pl.BlockSpec((k, tn), lambda j, i: (0, col_block0 + j))
    return pl.BlockSpec((None, k, tn), lambda j, i: (layer, 0, col_block0 + j))


def _out_proj_kernel(*refs, widths):
    a_refs = refs[:len(widths)]
    w_ref, res_ref, o_ref = refs[len(widths):]
    acc = res_ref[...]
    k0 = 0
    for a_ref, wk in zip(a_refs, widths):
        acc = acc + jnp.dot(a_ref[...], w_ref[k0:k0 + wk, :], preferred_element_type=F32)
        k0 += wk
    o_ref[...] = acc


def _out_proj(a_parts, w, residual, name):
    m = a_parts[0].shape[0]
    widths = tuple(a.shape[1] for a in a_parts)
    k, n = w.shape
    assert k == sum(widths)
    tm = _largest_tile(m, 1024, 16)
    tn = _largest_tile(n, 1024, LANE)
    in_specs = [pl.BlockSpec((tm, wk), lambda j, i: (i, 0)) for wk in widths]
    in_specs += [pl.BlockSpec((k, tn), lambda j, i: (0, j)),
                 pl.BlockSpec((tm, tn), lambda j, i: (i, j))]
    return pl.pallas_call(
        functools.partial(_out_proj_kernel, widths=widths),
        grid=(n // tn, m // tm),
        in_specs=in_specs,
        out_specs=pl.BlockSpec((tm, tn), lambda j, i: (i, j)),
        out_shape=jax.ShapeDtypeStruct((m, n), F32),
        compiler_params=_params("arbitrary", "arbitrary"),
        name=name,
    )(*a_parts, w, residual)


def _in_proj_kernel(a_ref, wt_ref, *rest):
    if len(rest) == 4:
        side_ref, o_ref, side_out_ref, wb_ref = rest
        side_out_ref[...] = side_ref[...].astype(side_out_ref.dtype)
    else:
        o_ref, wb_ref = rest

    @pl.when(pl.program_id(1) == 0)
    def _():
        _cast_rows(wt_ref.at[0], wb_ref)

    o_ref[...] = lax.dot_general(a_ref[...], wb_ref[...], (((1,), (1,)), ((), ())),
                                 preferred_element_type=F32)


def _in_proj(a, w_t, layer, row_start, n, tn, name, side=None):
    m, k = a.shape
    assert n % tn == 0
    tm = _largest_tile(m, IN_PROJ_ROWS, 16)
    row_tiles = m // tm
    in_specs = [pl.BlockSpec((tm, k), lambda j, i: (i, 0)),
                pl.BlockSpec((pl.Element(1), pl.Element(tn), pl.Element(k)),
                             lambda j, i: (layer, row_start(j), 0))]
    out_specs = [pl.BlockSpec((tm, tn), lambda j, i: (i, j))]
    out_shape = [jax.ShapeDtypeStruct((m, n), F32)]
    args = [a, w_t]
    if side is not None:
        rows, cols = side.shape[1:]
        n_steps = (n // tn) * row_tiles
        slab_rows = min(s for s in range(16, rows + 1, 16)
                        if rows % s == 0 and rows // s <= n_steps)
        n_slabs = rows // slab_rows

        def slab(j, i):
            return jnp.minimum(j * row_tiles + i, n_slabs - 1)

        in_specs.append(pl.BlockSpec((None, slab_rows, cols),
                                     lambda j, i: (layer, slab(j, i), 0)))
        out_specs.append(pl.BlockSpec((slab_rows, cols), lambda j, i: (slab(j, i), 0)))
        out_shape.append(jax.ShapeDtypeStruct((rows, cols), BF16))
        args.append(side)
    outs = pl.pallas_call(
        _in_proj_kernel,
        grid=(n // tn, row_tiles),
        in_specs=in_specs,
        out_specs=out_specs,
        out_shape=out_shape,
        scratch_shapes=[pltpu.VMEM((tn, k), BF16)],
        compiler_params=_params("arbitrary", "arbitrary"),
        name=name,
    )(*args)
    return outs if side is not None else outs[0]


FFN_ROW_BLOCK = 512


def _ffn_up_kernel(a_ref, wg_ref, wu_ref, cwg_ref, cwu_ref, bg_ref, bu_ref, wd_ref,
                   o_ref, wdb_ref, wgb, wub, gbuf, ubuf, *, tm, rb, tiles_per_seq):
    i = pl.program_id(1)
    taps = cwg_ref.shape[0]

    wdb_ref[...] = wd_ref[...].astype(wdb_ref.dtype)

    @pl.when(i == 0)
    def _():
        _cast_rows(wg_ref, wgb)
        _cast_rows(wu_ref, wub)

    first = (i % tiles_per_seq) == 0
    for buf in (gbuf, ubuf):
        @pl.when(first)
        def _():
            buf[0:HALO, :] = jnp.zeros((HALO, buf.shape[1]), F32)

        @pl.when(jnp.logical_not(first))
        def _():
            buf[0:HALO, :] = buf[tm:tm + HALO, :]

    def conv(buf, cw_ref, r0):
        p = buf[pl.ds(r0, rb + HALO), :]
        acc = cw_ref[0:1, :] * p
        for j in range(1, taps):
            acc = cw_ref[j:j + 1, :] * p + pltpu.roll(acc, 1, 0)
        return acc[HALO:]

    for r in range(tm // rb):
        a = a_ref[r * rb:(r + 1) * rb, :]
        gbuf[HALO + r * rb:HALO + (r + 1) * rb, :] = jnp.dot(
            a, wgb[...], preferred_element_type=F32)
        ubuf[HALO + r * rb:HALO + (r + 1) * rb, :] = jnp.dot(
            a, wub[...], preferred_element_type=F32)
        g = conv(gbuf, cwg_ref, r * rb) + bg_ref[...]
        u = conv(ubuf, cwu_ref, r * rb) + bu_ref[...]
        o_ref[r * rb:(r + 1) * rb, :] = (_silu(g) * u).astype(o_ref.dtype)


def _ffn_up(h, w_up, w_down, layer, conv_w, conv_b, seq):
    m, k = h.shape
    f = w_up.shape[-1] // 2
    d_out = w_down.shape[-1]
    taps = conv_w.shape[0]
    tm = _largest_tile(seq, 1024, FFN_ROW_BLOCK)
    tf = _largest_tile(f, 256, LANE)
    nf = f // tf
    row_tiles = m // tm
    slab = f // (nf * row_tiles)
    assert k % CAST_ROWS == 0 and slab * nf * row_tiles == f and slab % 16 == 0
    conv_b = conv_b.reshape(1, 2 * f)
    return pl.pallas_call(
        functools.partial(_ffn_up_kernel, tm=tm, rb=FFN_ROW_BLOCK, tiles_per_seq=seq // tm),
        grid=(nf, row_tiles),
        in_specs=[pl.BlockSpec((tm, k), lambda j, i: (i, 0)),
                  _weight_spec(w_up, layer, k, tf),
                  _weight_spec(w_up, layer, k, tf, nf),
                  pl.BlockSpec((taps, tf), lambda j, i: (0, j)),
                  pl.BlockSpec((taps, tf), lambda j, i: (0, nf + j)),
                  pl.BlockSpec((1, tf), lambda j, i: (0, j)),
                  pl.BlockSpec((1, tf), lambda j, i: (0, nf + j)),
                  pl.BlockSpec((None, slab, d_out),
                               lambda j, i: (layer, j * row_tiles + i, 0))],
        out_specs=[pl.BlockSpec((tm, tf), lambda j, i: (i, j)),
                   pl.BlockSpec((slab, d_out), lambda j, i: (j * row_tiles + i, 0))],
        out_shape=[jax.ShapeDtypeStruct((m, f), BF16),
                   jax.ShapeDtypeStruct((f, d_out), BF16)],
        scratch_shapes=[pltpu.VMEM((k, tf), BF16),
                        pltpu.VMEM((k, tf), BF16),
                        pltpu.VMEM((tm + HALO, tf), F32),
                        pltpu.VMEM((tm + HALO, tf), F32)],
        compiler_params=_params("arbitrary", "arbitrary"),
        name="ffn_up_conv_gate",
    )(h, w_up, w_up, conv_w, conv_w, conv_b, conv_b, w_down)


def _load_with_halo(xpad, idx, src_ref, is_first, rows):
    width = xpad.shape[-1]

    @pl.when(is_first)
    def _():
        xpad[idx, 0:HALO, :] = jnp.zeros((HALO, width), F32)

    @pl.when(jnp.logical_not(is_first))
    def _():
        xpad[idx, 0:HALO, :] = xpad[idx, rows:rows + HALO, :]

    xpad[idx, HALO:, :] = src_ref[...]


def _short_conv(xpad, idx, cw_ref, r0, rows, cs=slice(None)):
    p = xpad[idx, pl.ds(r0, rows + HALO), cs]
    acc = cw_ref[0:1, cs] * p
    for j in range(1, cw_ref.shape[0]):
        acc = cw_ref[j:j + 1, cs] * p + pltpu.roll(acc, 1, 0)
    return acc[HALO:]


def _deltanet_kernel(q_ref, k_ref, v_ref, z_ref, ba_ref, cwq_ref, cwk_ref, cwv_ref,
                     gp_ref, nw_ref, o_ref, xpad, qkvs, gct, state, *, sc, hb, nb, n_heads):
    c = DN_CHUNK
    c2 = 2 * c
    ppb = hb // 2
    npairs = nb * ppb
    hblk = pl.program_id(0)
    is_first = pl.program_id(1) == 0

    @pl.when(is_first)
    def _():
        state[...] = jnp.zeros(state.shape, F32)

    for idx, src in enumerate((q_ref, k_ref, v_ref)):
        for b in range(nb):
            _load_with_halo(xpad, idx * nb + b, src.at[b], is_first, sc)

    for idx, cw_ref in enumerate((cwq_ref, cwk_ref, cwv_ref)):
        for b in range(nb):
            for ci in range(sc // c):
                for i in range(hb):
                    cs = slice(i * LANE, (i + 1) * LANE)
                    seg = _silu(_short_conv(xpad, idx * nb + b, cw_ref, ci * c, c, cs))
                    if idx < 2:
                        seg = seg * lax.rsqrt(
                            jnp.sum(seg * seg, axis=-1, keepdims=True) + EPS)
                    qkvs[idx, b * ppb + i // 2, ci, (i % 2) * c:(i % 2 + 1) * c, :] = seg

    row = _iota((c2, c2), 0)
    col = _iota((c2, c2), 1)
    same_head = (row >= c) == (col >= c)
    causal = (row >= col) & same_head
    strict = (row > col) & same_head
    second = (row >= c).astype(jnp.int32)
    left_cols = col < c
    eye = (row == col).astype(F32)
    tril = (_iota((c, c), 0) >= _iota((c, c), 1)).astype(F32)
    lane_row = _iota((1, c2), 1)
    rows1 = _iota((c2, 1), 0)
    rows2 = _iota((2 * c2, 1), 0)
    first_head_rows = (rows2 % c2) < c
    scale = HEAD_DIM ** -0.5
    pairs = range(npairs)

    def chunk_body(ci, carry):
        r0 = ci * c
        gc_stack, beta_stack = [], []
        for b in range(nb):
            ba = ba_ref[b, pl.ds(r0, c), :]
            beta_all = jax.nn.sigmoid(ba)
            g_all = -jnp.exp(gp_ref[0:1, :]) * _softplus(ba + gp_ref[1:2, :])
            gc_all = jnp.dot(tril, g_all, precision=HIGHEST,
                             preferred_element_type=F32)
            gc_stack.append(jnp.concatenate([gc_all, gc_all], axis=0))
            beta_stack.append(jnp.concatenate([beta_all, beta_all], axis=0))
            gct[b] = gc_stack[b].T
        bof = [u // ppb for u in pairs]
        h0 = [hblk * hb + 2 * (u % ppb) for u in pairs]
        bcol = [jnp.sum(jnp.where(col == h0[u] + second, beta_stack[bof[u]], 0.0), axis=-1,
                        keepdims=True) for u in pairs]
        gcol = [jnp.sum(jnp.where(col == h0[u] + n_heads + second, gc_stack[bof[u]], 0.0),
                        axis=-1, keepdims=True) for u in pairs]
        grow = [jnp.where(lane_row < c, gct[bof[u], pl.ds(h0[u] + n_heads, 1), :],
                          gct[bof[u], pl.ds(h0[u] + n_heads + 1, 1), :])
                for u in pairs]
        glast = [jnp.where(rows1 < c, g[c - 1:c, :], g[c2 - 1:c2, :]) for g in gcol]
        decay = [jnp.where(causal, jnp.exp(jnp.where(causal, gcol[p] - grow[p], 0.0)), 0.0)
                 for p in pairs]
        egc = [jnp.exp(g) for g in gcol]

        q = [qkvs[0, p, ci] * scale for p in pairs]
        k = [qkvs[1, p, ci] for p in pairs]
        v = [qkvs[2, p, ci] for p in pairs]
        kb = [k[p] * bcol[p] for p in pairs]
        kq = [_dot_nt(jnp.concatenate([kb[p], q[p]], axis=0), k[p]) for p in pairs]
        a_low = [jnp.where(strict, kq[p][:c2] * decay[p], 0.0) for p in pairs]
        attn = [kq[p][c2:] * decay[p] for p in pairs]

        t_inv = [eye - a for a in a_low]
        pw = a_low
        n = 2
        while n < c:
            pw = [_dot(x, x) for x in pw]
            t_inv = [t_inv[p] + _dot(t_inv[p], pw[p]) for p in pairs]
            n *= 2

        sol = [_dot(t_inv[p], jnp.concatenate([v[p] * bcol[p], kb[p] * egc[p]], axis=1))
               for p in pairs]
        k_dec = [k[p] * jnp.exp(glast[p] - gcol[p]) for p in pairs]
        s_prev = [state[p] for p in pairs]
        ws_qs = []
        for p in pairs:
            x = jnp.concatenate([sol[p][:, LANE:], q[p] * egc[p]], axis=0)
            x_wide = jnp.concatenate([jnp.where(first_head_rows, x, 0.0),
                                      jnp.where(first_head_rows, 0.0, x)], axis=1)
            ws_qs.append(_dot(x_wide, s_prev[p]))
        v_new = [sol[p][:, :LANE] - ws_qs[p][:c2] for p in pairs]
        o = [ws_qs[p][c2:] + _dot(attn[p], v_new[p]) for p in pairs]
        for p in pairs:
            kdt = k_dec[p].T
            kd_stack = jnp.concatenate([jnp.where(left_cols, kdt, 0.0),
                                        jnp.where(left_cols, 0.0, kdt)], axis=0)
            s_decay = jnp.where(rows2 < c2, jnp.exp(gcol[p][c - 1:c, :]),
                                jnp.exp(gcol[p][c2 - 1:c2, :]))
            state[p] = s_prev[p] * s_decay + _dot(kd_stack, v_new[p])
        for p in pairs:
            b, hp = bof[p], p % ppb
            cs0 = slice((2 * hp) * LANE, (2 * hp + 1) * LANE)
            cs1 = slice((2 * hp + 1) * LANE, (2 * hp + 2) * LANE)
            zg = _silu(jnp.concatenate([z_ref[b, pl.ds(r0, c), cs0],
                                        z_ref[b, pl.ds(r0, c), cs1]], axis=0))
            op = o[p] * lax.rsqrt(jnp.mean(o[p] * o[p], axis=-1, keepdims=True) + EPS)
            op = (op * nw_ref[...] * zg).astype(o_ref.dtype)
            o_ref[b, pl.ds(r0, c), cs0] = op[:c]
            o_ref[b, pl.ds(r0, c), cs1] = op[c:]
        return carry

    for ci in range(sc // c):
        chunk_body(ci, 0)


def _deltanet(proj3, ba3, conv_w, a_log, dt_bias, norm_w, n_heads, col0):
    b, s, _ = proj3.shape
    assert n_heads % 2 == 0 and 2 * n_heads <= LANE
    hb = _largest_tile(n_heads, 12, 2)
    width = hb * LANE
    nhb = n_heads // hb
    sc = _largest_tile(s, DN_SEQ_TILE, DN_CHUNK)
    base = col0 // width
    assert col0 % width == 0
    npairs = b * hb // 2
    gp = jnp.zeros((2, LANE), F32)
    gp = gp.at[0, n_heads:2 * n_heads].set(a_log).at[1, n_heads:2 * n_heads].set(dt_bias)

    def col_spec(group):
        return pl.BlockSpec((b, sc, width),
                            lambda hi, si, g=group: (0, si, base + g * nhb + hi))

    def cw_spec(group):
        return pl.BlockSpec((SHORT_CONV, width), lambda hi, si, g=group: (0, g * nhb + hi))

    return pl.pallas_call(
        functools.partial(_deltanet_kernel, sc=sc, hb=hb, nb=b, n_heads=n_heads),
        grid=(nhb, s // sc),
        in_specs=[col_spec(0), col_spec(1), col_spec(2), col_spec(3),
                  pl.BlockSpec((b, sc, LANE), lambda hi, si: (0, si, 0)),
                  cw_spec(0), cw_spec(1), cw_spec(2),
                  pl.BlockSpec((2, LANE), lambda hi, si: (0, 0)),
                  pl.BlockSpec((1, LANE), lambda hi, si: (0, 0))],
        out_specs=pl.BlockSpec((b, sc, width), lambda hi, si: (0, si, hi)),
        out_shape=jax.ShapeDtypeStruct((b, s, n_heads * LANE), BF16),
        scratch_shapes=[pltpu.VMEM((3 * b, sc + HALO, width), F32),
                        pltpu.VMEM((3, npairs, sc // DN_CHUNK, 2 * DN_CHUNK, LANE), F32),
                        pltpu.VMEM((b, LANE, LANE), F32),
                        pltpu.VMEM((npairs, 2 * HEAD_DIM, HEAD_DIM), F32)],
        compiler_params=_params("parallel", "arbitrary"),
        name="gated_deltanet",
    )(proj3, proj3, proj3, proj3, ba3, conv_w, conv_w, conv_w, gp, norm_w.reshape(1, LANE))


def _rglru_kernel(x_ref, g_ref, cw_ref, cb_ref, wa_ref, ba_ref, wx_ref, bx_ref, lam_ref,
                  nw_ref, o_ref, xpad, abuf, bbuf, hcarry, *, sc, gb):
    c = SCAN_CHUNK
    is_first = pl.program_id(2) == 0

    @pl.when(is_first)
    def _():
        hcarry[...] = jnp.zeros(hcarry.shape, F32)

    _load_with_halo(xpad, 0, x_ref, is_first, sc)

    rows = 2 * c
    for ci in range(sc // rows):
        xc = _short_conv(xpad, 0, cw_ref, ci * rows, rows) + cb_ref[...]
        for g in range(gb):
            cs = slice(g * LANE, (g + 1) * LANE)
            xg = xc[:, cs]
            r = jax.nn.sigmoid(_dot(xg, wa_ref[g]) + ba_ref[:, cs])
            ig = jax.nn.sigmoid(_dot(xg, wx_ref[g]) + bx_ref[:, cs])
            a = jnp.exp(-LRU_C * r * _softplus(-lam_ref[:, cs]))
            abuf[ci * rows:(ci + 1) * rows, cs] = a
            bbuf[ci * rows:(ci + 1) * rows, cs] = (
                jnp.sqrt((1.0 - a) * (1.0 + a)) * (ig * xg))

    row = _iota((HALO, LANE), 0)

    def chunk_body(ci, carry):
        r0 = pl.multiple_of(ci * c, c)
        for g in range(gb):
            cs = slice(g * LANE, (g + 1) * LANE)
            a = abuf[pl.ds(r0, c), cs]
            bb = bbuf[pl.ds(r0, c), cs]
            last = jnp.broadcast_to(hcarry[:, cs], (HALO, LANE))
            tiles = []
            for v in range(c // HALO):
                av = a[v * HALO:(v + 1) * HALO]
                bv = bb[v * HALO:(v + 1) * HALO]
                d = 1
                while d < HALO:
                    keep = row >= d
                    a_sh = jnp.where(keep, pltpu.roll(av, d, 0), 1.0)
                    b_sh = jnp.where(keep, pltpu.roll(bv, d, 0), 0.0)
                    bv = av * b_sh + bv
                    av = av * a_sh
                    d *= 2
                hv = bv + av * last
                last = jnp.broadcast_to(hv[HALO - 1:HALO, :], (HALO, LANE))
                tiles.append(hv)
            h = jnp.concatenate(tiles, axis=0)
            hcarry[:, cs] = last[0:1, :]
            y = h * jax.nn.gelu(g_ref[pl.ds(r0, c), cs])
            y = y * lax.rsqrt(jnp.mean(y * y, axis=-1, keepdims=True) + EPS)
            o_ref[pl.ds(r0, c), cs] = (y * nw_ref[:, cs]).astype(o_ref.dtype)
        return carry

    lax.fori_loop(0, sc // c, chunk_body, 0)


def _rglru(proj3, conv_w, conv_b, w_a, b_a, w_x, b_x, lam, norm_w, col_x, col_g):
    b, s, _ = proj3.shape
    nblk = w_a.shape[0]
    gb = 4 if nblk % 4 == 0 else (3 if nblk % 3 == 0 else 1)
    width = gb * LANE
    lw = nblk * LANE
    sc = _largest_tile(s, 512, 2 * SCAN_CHUNK)
    assert col_x % width == 0 and col_g % width == 0
    bx0, bg0 = col_x // width, col_g // width
    vec = lambda: pl.BlockSpec((1, width), lambda bi, gi, si: (0, gi))
    mat = lambda: pl.BlockSpec((gb, LANE, LANE), lambda bi, gi, si: (gi, 0, 0))
    return pl.pallas_call(
        functools.partial(_rglru_kernel, sc=sc, gb=gb),
        grid=(b, nblk // gb, s // sc),
        in_specs=[pl.BlockSpec((None, sc, width), lambda bi, gi, si: (bi, si, bx0 + gi)),
                  pl.BlockSpec((None, sc, width), lambda bi, gi, si: (bi, si, bg0 + gi)),
                  pl.BlockSpec((SHORT_CONV, width), lambda bi, gi, si: (0, gi)),
                  vec(), mat(), vec(), mat(), vec(), vec(), vec()],
        out_specs=pl.BlockSpec((None, sc, width), lambda bi, gi, si: (bi, si, gi)),
        out_shape=jax.ShapeDtypeStruct((b, s, lw), BF16),
        scratch_shapes=[pltpu.VMEM((1, sc + HALO, width), F32),
                        pltpu.VMEM((sc, width), F32),
                        pltpu.VMEM((sc, width), F32),
                        pltpu.VMEM((1, width), F32)],
        compiler_params=_params("parallel", "parallel", "arbitrary"),
        name="rglru",
    )(proj3, proj3, conv_w, conv_b.reshape(1, lw), w_a, b_a.reshape(1, lw), w_x,
      b_x.reshape(1, lw), lam.reshape(1, lw), norm_w.reshape(1, lw))


def _sg_kernel(u_ref, v_ref, lnw_ref, lnb_ref, ws_ref, bst_ref, nw_ref, o_ref, *, sc,
               groups):
    t = SG_CHUNK
    tril = _iota((t, t), 0) >= _iota((t, t), 1)
    for ci in range(sc // t):
        rs = slice(ci * t, (ci + 1) * t)
        v = jax.nn.gelu(v_ref[rs, :])
        mu = jnp.mean(v, axis=-1, keepdims=True)
        vc = v - mu
        var = jnp.mean(vc * vc, axis=-1, keepdims=True)
        vn = vc * lax.rsqrt(var + EPS) * lnw_ref[...] + lnb_ref[...]
        for g in range(groups):
            cs = slice(g * LANE, (g + 1) * LANE)
            w_causal = jnp.where(tril, ws_ref[g], 0.0)
            z = _dot(w_causal, vn[:, cs]) + bst_ref[:, g:g + 1]
            y = jax.nn.gelu(u_ref[rs, cs]) * z
            y = y * lax.rsqrt(jnp.mean(y * y, axis=-1, keepdims=True) + EPS)
            o_ref[rs, cs] = (y * nw_ref[:, cs]).astype(o_ref.dtype)


def _spatial_gating(proj3, ln_w, ln_b, w_s, b_s, norm_w, col_u, col_v):
    b, s, _ = proj3.shape
    groups = w_s.shape[0]
    width = groups * LANE
    sc = _largest_tile(s, 512, SG_CHUNK)
    assert col_u % width == 0 and col_v % width == 0
    bu0, bv0 = col_u // width, col_v // width
    vec = lambda: pl.BlockSpec((1, width), lambda bi, si: (0, 0))
    return pl.pallas_call(
        functools.partial(_sg_kernel, sc=sc, groups=groups),
        grid=(b, s // sc),
        in_specs=[pl.BlockSpec((None, sc, width), lambda bi, si: (bi, si, bu0)),
                  pl.BlockSpec((None, sc, width), lambda bi, si: (bi, si, bv0)),
                  vec(), vec(),
                  pl.BlockSpec((groups, SG_CHUNK, SG_CHUNK), lambda bi, si: (0, 0, 0)),
                  pl.BlockSpec((SG_CHUNK, groups), lambda bi, si: (0, 0)),
                  vec()],
        out_specs=pl.BlockSpec((None, sc, width), lambda bi, si: (bi, si, 0)),
        out_shape=jax.ShapeDtypeStruct((b, s, width), BF16),
        compiler_params=_params("parallel", "parallel"),
        name="spatial_gating",
    )(proj3, proj3, ln_w.reshape(1, width), ln_b.reshape(1, width), w_s, b_s.T,
      norm_w.reshape(1, width))


def kernel(x, norm_mix, w_in, dn_conv_w, dn_a_log, dn_dt_bias, dn_norm_w, lru_conv_w, lru_conv_b, lru_w_a, lru_b_a, lru_w_x, lru_b_x, lru_lambda, lru_norm_w, sg_ln_w, sg_ln_b, sg_w_s, sg_b_s, sg_norm_w, w_out, norm_ffn, w_up, ffn_conv_w, ffn_conv_b, w_down, norm_final):
    bsz, seq, d = x.shape
    depth = w_in.shape[0]
    n_heads = dn_a_log.shape[1]
    dn_w = n_heads * HEAD_DIM
    lru_w = lru_lambda.shape[1]
    sg_w = sg_ln_w.shape[1]
    m = bsz * seq
    gate0 = 4 * dn_w
    tail0 = gate0 + 2 * n_heads
    main_cols = gate0 + 2 * lru_w + 2 * sg_w
    tn = IN_PROJ_TILE
    assert gate0 % tn == 0 and tail0 % HALO == 0
    w_in_t = jnp.swapaxes(w_in, 1, 2)

    def main_rows(j):
        skip = jnp.where(j < gate0 // tn, 0, (tail0 - gate0) // HALO)
        return HALO * (j * (tn // HALO) + skip)

    x2 = x.reshape(m, d)
    for l in range(depth):
        h = _rmsnorm(x2, norm_mix[l], BF16)
        proj, w_out_bf16 = _in_proj(h, w_in_t, l, main_rows, main_cols, tn, "in_proj",
                                    side=w_out)
        ba = _in_proj(h, w_in_t, l, lambda j: gate0, LANE, LANE, "in_proj_gates")
        proj3 = proj.reshape(bsz, seq, main_cols)
        ba3 = ba.reshape(bsz, seq, LANE)

        y_a = _deltanet(proj3, ba3, dn_conv_w[l], dn_a_log[l], dn_dt_bias[l],
                        dn_norm_w[l], n_heads, 0)
        y_b = _rglru(proj3, lru_conv_w[l], lru_conv_b[l], lru_w_a[l], lru_b_a[l],
                     lru_w_x[l], lru_b_x[l], lru_lambda[l], lru_norm_w[l],
                     gate0, gate0 + lru_w)
        y_c = _spatial_gating(proj3, sg_ln_w[l], sg_ln_b[l], sg_w_s[l], sg_b_s[l],
                              sg_norm_w[l], gate0 + 2 * lru_w, gate0 + 2 * lru_w + sg_w)
        mix_parts = [y.reshape(m, y.shape[-1]) for y in (y_a, y_b, y_c)]
        x2 = _out_proj(mix_parts, w_out_bf16, x2, "out_proj")

        h = _rmsnorm(x2, norm_ffn[l], BF16)
        act, w_down_bf16 = _ffn_up(h, w_up, w_down, l, ffn_conv_w[l], ffn_conv_b[l], seq)
        x2 = _matmul(act, w_down_bf16, residual=x2, tk_cap=w_down_bf16.shape[0],
                     name="down_proj")
    return _rmsnorm(x2, norm_final, F32).reshape(bsz, seq, d)
```

```python
import functools

import jax
import jax.numpy as jnp
from jax import lax
from jax.experimental import pallas as pl
from jax.experimental.pallas import tpu as pltpu

F32 = jnp.float32
BF16 = jnp.bfloat16
HIGHEST = lax.Precision.HIGHEST

EPS = 1e-6
LANE = 128
HALO = 8
HEAD_DIM = LANE
DN_CHUNK = 64
DN_SEQ_TILE = 256
SG_CHUNK = 128
SCAN_CHUNK = 64
SHORT_CONV = 4
LRU_C = 8.0
VMEM_LIMIT_BYTES = 60 * 1024 * 1024


def _params(*semantics):
    return pltpu.CompilerParams(dimension_semantics=semantics,
                                vmem_limit_bytes=VMEM_LIMIT_BYTES)


def _largest_tile(n, cap, unit):
    best = None
    t = unit
    while t <= min(n, cap):
        if n % t == 0:
            best = t
        t += unit
    assert best is not None, (n, cap, unit)
    return best


def _dot(a, b):
    return jnp.dot(a.astype(BF16), b.astype(BF16), preferred_element_type=F32)


def _dot_nt(a, b):
    return lax.dot_general(a.astype(BF16), b.astype(BF16), (((1,), (1,)), ((), ())),
                           preferred_element_type=F32)


def _softplus(x):
    return jnp.maximum(x, 0.0) + jnp.log1p(jnp.exp(-jnp.abs(x)))


def _silu(x):
    return x * jax.nn.sigmoid(x)


def _iota(shape, dim):
    return lax.broadcasted_iota(jnp.int32, shape, dim)


def _rmsnorm_kernel(x_ref, w_ref, o_ref):
    x = x_ref[...]
    ms = jnp.mean(x * x, axis=-1, keepdims=True)
    o_ref[...] = (x * lax.rsqrt(ms + EPS) * w_ref[...]).astype(o_ref.dtype)


def _rmsnorm(x2, w, out_dtype):
    m, d = x2.shape
    tm = _largest_tile(m, 512, 16)
    return pl.pallas_call(
        _rmsnorm_kernel,
        grid=(m // tm,),
        in_specs=[pl.BlockSpec((tm, d), lambda i: (i, 0)),
                  pl.BlockSpec((1, d), lambda i: (0, 0))],
        out_specs=pl.BlockSpec((tm, d), lambda i: (i, 0)),
        out_shape=jax.ShapeDtypeStruct((m, d), out_dtype),
        compiler_params=_params("parallel"),
        name="rmsnorm",
    )(x2, w.reshape(1, d))


def _matmul_kernel(*refs, nk, has_res):
    if has_res:
        a_ref, w_ref, res_ref, o_ref = refs[:4]
    else:
        a_ref, w_ref, o_ref = refs[:3]
        res_ref = None
    part = jnp.dot(a_ref[...], w_ref[...], preferred_element_type=F32)

    def finish(acc):
        if res_ref is not None:
            acc = acc + res_ref[...]
        o_ref[...] = acc.astype(o_ref.dtype)

    if nk == 1:
        finish(part)
        return
    acc_ref = refs[-1]
    k = pl.program_id(2)

    @pl.when(k == 0)
    def _():
        acc_ref[...] = part

    @pl.when(k > 0)
    def _():
        acc_ref[...] += part

    @pl.when(k == nk - 1)
    def _():
        finish(acc_ref[...])


def _matmul(a, w, *, residual=None, out_dtype=F32, tm_cap=512, tn_cap=512, tk_cap=4096,
            name="matmul"):
    m, k = a.shape
    n = w.shape[1]
    tm = _largest_tile(m, tm_cap, 16)
    tn = _largest_tile(n, tn_cap, LANE)
    tk = _largest_tile(k, tk_cap, LANE)
    nk = k // tk
    in_specs = [pl.BlockSpec((tm, tk), lambda j, i, kk: (i, kk)),
                pl.BlockSpec((tk, tn), lambda j, i, kk: (kk, j))]
    args = [a, w]
    if residual is not None:
        in_specs.append(pl.BlockSpec((tm, tn), lambda j, i, kk: (i, j)))
        args.append(residual)
    scratch = [pltpu.VMEM((tm, tn), F32)] if nk > 1 else []
    return pl.pallas_call(
        functools.partial(_matmul_kernel, nk=nk, has_res=residual is not None),
        grid=(n // tn, m // tm, nk),
        in_specs=in_specs,
        out_specs=pl.BlockSpec((tm, tn), lambda j, i, kk: (i, j)),
        out_shape=jax.ShapeDtypeStruct((m, n), out_dtype),
        scratch_shapes=scratch,
        compiler_params=_params("parallel", "parallel", "arbitrary"),
        name=name,
    )(*args)


CAST_ROWS = 256
IN_PROJ_TILE = 512
IN_PROJ_ROWS = 1024


def _cast_rows(src_ref, dst_ref):
    step = min(CAST_ROWS, src_ref.shape[0])
    assert src_ref.shape[0] % step == 0

    def body(r, carry):
        rows = pl.ds(pl.multiple_of(r * step, step), step)
        dst_ref[rows, :] = src_ref[rows, :].astype(dst_ref.dtype)
        return carry

    lax.fori_loop(0, src_ref.shape[0] // step, body, 0)


def _weight_spec(w, layer, k, tn, col_block0=0):
    if layer is None:
        return pl.BlockSpec((k, tn), lambda j, i: (0, col_block0 + j))
    return pl.BlockSpec((None, k, tn), lambda j, i: (layer, 0, col_block0 + j))


def _out_proj_kernel(*refs, widths):
    a_refs = refs[:len(widths)]
    w_ref, res_ref, o_ref = refs[len(widths):]
    acc = res_ref[...]
    k0 = 0
    for a_ref, wk in zip(a_refs, widths):
        acc = acc + jnp.dot(a_ref[...], w_ref[k0:k0 + wk, :], preferred_element_type=F32)
        k0 += wk
    o_ref[...] = acc


def _out_proj(a_parts, w, residual, name):
    m = a_parts[0].shape[0]
    widths = tuple(a.shape[1] for a in a_parts)
    k, n = w.shape
    assert k == sum(widths)
    tm = _largest_tile(m, 1024, 16)
    tn = _largest_tile(n, 1024, LANE)
    in_specs = [pl.BlockSpec((tm, wk), lambda j, i: (i, 0)) for wk in widths]
    in_specs += [pl.BlockSpec((k, tn), lambda j, i: (0, j)),
                 pl.BlockSpec((tm, tn), lambda j, i: (i, j))]
    return pl.pallas_call(
        functools.partial(_out_proj_kernel, widths=widths),
        grid=(n // tn, m // tm),
        in_specs=in_specs,
        out_specs=pl.BlockSpec((tm, tn), lambda j, i: (i, j)),
        out_shape=jax.ShapeDtypeStruct((m, n), F32),
        compiler_params=_params("arbitrary", "arbitrary"),
        name=name,
    )(*a_parts, w, residual)


def _in_proj_kernel(a_ref, wt_ref, *rest, n_side):
    side_refs = rest[:n_side]
    o_ref = rest[n_side]
    side_out_refs = rest[n_side + 1:2 * n_side + 1]
    wb_ref = rest[-1]

    for src, dst in zip(side_refs, side_out_refs):
        dst[...] = src[...].astype(dst.dtype)

    @pl.when(pl.program_id(1) == 0)
    def _():
        _cast_rows(wt_ref.at[0], wb_ref)

    o_ref[...] = lax.dot_general(a_ref[...], wb_ref[...], (((1,), (1,)), ((), ())),
                                 preferred_element_type=F32)


def _in_proj(a, w_t, layer, row_start, n, tn, name, sides=()):
    m, k = a.shape
    assert n % tn == 0
    tm = _largest_tile(m, IN_PROJ_ROWS, 16)
    row_tiles = m // tm
    n_steps = (n // tn) * row_tiles
    in_specs = [pl.BlockSpec((tm, k), lambda j, i: (i, 0)),
                pl.BlockSpec((pl.Element(1), pl.Element(tn), pl.Element(k)),
                             lambda j, i: (layer, row_start(j), 0))]
    out_specs = [pl.BlockSpec((tm, tn), lambda j, i: (i, j))]
    out_shape = [jax.ShapeDtypeStruct((m, n), F32)]
    for side in sides:
        rows, cols = side.shape[1:]
        slab_rows = min(s for s in range(16, rows + 1, 16)
                        if rows % s == 0 and rows // s <= n_steps)

        def slab(j, i, n_slabs=rows // slab_rows):
            return jnp.minimum(j * row_tiles + i, n_slabs - 1)

        in_specs.append(pl.BlockSpec((None, slab_rows, cols),
                                     lambda j, i, slab=slab: (layer, slab(j, i), 0)))
        out_specs.append(pl.BlockSpec((slab_rows, cols),
                                      lambda j, i, slab=slab: (slab(j, i), 0)))
        out_shape.append(jax.ShapeDtypeStruct((rows, cols), BF16))
    return pl.pallas_call(
        functools.partial(_in_proj_kernel, n_side=len(sides)),
        grid=(n // tn, row_tiles),
        in_specs=in_specs,
        out_specs=out_specs,
        out_shape=out_shape,
        scratch_shapes=[pltpu.VMEM((tn, k), BF16)],
        compiler_params=_params("arbitrary", "arbitrary"),
        name=name,
    )(a, w_t, *sides)


FFN_ROW_TILE = 2048
FFN_ROW_BLOCK = 512


def _ffn_up_kernel(a_ref, wg_ref, wu_ref, cwg_ref, cwu_ref, bg_ref, bu_ref, wd_ref,
                   o_ref, wdb_ref, gbuf, ubuf, *, tm, rb, tiles_per_seq):
    i = pl.program_id(1)
    taps = cwg_ref.shape[0]

    wdb_ref[...] = wd_ref[...].astype(wdb_ref.dtype)

    first = (i % tiles_per_seq) == 0
    for buf in (gbuf, ubuf):
        @pl.when(first)
        def _():
            buf[0:HALO, :] = jnp.zeros((HALO, buf.shape[1]), F32)

        @pl.when(jnp.logical_not(first))
        def _():
            buf[0:HALO, :] = buf[tm:tm + HALO, :]

    def conv(buf, cw_ref, r0):
        p = buf[pl.ds(r0, rb + HALO), :]
        acc = cw_ref[0:1, :] * p
        for j in range(1, taps):
            acc = cw_ref[j:j + 1, :] * p + pltpu.roll(acc, 1, 0)
        return acc[HALO:]

    for r in range(tm // rb):
        a = a_ref[r * rb:(r + 1) * rb, :]
        gbuf[HALO + r * rb:HALO + (r + 1) * rb, :] = jnp.dot(
            a, wg_ref[...], preferred_element_type=F32)
        ubuf[HALO + r * rb:HALO + (r + 1) * rb, :] = jnp.dot(
            a, wu_ref[...], preferred_element_type=F32)
        g = conv(gbuf, cwg_ref, r * rb) + bg_ref[...]
        u = conv(ubuf, cwu_ref, r * rb) + bu_ref[...]
        o_ref[r * rb:(r + 1) * rb, :] = (_silu(g) * u).astype(o_ref.dtype)


def _ffn_up(h, w_up, w_down, layer, conv_w, conv_b, seq):
    m, k = h.shape
    f = w_up.shape[-1] // 2
    d_out = w_down.shape[-1]
    taps = conv_w.shape[0]
    tm = _largest_tile(seq, FFN_ROW_TILE, FFN_ROW_BLOCK)
    tf = _largest_tile(f, 256, LANE)
    nf = f // tf
    row_tiles = m // tm
    slab = f // (nf * row_tiles)
    assert slab * nf * row_tiles == f and slab % 16 == 0
    conv_b = conv_b.reshape(1, 2 * f)
    return pl.pallas_call(
        functools.partial(_ffn_up_kernel, tm=tm, rb=FFN_ROW_BLOCK, tiles_per_seq=seq // tm),
        grid=(nf, row_tiles),
        in_specs=[pl.BlockSpec((tm, k), lambda j, i: (i, 0)),
                  pl.BlockSpec((k, tf), lambda j, i: (0, j)),
                  pl.BlockSpec((k, tf), lambda j, i: (0, nf + j)),
                  pl.BlockSpec((taps, tf), lambda j, i: (0, j)),
                  pl.BlockSpec((taps, tf), lambda j, i: (0, nf + j)),
                  pl.BlockSpec((1, tf), lambda j, i: (0, j)),
                  pl.BlockSpec((1, tf), lambda j, i: (0, nf + j)),
                  pl.BlockSpec((None, slab, d_out),
                               lambda j, i: (layer, j * row_tiles + i, 0))],
        out_specs=[pl.BlockSpec((tm, tf), lambda j, i: (i, j)),
                   pl.BlockSpec((slab, d_out), lambda j, i: (j * row_tiles + i, 0))],
        out_shape=[jax.ShapeDtypeStruct((m, f), BF16),
                   jax.ShapeDtypeStruct((f, d_out), BF16)],
        scratch_shapes=[pltpu.VMEM((tm + HALO, tf), F32),
                        pltpu.VMEM((tm + HALO, tf), F32)],
        compiler_params=_params("arbitrary", "arbitrary"),
        name="ffn_up_conv_gate",
    )(h, w_up, w_up, conv_w, conv_w, conv_b, conv_b, w_down)


def _load_with_halo(xpad, idx, src_ref, is_first, rows):
    width = xpad.shape[-1]

    @pl.when(is_first)
    def _():
        xpad[idx, 0:HALO, :] = jnp.zeros((HALO, width), F32)

    @pl.when(jnp.logical_not(is_first))
    def _():
        xpad[idx, 0:HALO, :] = xpad[idx, rows:rows + HALO, :]

    xpad[idx, HALO:, :] = src_ref[...]


def _short_conv(xpad, idx, cw_ref, r0, rows, cs=slice(None)):
    p = xpad[idx, pl.ds(r0, rows + HALO), cs]
    acc = cw_ref[0:1, cs] * p
    for j in range(1, cw_ref.shape[0]):
        acc = cw_ref[j:j + 1, cs] * p + pltpu.roll(acc, 1, 0)
    return acc[HALO:]


def _deltanet_kernel(q_ref, k_ref, v_ref, z_ref, ba_ref, cwq_ref, cwk_ref, cwv_ref,
                     gp_ref, nw_ref, o_ref, xpad, qkvs, gct, state, *, sc, hb, nb, n_heads):
    c = DN_CHUNK
    c2 = 2 * c
    ppb = hb // 2
    npairs = nb * ppb
    hblk = pl.program_id(0)
    is_first = pl.program_id(1) == 0

    @pl.when(is_first)
    def _():
        state[...] = jnp.zeros(state.shape, F32)

    for idx, src in enumerate((q_ref, k_ref, v_ref)):
        for b in range(nb):
            _load_with_halo(xpad, idx * nb + b, src.at[b], is_first, sc)

    for idx, cw_ref in enumerate((cwq_ref, cwk_ref, cwv_ref)):
        for b in range(nb):
            for ci in range(sc // c):
                for i in range(hb):
                    cs = slice(i * LANE, (i + 1) * LANE)
                    seg = _silu(_short_conv(xpad, idx * nb + b, cw_ref, ci * c, c, cs))
                    if idx < 2:
                        seg = seg * lax.rsqrt(
                            jnp.sum(seg * seg, axis=-1, keepdims=True) + EPS)
                    qkvs[idx, b * ppb + i // 2, ci, (i % 2) * c:(i % 2 + 1) * c, :] = seg

    row = _iota((c2, c2), 0)
    col = _iota((c2, c2), 1)
    same_head = (row >= c) == (col >= c)
    causal = (row >= col) & same_head
    strict = (row > col) & same_head
    second = (row >= c).astype(jnp.int32)
    left_cols = col < c
    eye = (row == col).astype(F32)
    tril = (_iota((c, c), 0) >= _iota((c, c), 1)).astype(F32)
    lane_row = _iota((1, c2), 1)
    rows1 = _iota((c2, 1), 0)
    rows2 = _iota((2 * c2, 1), 0)
    first_head_rows = (rows2 % c2) < c
    scale = HEAD_DIM ** -0.5
    pairs = range(npairs)

    def chunk_body(ci, carry):
        r0 = ci * c
        gc_stack, beta_stack = [], []
        for b in range(nb):
            ba = ba_ref[b, pl.ds(r0, c), :]
            beta_all = jax.nn.sigmoid(ba)
            g_all = -jnp.exp(gp_ref[0:1, :]) * _softplus(ba + gp_ref[1:2, :])
            gc_all = jnp.dot(tril, g_all, precision=HIGHEST,
                             preferred_element_type=F32)
            gc_stack.append(jnp.concatenate([gc_all, gc_all], axis=0))
            beta_stack.append(jnp.concatenate([beta_all, beta_all], axis=0))
            gct[b] = gc_stack[b].T
        bof = [u // ppb for u in pairs]
        h0 = [hblk * hb + 2 * (u % ppb) for u in pairs]
        bcol = [jnp.sum(jnp.where(col == h0[u] + second, beta_stack[bof[u]], 0.0), axis=-1,
                        keepdims=True) for u in pairs]
        gcol = [jnp.sum(jnp.where(col == h0[u] + n_heads + second, gc_stack[bof[u]], 0.0),
                        axis=-1, keepdims=True) for u in pairs]
        grow = [jnp.where(lane_row < c, gct[bof[u], pl.ds(h0[u] + n_heads, 1), :],
                          gct[bof[u], pl.ds(h0[u] + n_heads + 1, 1), :])
                for u in pairs]
        glast = [jnp.where(rows1 < c, g[c - 1:c, :], g[c2 - 1:c2, :]) for g in gcol]
        decay = [jnp.where(causal, jnp.exp(jnp.where(causal, gcol[p] - grow[p], 0.0)), 0.0)
                 for p in pairs]
        egc = [jnp.exp(g) for g in gcol]

        q = [qkvs[0, p, ci] * scale for p in pairs]
        k = [qkvs[1, p, ci] for p in pairs]
        v = [qkvs[2, p, ci] for p in pairs]
        kb = [k[p] * bcol[p] for p in pairs]
        kq = [_dot_nt(jnp.concatenate([kb[p], q[p]], axis=0), k[p]) for p in pairs]
        a_low = [jnp.where(strict, kq[p][:c2] * decay[p], 0.0) for p in pairs]
        attn = [kq[p][c2:] * decay[p] for p in pairs]

        t_inv = [eye - a for a in a_low]
        pw = a_low
        n = 2
        while n < c:
            pw = [_dot(x, x) for x in pw]
            t_inv = [t_inv[p] + _dot(t_inv[p], pw[p]) for p in pairs]
            n *= 2

        sol = [_dot(t_inv[p], jnp.concatenate([v[p] * bcol[p], kb[p] * egc[p]], axis=1))
               for p in pairs]
        k_dec = [k[p] * jnp.exp(glast[p] - gcol[p]) for p in pairs]
        s_prev = [state[p] for p in pairs]
        ws_qs = []
        for p in pairs:
            x = jnp.concatenate([sol[p][:, LANE:], q[p] * egc[p]], axis=0)
            x_wide = jnp.concatenate([jnp.where(first_head_rows, x, 0.0),
                                      jnp.where(first_head_rows, 0.0, x)], axis=1)
            ws_qs.append(_dot(x_wide, s_prev[p]))
        v_new = [sol[p][:, :LANE] - ws_qs[p][:c2] for p in pairs]
        o = [ws_qs[p][c2:] + _dot(attn[p], v_new[p]) for p in pairs]
        for p in pairs:
            kdt = k_dec[p].T
            kd_stack = jnp.concatenate([jnp.where(left_cols, kdt, 0.0),
                                        jnp.where(left_cols, 0.0, kdt)], axis=0)
            s_decay = jnp.where(rows2 < c2, jnp.exp(gcol[p][c - 1:c, :]),
                                jnp.exp(gcol[p][c2 - 1:c2, :]))
            state[p] = s_prev[p] * s_decay + _dot(kd_stack, v_new[p])
        for p in pairs:
            b, hp = bof[p], p % ppb
            cs0 = slice((2 * hp) * LANE, (2 * hp + 1) * LANE)
            cs1 = slice((2 * hp + 1) * LANE, (2 * hp + 2) * LANE)
            zg = _silu(jnp.concatenate([z_ref[b, pl.ds(r0, c), cs0],
                                        z_ref[b, pl.ds(r0, c), cs1]], axis=0))
            op = o[p] * lax.rsqrt(jnp.mean(o[p] * o[p], axis=-1, keepdims=True) + EPS)
            op = (op * nw_ref[...] * zg).astype(o_ref.dtype)
            o_ref[b, pl.ds(r0, c), cs0] = op[:c]
            o_ref[b, pl.ds(r0, c), cs1] = op[c:]
        return carry

    for ci in range(sc // c):
        chunk_body(ci, 0)


def _deltanet(proj3, ba3, conv_w, a_log, dt_bias, norm_w, n_heads, col0):
    b, s, _ = proj3.shape
    assert n_heads % 2 == 0 and 2 * n_heads <= LANE
    hb = _largest_tile(n_heads, 12, 2)
    width = hb * LANE
    nhb = n_heads // hb
    sc = _largest_tile(s, DN_SEQ_TILE, DN_CHUNK)
    base = col0 // width
    assert col0 % width == 0
    npairs = b * hb // 2
    gp = jnp.zeros((2, LANE), F32)
    gp = gp.at[0, n_heads:2 * n_heads].set(a_log).at[1, n_heads:2 * n_heads].set(dt_bias)

    def col_spec(group):
        return pl.BlockSpec((b, sc, width),
                            lambda hi, si, g=group: (0, si, base + g * nhb + hi))

    def cw_spec(group):
        return pl.BlockSpec((SHORT_CONV, width), lambda hi, si, g=group: (0, g * nhb + hi))

    return pl.pallas_call(
        functools.partial(_deltanet_kernel, sc=sc, hb=hb, nb=b, n_heads=n_heads),
        grid=(nhb, s // sc),
        in_specs=[col_spec(0), col_spec(1), col_spec(2), col_spec(3),
                  pl.BlockSpec((b, sc, LANE), lambda hi, si: (0, si, 0)),
                  cw_spec(0), cw_spec(1), cw_spec(2),
                  pl.BlockSpec((2, LANE), lambda hi, si: (0, 0)),
                  pl.BlockSpec((1, LANE), lambda hi, si: (0, 0))],
        out_specs=pl.BlockSpec((b, sc, width), lambda hi, si: (0, si, hi)),
        out_shape=jax.ShapeDtypeStruct((b, s, n_heads * LANE), BF16),
        scratch_shapes=[pltpu.VMEM((3 * b, sc + HALO, width), F32),
                        pltpu.VMEM((3, npairs, sc // DN_CHUNK, 2 * DN_CHUNK, LANE), F32),
                        pltpu.VMEM((b, LANE, LANE), F32),
                        pltpu.VMEM((npairs, 2 * HEAD_DIM, HEAD_DIM), F32)],
        compiler_params=_params("parallel", "arbitrary"),
        name="gated_deltanet",
    )(proj3, proj3, proj3, proj3, ba3, conv_w, conv_w, conv_w, gp, norm_w.reshape(1, LANE))


def _rglru_kernel(x_ref, g_ref, cw_ref, cb_ref, wa_ref, ba_ref, wx_ref, bx_ref, lam_ref,
                  nw_ref, o_ref, xpad, abuf, bbuf, hcarry, *, sc, gb):
    c = SCAN_CHUNK
    is_first = pl.program_id(2) == 0

    @pl.when(is_first)
    def _():
        hcarry[...] = jnp.zeros(hcarry.shape, F32)

    _load_with_halo(xpad, 0, x_ref, is_first, sc)

    rows = 2 * c
    for ci in range(sc // rows):
        xc = _short_conv(xpad, 0, cw_ref, ci * rows, rows) + cb_ref[...]
        for g in range(gb):
            cs = slice(g * LANE, (g + 1) * LANE)
            xg = xc[:, cs]
            r = jax.nn.sigmoid(_dot(xg, wa_ref[g]) + ba_ref[:, cs])
            ig = jax.nn.sigmoid(_dot(xg, wx_ref[g]) + bx_ref[:, cs])
            a = jnp.exp(-LRU_C * r * _softplus(-lam_ref[:, cs]))
            abuf[ci * rows:(ci + 1) * rows, cs] = a
            bbuf[ci * rows:(ci + 1) * rows, cs] = (
                jnp.sqrt((1.0 - a) * (1.0 + a)) * (ig * xg))

    row = _iota((HALO, LANE), 0)

    def chunk_body(ci, carry):
        r0 = pl.multiple_of(ci * c, c)
        for g in range(gb):
            cs = slice(g * LANE, (g + 1) * LANE)
            a = abuf[pl.ds(r0, c), cs]
            bb = bbuf[pl.ds(r0, c), cs]
            last = jnp.broadcast_to(hcarry[:, cs], (HALO, LANE))
            tiles = []
            for v in range(c // HALO):
                av = a[v * HALO:(v + 1) * HALO]
                bv = bb[v * HALO:(v + 1) * HALO]
                d = 1
                while d < HALO:
                    keep = row >= d
                    a_sh = jnp.where(keep, pltpu.roll(av, d, 0), 1.0)
                    b_sh = jnp.where(keep, pltpu.roll(bv, d, 0), 0.0)
                    bv = av * b_sh + bv
                    av = av * a_sh
                    d *= 2
                hv = bv + av * last
                last = jnp.broadcast_to(hv[HALO - 1:HALO, :], (HALO, LANE))
                tiles.append(hv)
            h = jnp.concatenate(tiles, axis=0)
            hcarry[:, cs] = last[0:1, :]
            y = h * jax.nn.gelu(g_ref[pl.ds(r0, c), cs])
            y = y * lax.rsqrt(jnp.mean(y * y, axis=-1, keepdims=True) + EPS)
            o_ref[pl.ds(r0, c), cs] = (y * nw_ref[:, cs]).astype(o_ref.dtype)
        return carry

    lax.fori_loop(0, sc // c, chunk_body, 0)


def _rglru(proj3, conv_w, conv_b, w_a, b_a, w_x, b_x, lam, norm_w, col_x, col_g):
    b, s, _ = proj3.shape
    nblk = w_a.shape[0]
    gb = 4 if nblk % 4 == 0 else (3 if nblk % 3 == 0 else 1)
    width = gb * LANE
    lw = nblk * LANE
    sc = _largest_tile(s, 512, 2 * SCAN_CHUNK)
    assert col_x % width == 0 and col_g % width == 0
    bx0, bg0 = col_x // width, col_g // width
    vec = lambda: pl.BlockSpec((1, width), lambda bi, gi, si: (0, gi))
    mat = lambda: pl.BlockSpec((gb, LANE, LANE), lambda bi, gi, si: (gi, 0, 0))
    return pl.pallas_call(
        functools.partial(_rglru_kernel, sc=sc, gb=gb),
        grid=(b, nblk // gb, s // sc),
        in_specs=[pl.BlockSpec((None, sc, width), lambda bi, gi, si: (bi, si, bx0 + gi)),
                  pl.BlockSpec((None, sc, width), lambda bi, gi, si: (bi, si, bg0 + gi)),
                  pl.BlockSpec((SHORT_CONV, width), lambda bi, gi, si: (0, gi)),
                  vec(), mat(), vec(), mat(), vec(), vec(), vec()],
        out_specs=pl.BlockSpec((None, sc, width), lambda bi, gi, si: (bi, si, gi)),
        out_shape=jax.ShapeDtypeStruct((b, s, lw), BF16),
        scratch_shapes=[pltpu.VMEM((1, sc + HALO, width), F32),
                        pltpu.VMEM((sc, width), F32),
                        pltpu.VMEM((sc, width), F32),
                        pltpu.VMEM((1, width), F32)],
        compiler_params=_params("parallel", "parallel", "arbitrary"),
        name="rglru",
    )(proj3, proj3, conv_w, conv_b.reshape(1, lw), w_a, b_a.reshape(1, lw), w_x,
      b_x.reshape(1, lw), lam.reshape(1, lw), norm_w.reshape(1, lw))


def _sg_kernel(u_ref, v_ref, lnw_ref, lnb_ref, ws_ref, bst_ref, nw_ref, o_ref, *, sc,
               groups):
    t = SG_CHUNK
    tril = _iota((t, t), 0) >= _iota((t, t), 1)
    for ci in range(sc // t):
        rs = slice(ci * t, (ci + 1) * t)
        v = jax.nn.gelu(v_ref[rs, :])
        mu = jnp.mean(v, axis=-1, keepdims=True)
        vc = v - mu
        var = jnp.mean(vc * vc, axis=-1, keepdims=True)
        vn = vc * lax.rsqrt(var + EPS) * lnw_ref[...] + lnb_ref[...]
        for g in range(groups):
            cs = slice(g * LANE, (g + 1) * LANE)
            w_causal = jnp.where(tril, ws_ref[g], 0.0)
            z = _dot(w_causal, vn[:, cs]) + bst_ref[:, g:g + 1]
            y = jax.nn.gelu(u_ref[rs, cs]) * z
            y = y * lax.rsqrt(jnp.mean(y * y, axis=-1, keepdims=True) + EPS)
            o_ref[rs, cs] = (y * nw_ref[:, cs]).astype(o_ref.dtype)


def _spatial_gating(proj3, ln_w, ln_b, w_s, b_s, norm_w, col_u, col_v):
    b, s, _ = proj3.shape
    groups = w_s.shape[0]
    width = groups * LANE
    sc = _largest_tile(s, 512, SG_CHUNK)
    assert col_u % width == 0 and col_v % width == 0
    bu0, bv0 = col_u // width, col_v // width
    vec = lambda: pl.BlockSpec((1, width), lambda bi, si: (0, 0))
    return pl.pallas_call(
        functools.partial(_sg_kernel, sc=sc, groups=groups),
        grid=(b, s // sc),
        in_specs=[pl.BlockSpec((None, sc, width), lambda bi, si: (bi, si, bu0)),
                  pl.BlockSpec((None, sc, width), lambda bi, si: (bi, si, bv0)),
                  vec(), vec(),
                  pl.BlockSpec((groups, SG_CHUNK, SG_CHUNK), lambda bi, si: (0, 0, 0)),
                  pl.BlockSpec((SG_CHUNK, groups), lambda bi, si: (0, 0)),
                  vec()],
        out_specs=pl.BlockSpec((None, sc, width), lambda bi, si: (bi, si, 0)),
        out_shape=jax.ShapeDtypeStruct((b, s, width), BF16),
        compiler_params=_params("parallel", "parallel"),
        name="spatial_gating",
    )(proj3, proj3, ln_w.reshape(1, width), ln_b.reshape(1, width), w_s, b_s.T,
      norm_w.reshape(1, width))


def kernel(x, norm_mix, w_in, dn_conv_w, dn_a_log, dn_dt_bias, dn_norm_w, lru_conv_w, lru_conv_b, lru_w_a, lru_b_a, lru_w_x, lru_b_x, lru_lambda, lru_norm_w, sg_ln_w, sg_ln_b, sg_w_s, sg_b_s, sg_norm_w, w_out, norm_ffn, w_up, ffn_conv_w, ffn_conv_b, w_down, norm_final):
    bsz, seq, d = x.shape
    depth = w_in.shape[0]
    n_heads = dn_a_log.shape[1]
    dn_w = n_heads * HEAD_DIM
    lru_w = lru_lambda.shape[1]
    sg_w = sg_ln_w.shape[1]
    m = bsz * seq
    gate0 = 4 * dn_w
    tail0 = gate0 + 2 * n_heads
    main_cols = gate0 + 2 * lru_w + 2 * sg_w
    tn = IN_PROJ_TILE
    assert gate0 % tn == 0 and tail0 % HALO == 0
    w_in_t = jnp.swapaxes(w_in, 1, 2)

    def main_rows(j):
        skip = jnp.where(j < gate0 // tn, 0, (tail0 - gate0) // HALO)
        return HALO * (j * (tn // HALO) + skip)

    x2 = x.reshape(m, d)
    for l in range(depth):
        h = _rmsnorm(x2, norm_mix[l], BF16)
        proj, w_out_bf16, w_up_bf16 = _in_proj(h, w_in_t, l, main_rows, main_cols, tn,
                                               "in_proj", sides=(w_out, w_up))
        ba, = _in_proj(h, w_in_t, l, lambda j: gate0, LANE, LANE, "in_proj_gates")
        proj3 = proj.reshape(bsz, seq, main_cols)
        ba3 = ba.reshape(bsz, seq, LANE)

        y_a = _deltanet(proj3, ba3, dn_conv_w[l], dn_a_log[l], dn_dt_bias[l],
                        dn_norm_w[l], n_heads, 0)
        y_b = _rglru(proj3, lru_conv_w[l], lru_conv_b[l], lru_w_a[l], lru_b_a[l],
                     lru_w_x[l], lru_b_x[l], lru_lambda[l], lru_norm_w[l],
                     gate0, gate0 + lru_w)
        y_c = _spatial_gating(proj3, sg_ln_w[l], sg_ln_b[l], sg_w_s[l], sg_b_s[l],
                              sg_norm_w[l], gate0 + 2 * lru_w, gate0 + 2 * lru_w + sg_w)
        mix_parts = [y.reshape(m, y.shape[-1]) for y in (y_a, y_b, y_c)]
        x2 = _out_proj(mix_parts, w_out_bf16, x2, "out_proj")

        h = _rmsnorm(x2, norm_ffn[l], BF16)
        act, w_down_bf16 = _ffn_up(h, w_up_bf16, w_down, l, ffn_conv_w[l], ffn_conv_b[l],
                                   seq)
        x2 = _matmul(act, w_down_bf16, residual=x2, tk_cap=w_down_bf16.shape[0],
                     name="down_proj")
    return _rmsnorm(x2, norm_final, F32).reshape(bsz, seq, d)
```

```python
import functools

import jax
import jax.numpy as jnp
from jax import lax
from jax.experimental import pallas as pl
from jax.experimental.pallas import tpu as pltpu

F32 = jnp.float32
BF16 = jnp.bfloat16
HIGHEST = lax.Precision.HIGHEST

EPS = 1e-6
LANE = 128
HALO = 8
HEAD_DIM = LANE
DN_CHUNK = 64
DN_SEQ_TILE = 256
SG_CHUNK = 128
SCAN_CHUNK = 64
SHORT_CONV = 4
LRU_C = 8.0
VMEM_LIMIT_BYTES = 60 * 1024 * 1024


def _params(*semantics):
    return pltpu.CompilerParams(dimension_semantics=semantics,
                                vmem_limit_bytes=VMEM_LIMIT_BYTES)


def _largest_tile(n, cap, unit):
    best = None
    t = unit
    while t <= min(n, cap):
        if n % t == 0:
            best = t
        t += unit
    assert best is not None, (n, cap, unit)
    return best


def _dot(a, b):
    return jnp.dot(a.astype(BF16), b.astype(BF16), preferred_element_type=F32)


def _dot_nt(a, b):
    return lax.dot_general(a.astype(BF16), b.astype(BF16), (((1,), (1,)), ((), ())),
                           preferred_element_type=F32)


def _softplus(x):
    return jnp.maximum(x, 0.0) + jnp.log1p(jnp.exp(-jnp.abs(x)))


def _silu(x):
    return x * jax.nn.sigmoid(x)


def _iota(shape, dim):
    return lax.broadcasted_iota(jnp.int32, shape, dim)


def _rmsnorm_kernel(x_ref, w_ref, o_ref):
    x = x_ref[...]
    ms = jnp.mean(x * x, axis=-1, keepdims=True)
    o_ref[...] = (x * lax.rsqrt(ms + EPS) * w_ref[...]).astype(o_ref.dtype)


def _rmsnorm(x2, w, out_dtype):
    m, d = x2.shape
    tm = _largest_tile(m, 512, 16)
    return pl.pallas_call(
        _rmsnorm_kernel,
        grid=(m // tm,),
        in_specs=[pl.BlockSpec((tm, d), lambda i: (i, 0)),
                  pl.BlockSpec((1, d), lambda i: (0, 0))],
        out_specs=pl.BlockSpec((tm, d), lambda i: (i, 0)),
        out_shape=jax.ShapeDtypeStruct((m, d), out_dtype),
        compiler_params=_params("parallel"),
        name="rmsnorm",
    )(x2, w.reshape(1, d))


def _matmul_kernel(*refs, nk, has_res):
    if has_res:
        a_ref, w_ref, res_ref, o_ref = refs[:4]
    else:
        a_ref, w_ref, o_ref = refs[:3]
        res_ref = None
    part = jnp.dot(a_ref[...], w_ref[...], preferred_element_type=F32)

    def finish(acc):
        if res_ref is not None:
            acc = acc + res_ref[...]
        o_ref[...] = acc.astype(o_ref.dtype)

    if nk == 1:
        finish(part)
        return
    acc_ref = refs[-1]
    k = pl.program_id(2)

    @pl.when(k == 0)
    def _():
        acc_ref[...] = part

    @pl.when(k > 0)
    def _():
        acc_ref[...] += part

    @pl.when(k == nk - 1)
    def _():
        finish(acc_ref[...])


def _matmul(a, w, *, residual=None, out_dtype=F32, tm_cap=512, tn_cap=512, tk_cap=4096,
            name="matmul"):
    m, k = a.shape
    n = w.shape[1]
    tm = _largest_tile(m, tm_cap, 16)
    tn = _largest_tile(n, tn_cap, LANE)
    tk = _largest_tile(k, tk_cap, LANE)
    nk = k // tk
    in_specs = [pl.BlockSpec((tm, tk), lambda j, i, kk: (i, kk)),
                pl.BlockSpec((tk, tn), lambda j, i, kk: (kk, j))]
    args = [a, w]
    if residual is not None:
        in_specs.append(pl.BlockSpec((tm, tn), lambda j, i, kk: (i, j)))
        args.append(residual)
    scratch = [pltpu.VMEM((tm, tn), F32)] if nk > 1 else []
    return pl.pallas_call(
        functools.partial(_matmul_kernel, nk=nk, has_res=residual is not None),
        grid=(n // tn, m // tm, nk),
        in_specs=in_specs,
        out_specs=pl.BlockSpec((tm, tn), lambda j, i, kk: (i, j)),
        out_shape=jax.ShapeDtypeStruct((m, n), out_dtype),
        scratch_shapes=scratch,
        compiler_params=_params("parallel", "parallel", "arbitrary"),
        name=name,
    )(*args)


CAST_ROWS = 256
IN_PROJ_TILE = 512
IN_PROJ_ROWS = 1024


def _cast_rows(src_ref, dst_ref):
    step = min(CAST_ROWS, src_ref.shape[0])
    assert src_ref.shape[0] % step == 0

    def body(r, carry):
        rows = pl.ds(pl.multiple_of(r * step, step), step)
        dst_ref[rows, :] = src_ref[rows, :].astype(dst_ref.dtype)
        return carry

    lax.fori_loop(0, src_ref.shape[0] // step, body, 0)


def _weight_spec(w, layer, k, tn, col_block0=0):
    if layer is None:
        return pl.BlockSpec((k, tn), lambda j, i: (0, col_block0 + j))
    return pl.BlockSpec((None, k, tn), lambda j, i: (layer, 0, col_block0 + j))


def _out_proj_kernel(*refs, widths):
    a_refs = refs[:len(widths)]
    w_ref, res_ref, o_ref = refs[len(widths):]
    acc = res_ref[...]
    k0 = 0
    for a_ref, wk in zip(a_refs, widths):
        acc = acc + jnp.dot(a_ref[...], w_ref[k0:k0 + wk, :], preferred_element_type=F32)
        k0 += wk
    o_ref[...] = acc


def _out_proj(a_parts, w, residual, name):
    m = a_parts[0].shape[0]
    widths = tuple(a.shape[1] for a in a_parts)
    k, n = w.shape
    assert k == sum(widths)
    tm = _largest_tile(m, 1024, 16)
    tn = _largest_tile(n, 1024, LANE)
    in_specs = [pl.BlockSpec((tm, wk), lambda j, i: (i, 0)) for wk in widths]
    in_specs += [pl.BlockSpec((k, tn), lambda j, i: (0, j)),
                 pl.BlockSpec((tm, tn), lambda j, i: (i, j))]
    return pl.pallas_call(
        functools.partial(_out_proj_kernel, widths=widths),
        grid=(n // tn, m // tm),
        in_specs=in_specs,
        out_specs=pl.BlockSpec((tm, tn), lambda j, i: (i, j)),
        out_shape=jax.ShapeDtypeStruct((m, n), F32),
        compiler_params=_params("arbitrary", "arbitrary"),
        name=name,
    )(*a_parts, w, residual)


def _in_proj_kernel(a_ref, wt_ref, *rest, n_side):
    side_refs = rest[:n_side]
    o_ref = rest[n_side]
    side_out_refs = rest[n_side + 1:2 * n_side + 1]
    wb_ref = rest[-1]

    for src, dst in zip(side_refs, side_out_refs):
        dst[...] = src[...].astype(dst.dtype)

    @pl.when(pl.program_id(1) == 0)
    def _():
        _cast_rows(wt_ref.at[0], wb_ref)

    o_ref[...] = lax.dot_general(a_ref[...], wb_ref[...], (((1,), (1,)), ((), ())),
                                 preferred_element_type=F32)


def _in_proj(a, w_t, layer, row_start, n, tn, name, sides=()):
    m, k = a.shape
    assert n % tn == 0
    tm = _largest_tile(m, IN_PROJ_ROWS, 16)
    row_tiles = m // tm
    n_steps = (n // tn) * row_tiles
    in_specs = [pl.BlockSpec((tm, k), lambda j, i: (i, 0)),
                pl.BlockSpec((pl.Element(1), pl.Element(tn), pl.Element(k)),
                             lambda j, i: (layer, row_start(j), 0))]
    out_specs = [pl.BlockSpec((tm, tn), lambda j, i: (i, j))]
    out_shape = [jax.ShapeDtypeStruct((m, n), F32)]
    for side in sides:
        rows, cols = side.shape[1:]
        slab_rows = min(s for s in range(16, rows + 1, 16)
                        if rows % s == 0 and rows // s <= n_steps)

        def slab(j, i, n_slabs=rows // slab_rows):
            return jnp.minimum(j * row_tiles + i, n_slabs - 1)

        in_specs.append(pl.BlockSpec((None, slab_rows, cols),
                                     lambda j, i, slab=slab: (layer, slab(j, i), 0)))
        out_specs.append(pl.BlockSpec((slab_rows, cols),
                                      lambda j, i, slab=slab: (slab(j, i), 0)))
        out_shape.append(jax.ShapeDtypeStruct((rows, cols), BF16))
    return pl.pallas_call(
        functools.partial(_in_proj_kernel, n_side=len(sides)),
        grid=(n // tn, row_tiles),
        in_specs=in_specs,
        out_specs=out_specs,
        out_shape=out_shape,
        scratch_shapes=[pltpu.VMEM((tn, k), BF16)],
        compiler_params=_params("arbitrary", "arbitrary"),
        name=name,
    )(a, w_t, *sides)


FFN_ROW_TILE = 2048
FFN_ROW_BLOCK = 512


def _ffn_up_kernel(a_ref, wg_ref, wu_ref, cwg_ref, cwu_ref, bg_ref, bu_ref, o_ref,
                   wgb, wub, gbuf, ubuf, *, tm, rb, tiles_per_seq):
    i = pl.program_id(1)
    taps = cwg_ref.shape[0]

    @pl.when(i == 0)
    def _():
        _cast_rows(wg_ref, wgb)
        _cast_rows(wu_ref, wub)

    first = (i % tiles_per_seq) == 0
    for buf in (gbuf, ubuf):
        @pl.when(first)
        def _():
            buf[0:HALO, :] = jnp.zeros((HALO, buf.shape[1]), F32)

        @pl.when(jnp.logical_not(first))
        def _():
            buf[0:HALO, :] = buf[tm:tm + HALO, :]

    def conv(buf, cw_ref, r0):
        p = buf[pl.ds(r0, rb + HALO), :]
        acc = cw_ref[0:1, :] * p
        for j in range(1, taps):
            acc = cw_ref[j:j + 1, :] * p + pltpu.roll(acc, 1, 0)
        return acc[HALO:]

    for r in range(tm // rb):
        a = a_ref[r * rb:(r + 1) * rb, :]
        gbuf[HALO + r * rb:HALO + (r + 1) * rb, :] = jnp.dot(
            a, wgb[...], preferred_element_type=F32)
        ubuf[HALO + r * rb:HALO + (r + 1) * rb, :] = jnp.dot(
            a, wub[...], preferred_element_type=F32)
        g = conv(gbuf, cwg_ref, r * rb) + bg_ref[...]
        u = conv(ubuf, cwu_ref, r * rb) + bu_ref[...]
        o_ref[r * rb:(r + 1) * rb, :] = (_silu(g) * u).astype(o_ref.dtype)


def _ffn_up(h, w_up, layer, conv_w, conv_b, seq):
    m, k = h.shape
    f = w_up.shape[-1] // 2
    taps = conv_w.shape[0]
    tm = _largest_tile(seq, FFN_ROW_TILE, FFN_ROW_BLOCK)
    tf = _largest_tile(f, 256, LANE)
    nf = f // tf
    assert k % CAST_ROWS == 0
    conv_b = conv_b.reshape(1, 2 * f)
    return pl.pallas_call(
        functools.partial(_ffn_up_kernel, tm=tm, rb=FFN_ROW_BLOCK, tiles_per_seq=seq // tm),
        grid=(nf, m // tm),
        in_specs=[pl.BlockSpec((tm, k), lambda j, i: (i, 0)),
                  _weight_spec(w_up, layer, k, tf),
                  _weight_spec(w_up, layer, k, tf, nf),
                  pl.BlockSpec((taps, tf), lambda j, i: (0, j)),
                  pl.BlockSpec((taps, tf), lambda j, i: (0, nf + j)),
                  pl.BlockSpec((1, tf), lambda j, i: (0, j)),
                  pl.BlockSpec((1, tf), lambda j, i: (0, nf + j))],
        out_specs=pl.BlockSpec((tm, tf), lambda j, i: (i, j)),
        out_shape=jax.ShapeDtypeStruct((m, f), BF16),
        scratch_shapes=[pltpu.VMEM((k, tf), BF16),
                        pltpu.VMEM((k, tf), BF16),
                        pltpu.VMEM((tm + HALO, tf), F32),
                        pltpu.VMEM((tm + HALO, tf), F32)],
        compiler_params=_params("arbitrary", "arbitrary"),
        name="ffn_up_conv_gate",
    )(h, w_up, w_up, conv_w, conv_w, conv_b, conv_b)


def _load_with_halo(xpad, idx, src_ref, is_first, rows):
    width = xpad.shape[-1]

    @pl.when(is_first)
    def _():
        xpad[idx, 0:HALO, :] = jnp.zeros((HALO, width), F32)

    @pl.when(jnp.logical_not(is_first))
    def _():
        xpad[idx, 0:HALO, :] = xpad[idx, rows:rows + HALO, :]

    xpad[idx, HALO:, :] = src_ref[...]


def _short_conv(xpad, idx, cw_ref, r0, rows, cs=slice(None)):
    p = xpad[idx, pl.ds(r0, rows + HALO), cs]
    acc = cw_ref[0:1, cs] * p
    for j in range(1, cw_ref.shape[0]):
        acc = cw_ref[j:j + 1, cs] * p + pltpu.roll(acc, 1, 0)
    return acc[HALO:]


def _deltanet_kernel(q_ref, k_ref, v_ref, z_ref, ba_ref, cwq_ref, cwk_ref, cwv_ref,
                     gp_ref, nw_ref, o_ref, xpad, qkvs, gct, state, *, sc, hb, nb, n_heads):
    c = DN_CHUNK
    c2 = 2 * c
    ppb = hb // 2
    npairs = nb * ppb
    hblk = pl.program_id(0)
    is_first = pl.program_id(1) == 0

    @pl.when(is_first)
    def _():
        state[...] = jnp.zeros(state.shape, F32)

    for idx, src in enumerate((q_ref, k_ref, v_ref)):
        for b in range(nb):
            _load_with_halo(xpad, idx * nb + b, src.at[b], is_first, sc)

    for idx, cw_ref in enumerate((cwq_ref, cwk_ref, cwv_ref)):
        for b in range(nb):
            for ci in range(sc // c):
                for i in range(hb):
                    cs = slice(i * LANE, (i + 1) * LANE)
                    seg = _silu(_short_conv(xpad, idx * nb + b, cw_ref, ci * c, c, cs))
                    if idx < 2:
                        seg = seg * lax.rsqrt(
                            jnp.sum(seg * seg, axis=-1, keepdims=True) + EPS)
                    qkvs[idx, b * ppb + i // 2, ci, (i % 2) * c:(i % 2 + 1) * c, :] = seg

    row = _iota((c2, c2), 0)
    col = _iota((c2, c2), 1)
    same_head = (row >= c) == (col >= c)
    causal = (row >= col) & same_head
    strict = (row > col) & same_head
    second = (row >= c).astype(jnp.int32)
    left_cols = col < c
    eye = (row == col).astype(F32)
    tril = (_iota((c, c), 0) >= _iota((c, c), 1)).astype(F32)
    lane_row = _iota((1, c2), 1)
    rows1 = _iota((c2, 1), 0)
    rows2 = _iota((2 * c2, 1), 0)
    first_head_rows = (rows2 % c2) < c
    scale = HEAD_DIM ** -0.5
    pairs = range(npairs)

    def chunk_body(ci, carry):
        r0 = ci * c
        gc_stack, beta_stack = [], []
        for b in range(nb):
            ba = ba_ref[b, pl.ds(r0, c), :]
            beta_all = jax.nn.sigmoid(ba)
            g_all = -jnp.exp(gp_ref[0:1, :]) * _softplus(ba + gp_ref[1:2, :])
            gc_all = jnp.dot(tril, g_all, precision=HIGHEST,
                             preferred_element_type=F32)
            gc_stack.append(jnp.concatenate([gc_all, gc_all], axis=0))
            beta_stack.append(jnp.concatenate([beta_all, beta_all], axis=0))
            gct[b] = gc_stack[b].T
        bof = [u // ppb for u in pairs]
        h0 = [hblk * hb + 2 * (u % ppb) for u in pairs]
        bcol = [jnp.sum(jnp.where(col == h0[u] + second, beta_stack[bof[u]], 0.0), axis=-1,
                        keepdims=True) for u in pairs]
        gcol = [jnp.sum(jnp.where(col == h0[u] + n_heads + second, gc_stack[bof[u]], 0.0),
                        axis=-1, keepdims=True) for u in pairs]
        grow = [jnp.where(lane_row < c, gct[bof[u], pl.ds(h0[u] + n_heads, 1), :],
                          gct[bof[u], pl.ds(h0[u] + n_heads + 1, 1), :])
                for u in pairs]
        glast = [jnp.where(rows1 < c, g[c - 1:c, :], g[c2 - 1:c2, :]) for g in gcol]
        decay = [jnp.where(causal, jnp.exp(jnp.where(causal, gcol[p] - grow[p], 0.0)), 0.0)
                 for p in pairs]
        egc = [jnp.exp(g) for g in gcol]

        q = [qkvs[0, p, ci] * scale for p in pairs]
        k = [qkvs[1, p, ci] for p in pairs]
        v = [qkvs[2, p, ci] for p in pairs]
        kb = [k[p] * bcol[p] for p in pairs]
        kq = [_dot_nt(jnp.concatenate([kb[p], q[p]], axis=0), k[p]) for p in pairs]
        a_low = [jnp.where(strict, kq[p][:c2] * decay[p], 0.0) for p in pairs]
        attn = [kq[p][c2:] * decay[p] for p in pairs]

        t_inv = [eye - a for a in a_low]
        pw = a_low
        n = 2
        while n < c:
            pw = [_dot(x, x) for x in pw]
            t_inv = [t_inv[p] + _dot(t_inv[p], pw[p]) for p in pairs]
            n *= 2

        sol = [_dot(t_inv[p], jnp.concatenate([v[p] * bcol[p], kb[p] * egc[p]], axis=1))
               for p in pairs]
        k_dec = [k[p] * jnp.exp(glast[p] - gcol[p]) for p in pairs]
        s_prev = [state[p] for p in pairs]
        ws_qs = []
        for p in pairs:
            x = jnp.concatenate([sol[p][:, LANE:], q[p] * egc[p]], axis=0)
            x_wide = jnp.concatenate([jnp.where(first_head_rows, x, 0.0),
                                      jnp.where(first_head_rows, 0.0, x)], axis=1)
            ws_qs.append(_dot(x_wide, s_prev[p]))
        v_new = [sol[p][:, :LANE] - ws_qs[p][:c2] for p in pairs]
        o = [ws_qs[p][c2:] + _dot(attn[p], v_new[p]) for p in pairs]
        for p in pairs:
            kdt = k_dec[p].T
            kd_stack = jnp.concatenate([jnp.where(left_cols, kdt, 0.0),
                                        jnp.where(left_cols, 0.0, kdt)], axis=0)
            s_decay = jnp.where(rows2 < c2, jnp.exp(gcol[p][c - 1:c, :]),
                                jnp.exp(gcol[p][c2 - 1:c2, :]))
            state[p] = s_prev[p] * s_decay + _dot(kd_stack, v_new[p])
        for p in pairs:
            b, hp = bof[p], p % ppb
            cs0 = slice((2 * hp) * LANE, (2 * hp + 1) * LANE)
            cs1 = slice((2 * hp + 1) * LANE, (2 * hp + 2) * LANE)
            zg = _silu(jnp.concatenate([z_ref[b, pl.ds(r0, c), cs0],
                                        z_ref[b, pl.ds(r0, c), cs1]], axis=0))
            op = o[p] * lax.rsqrt(jnp.mean(o[p] * o[p], axis=-1, keepdims=True) + EPS)
            op = (op * nw_ref[...] * zg).astype(o_ref.dtype)
            o_ref[b, pl.ds(r0, c), cs0] = op[:c]
            o_ref[b, pl.ds(r0, c), cs1] = op[c:]
        return carry

    for ci in range(sc // c):
        chunk_body(ci, 0)


def _deltanet(proj3, ba3, conv_w, a_log, dt_bias, norm_w, n_heads, col0):
    b, s, _ = proj3.shape
    assert n_heads % 2 == 0 and 2 * n_heads <= LANE
    hb = _largest_tile(n_heads, 12, 2)
    width = hb * LANE
    nhb = n_heads // hb
    sc = _largest_tile(s, DN_SEQ_TILE, DN_CHUNK)
    base = col0 // width
    assert col0 % width == 0
    npairs = b * hb // 2
    gp = jnp.zeros((2, LANE), F32)
    gp = gp.at[0, n_heads:2 * n_heads].set(a_log).at[1, n_heads:2 * n_heads].set(dt_bias)

    def col_spec(group):
        return pl.BlockSpec((b, sc, width),
                            lambda hi, si, g=group: (0, si, base + g * nhb + hi))

    def cw_spec(group):
        return pl.BlockSpec((SHORT_CONV, width), lambda hi, si, g=group: (0, g * nhb + hi))

    return pl.pallas_call(
        functools.partial(_deltanet_kernel, sc=sc, hb=hb, nb=b, n_heads=n_heads),
        grid=(nhb, s // sc),
        in_specs=[col_spec(0), col_spec(1), col_spec(2), col_spec(3),
                  pl.BlockSpec((b, sc, LANE), lambda hi, si: (0, si, 0)),
                  cw_spec(0), cw_spec(1), cw_spec(2),
                  pl.BlockSpec((2, LANE), lambda hi, si: (0, 0)),
                  pl.BlockSpec((1, LANE), lambda hi, si: (0, 0))],
        out_specs=pl.BlockSpec((b, sc, width), lambda hi, si: (0, si, hi)),
        out_shape=jax.ShapeDtypeStruct((b, s, n_heads * LANE), BF16),
        scratch_shapes=[pltpu.VMEM((3 * b, sc + HALO, width), F32),
                        pltpu.VMEM((3, npairs, sc // DN_CHUNK, 2 * DN_CHUNK, LANE), F32),
                        pltpu.VMEM((b, LANE, LANE), F32),
                        pltpu.VMEM((npairs, 2 * HEAD_DIM, HEAD_DIM), F32)],
        compiler_params=_params("parallel", "arbitrary"),
        name="gated_deltanet",
    )(proj3, proj3, proj3, proj3, ba3, conv_w, conv_w, conv_w, gp, norm_w.reshape(1, LANE))


def _rglru_kernel(x_ref, g_ref, cw_ref, cb_ref, wa_ref, ba_ref, wx_ref, bx_ref, lam_ref,
                  nw_ref, o_ref, xpad, abuf, bbuf, hcarry, *, sc, gb):
    c = SCAN_CHUNK
    is_first = pl.program_id(2) == 0

    @pl.when(is_first)
    def _():
        hcarry[...] = jnp.zeros(hcarry.shape, F32)

    _load_with_halo(xpad, 0, x_ref, is_first, sc)

    rows = 2 * c
    for ci in range(sc // rows):
        xc = _short_conv(xpad, 0, cw_ref, ci * rows, rows) + cb_ref[...]
        for g in range(gb):
            cs = slice(g * LANE, (g + 1) * LANE)
            xg = xc[:, cs]
            r = jax.nn.sigmoid(_dot(xg, wa_ref[g]) + ba_ref[:, cs])
            ig = jax.nn.sigmoid(_dot(xg, wx_ref[g]) + bx_ref[:, cs])
            a = jnp.exp(-LRU_C * r * _softplus(-lam_ref[:, cs]))
            abuf[ci * rows:(ci + 1) * rows, cs] = a
            bbuf[ci * rows:(ci + 1) * rows, cs] = (
                jnp.sqrt((1.0 - a) * (1.0 + a)) * (ig * xg))

    row = _iota((HALO, LANE), 0)

    def chunk_body(ci, carry):
        r0 = pl.multiple_of(ci * c, c)
        for g in range(gb):
            cs = slice(g * LANE, (g + 1) * LANE)
            a = abuf[pl.ds(r0, c), cs]
            bb = bbuf[pl.ds(r0, c), cs]
            last = jnp.broadcast_to(hcarry[:, cs], (HALO, LANE))
            tiles = []
            for v in range(c // HALO):
                av = a[v * HALO:(v + 1) * HALO]
                bv = bb[v * HALO:(v + 1) * HALO]
                d = 1
                while d < HALO:
                    keep = row >= d
                    a_sh = jnp.where(keep, pltpu.roll(av, d, 0), 1.0)
                    b_sh = jnp.where(keep, pltpu.roll(bv, d, 0), 0.0)
                    bv = av * b_sh + bv
                    av = av * a_sh
                    d *= 2
                hv = bv + av * last
                last = jnp.broadcast_to(hv[HALO - 1:HALO, :], (HALO, LANE))
                tiles.append(hv)
            h = jnp.concatenate(tiles, axis=0)
            hcarry[:, cs] = last[0:1, :]
            y = h * jax.nn.gelu(g_ref[pl.ds(r0, c), cs])
            y = y * lax.rsqrt(jnp.mean(y * y, axis=-1, keepdims=True) + EPS)
            o_ref[pl.ds(r0, c), cs] = (y * nw_ref[:, cs]).astype(o_ref.dtype)
        return carry

    lax.fori_loop(0, sc // c, chunk_body, 0)


def _rglru(proj3, conv_w, conv_b, w_a, b_a, w_x, b_x, lam, norm_w, col_x, col_g):
    b, s, _ = proj3.shape
    nblk = w_a.shape[0]
    gb = 4 if nblk % 4 == 0 else (3 if nblk % 3 == 0 else 1)
    width = gb * LANE
    lw = nblk * LANE
    sc = _largest_tile(s, 512, 2 * SCAN_CHUNK)
    assert col_x % width == 0 and col_g % width == 0
    bx0, bg0 = col_x // width, col_g // width
    vec = lambda: pl.BlockSpec((1, width), lambda bi, gi, si: (0, gi))
    mat = lambda: pl.BlockSpec((gb, LANE, LANE), lambda bi, gi, si: (gi, 0, 0))
    return pl.pallas_call(
        functools.partial(_rglru_kernel, sc=sc, gb=gb),
        grid=(b, nblk // gb, s // sc),
        in_specs=[pl.BlockSpec((None, sc, width), lambda bi, gi, si: (bi, si, bx0 + gi)),
                  pl.BlockSpec((None, sc, width), lambda bi, gi, si: (bi, si, bg0 + gi)),
                  pl.BlockSpec((SHORT_CONV, width), lambda bi, gi, si: (0, gi)),
                  vec(), mat(), vec(), mat(), vec(), vec(), vec()],
        out_specs=pl.BlockSpec((None, sc, width), lambda bi, gi, si: (bi, si, gi)),
        out_shape=jax.ShapeDtypeStruct((b, s, lw), BF16),
        scratch_shapes=[pltpu.VMEM((1, sc + HALO, width), F32),
                        pltpu.VMEM((sc, width), F32),
                        pltpu.VMEM((sc, width), F32),
                        pltpu.VMEM((1, width), F32)],
        compiler_params=_params("parallel", "parallel", "arbitrary"),
        name="rglru",
    )(proj3, proj3, conv_w, conv_b.reshape(1, lw), w_a, b_a.reshape(1, lw), w_x,
      b_x.reshape(1, lw), lam.reshape(1, lw), norm_w.reshape(1, lw))


def _sg_kernel(u_ref, v_ref, lnw_ref, lnb_ref, ws_ref, bst_ref, nw_ref, o_ref, *, sc,
               groups):
    t = SG_CHUNK
    tril = _iota((t, t), 0) >= _iota((t, t), 1)
    for ci in range(sc // t):
        rs = slice(ci * t, (ci + 1) * t)
        v = jax.nn.gelu(v_ref[rs, :])
        mu = jnp.mean(v, axis=-1, keepdims=True)
        vc = v - mu
        var = jnp.mean(vc * vc, axis=-1, keepdims=True)
        vn = vc * lax.rsqrt(var + EPS) * lnw_ref[...] + lnb_ref[...]
        for g in range(groups):
            cs = slice(g * LANE, (g + 1) * LANE)
            w_causal = jnp.where(tril, ws_ref[g], 0.0)
            z = _dot(w_causal, vn[:, cs]) + bst_ref[:, g:g + 1]
            y = jax.nn.gelu(u_ref[rs, cs]) * z
            y = y * lax.rsqrt(jnp.mean(y * y, axis=-1, keepdims=True) + EPS)
            o_ref[rs, cs] = (y * nw_ref[:, cs]).astype(o_ref.dtype)


def _spatial_gating(proj3, ln_w, ln_b, w_s, b_s, norm_w, col_u, col_v):
    b, s, _ = proj3.shape
    groups = w_s.shape[0]
    width = groups * LANE
    sc = _largest_tile(s, 512, SG_CHUNK)
    assert col_u % width == 0 and col_v % width == 0
    bu0, bv0 = col_u // width, col_v // width
    vec = lambda: pl.BlockSpec((1, width), lambda bi, si: (0, 0))
    return pl.pallas_call(
        functools.partial(_sg_kernel, sc=sc, groups=groups),
        grid=(b, s // sc),
        in_specs=[pl.BlockSpec((None, sc, width), lambda bi, si: (bi, si, bu0)),
                  pl.BlockSpec((None, sc, width), lambda bi, si: (bi, si, bv0)),
                  vec(), vec(),
                  pl.BlockSpec((groups, SG_CHUNK, SG_CHUNK), lambda bi, si: (0, 0, 0)),
                  pl.BlockSpec((SG_CHUNK, groups), lambda bi, si: (0, 0)),
                  vec()],
        out_specs=pl.BlockSpec((None, sc, width), lambda bi, si: (bi, si, 0)),
        out_shape=jax.ShapeDtypeStruct((b, s, width), BF16),
        compiler_params=_params("parallel", "parallel"),
        name="spatial_gating",
    )(proj3, proj3, ln_w.reshape(1, width), ln_b.reshape(1, width), w_s, b_s.T,
      norm_w.reshape(1, width))


def kernel(x, norm_mix, w_in, dn_conv_w, dn_a_log, dn_dt_bias, dn_norm_w, lru_conv_w, lru_conv_b, lru_w_a, lru_b_a, lru_w_x, lru_b_x, lru_lambda, lru_norm_w, sg_ln_w, sg_ln_b, sg_w_s, sg_b_s, sg_norm_w, w_out, norm_ffn, w_up, ffn_conv_w, ffn_conv_b, w_down, norm_final):
    bsz, seq, d = x.shape
    depth = w_in.shape[0]
    n_heads = dn_a_log.shape[1]
    dn_w = n_heads * HEAD_DIM
    lru_w = lru_lambda.shape[1]
    sg_w = sg_ln_w.shape[1]
    m = bsz * seq
    gate0 = 4 * dn_w
    tail0 = gate0 + 2 * n_heads
    main_cols = gate0 + 2 * lru_w + 2 * sg_w
    tn = IN_PROJ_TILE
    assert gate0 % tn == 0 and tail0 % HALO == 0
    w_in_t = jnp.swapaxes(w_in, 1, 2)

    def main_rows(j):
        skip = jnp.where(j < gate0 // tn, 0, (tail0 - gate0) // HALO)
        return HALO * (j * (tn // HALO) + skip)

    x2 = x.reshape(m, d)
    for l in range(depth):
        h = _rmsnorm(x2, norm_mix[l], BF16)
        proj, w_out_bf16, w_down_bf16 = _in_proj(h, w_in_t, l, main_rows, main_cols, tn,
                                                 "in_proj", sides=(w_out, w_down))
        ba, = _in_proj(h, w_in_t, l, lambda j: gate0, LANE, LANE, "in_proj_gates")
        proj3 = proj.reshape(bsz, seq, main_cols)
        ba3 = ba.reshape(bsz, seq, LANE)

        y_a = _deltanet(proj3, ba3, dn_conv_w[l], dn_a_log[l], dn_dt_bias[l],
                        dn_norm_w[l], n_heads, 0)
        y_b = _rglru(proj3, lru_conv_w[l], lru_conv_b[l], lru_w_a[l], lru_b_a[l],
                     lru_w_x[l], lru_b_x[l], lru_lambda[l], lru_norm_w[l],
                     gate0, gate0 + lru_w)
        y_c = _spatial_gating(proj3, sg_ln_w[l], sg_ln_b[l], sg_w_s[l], sg_b_s[l],
                              sg_norm_w[l], gate0 + 2 * lru_w, gate0 + 2 * lru_w + sg_w)
        mix_parts = [y.reshape(m, y.shape[-1]) for y in (y_a, y_b, y_c)]
        x2 = _out_proj(mix_parts, w_out_bf16, x2, "out_proj")

        h = _rmsnorm(x2, norm_ffn[l], BF16)
        act = _ffn_up(h, w_up, l, ffn_conv_w[l], ffn_conv_b[l], seq)
        x2 = _matmul(act, w_down_bf16, residual=x2, tk_cap=w_down_bf16.shape[0],
                     name="down_proj")
    return _rmsnorm(x2, norm_final, F32).reshape(bsz, seq, d)
```

```python
import functools

import jax
import jax.numpy as jnp
from jax import lax
from jax.experimental import pallas as pl
from jax.experimental.pallas import tpu as pltpu

F32 = jnp.float32
BF16 = jnp.bfloat16
HIGHEST = lax.Precision.HIGHEST

EPS = 1e-6
LANE = 128
HALO = 8
HEAD_DIM = LANE
DN_CHUNK = 64
DN_SEQ_TILE = 256
SG_CHUNK = 128
SCAN_CHUNK = 64
SHORT_CONV = 4
LRU_C = 8.0
VMEM_LIMIT_BYTES = 62 * 1024 * 1024


def _params(*semantics):
    return pltpu.CompilerParams(dimension_semantics=semantics,
                                vmem_limit_bytes=VMEM_LIMIT_BYTES)


def _largest_tile(n, cap, unit):
    best = None
    t = unit
    while t <= min(n, cap):
        if n % t == 0:
            best = t
        t += unit
    assert best is not None, (n, cap, unit)
    return best


def _dot(a, b):
    return jnp.dot(a.astype(BF16), b.astype(BF16), preferred_element_type=F32)


def _dot_nt(a, b):
    return lax.dot_general(a.astype(BF16), b.astype(BF16), (((1,), (1,)), ((), ())),
                           preferred_element_type=F32)


def _softplus(x):
    return jnp.maximum(x, 0.0) + jnp.log1p(jnp.exp(-jnp.abs(x)))


def _silu(x):
    return x * jax.nn.sigmoid(x)


def _iota(shape, dim):
    return lax.broadcasted_iota(jnp.int32, shape, dim)


def _rmsnorm_kernel(x_ref, w_ref, o_ref):
    x = x_ref[...]
    ms = jnp.mean(x * x, axis=-1, keepdims=True)
    o_ref[...] = (x * lax.rsqrt(ms + EPS) * w_ref[...]).astype(o_ref.dtype)


def _rmsnorm(x2, w, out_dtype):
    m, d = x2.shape
    tm = _largest_tile(m, 512, 16)
    return pl.pallas_call(
        _rmsnorm_kernel,
        grid=(m // tm,),
        in_specs=[pl.BlockSpec((tm, d), lambda i: (i, 0)),
                  pl.BlockSpec((1, d), lambda i: (0, 0))],
        out_specs=pl.BlockSpec((tm, d), lambda i: (i, 0)),
        out_shape=jax.ShapeDtypeStruct((m, d), out_dtype),
        compiler_params=_params("parallel"),
        name="rmsnorm",
    )(x2, w.reshape(1, d))


def _matmul_kernel(*refs, nk, has_res):
    if has_res:
        a_ref, w_ref, res_ref, o_ref = refs[:4]
    else:
        a_ref, w_ref, o_ref = refs[:3]
        res_ref = None
    part = jnp.dot(a_ref[...], w_ref[...], preferred_element_type=F32)

    def finish(acc):
        if res_ref is not None:
            acc = acc + res_ref[...]
        o_ref[...] = acc.astype(o_ref.dtype)

    if nk == 1:
        finish(part)
        return
    acc_ref = refs[-1]
    k = pl.program_id(2)

    @pl.when(k == 0)
    def _():
        acc_ref[...] = part

    @pl.when(k > 0)
    def _():
        acc_ref[...] += part

    @pl.when(k == nk - 1)
    def _():
        finish(acc_ref[...])


def _matmul(a, w, *, residual=None, out_dtype=F32, tm_cap=512, tn_cap=512, tk_cap=4096,
            name="matmul"):
    m, k = a.shape
    n = w.shape[1]
    tm = _largest_tile(m, tm_cap, 16)
    tn = _largest_tile(n, tn_cap, LANE)
    tk = _largest_tile(k, tk_cap, LANE)
    nk = k // tk
    in_specs = [pl.BlockSpec((tm, tk), lambda j, i, kk: (i, kk)),
                pl.BlockSpec((tk, tn), lambda j, i, kk: (kk, j))]
    args = [a, w]
    if residual is not None:
        in_specs.append(pl.BlockSpec((tm, tn), lambda j, i, kk: (i, j)))
        args.append(residual)
    scratch = [pltpu.VMEM((tm, tn), F32)] if nk > 1 else []
    return pl.pallas_call(
        functools.partial(_matmul_kernel, nk=nk, has_res=residual is not None),
        grid=(n // tn, m // tm, nk),
        in_specs=in_specs,
        out_specs=pl.BlockSpec((tm, tn), lambda j, i, kk: (i, j)),
        out_shape=jax.ShapeDtypeStruct((m, n), out_dtype),
        scratch_shapes=scratch,
        compiler_params=_params("parallel", "parallel", "arbitrary"),
        name=name,
    )(*args)


CAST_ROWS = 256
IN_PROJ_TILE = 1024
IN_PROJ_ROWS = 512


def _cast_rows(src_ref, dst_ref):
    step = min(CAST_ROWS, src_ref.shape[0])
    assert src_ref.shape[0] % step == 0

    def body(r, carry):
        rows = pl.ds(pl.multiple_of(r * step, step), step)
        dst_ref[rows, :] = src_ref[rows, :].astype(dst_ref.dtype)
        return carry

    lax.fori_loop(0, src_ref.shape[0] // step, body, 0)


def _weight_spec(w, layer, k, tn, col_block0=0):
    if layer is None:
        return pl.BlockSpec((k, tn), lambda j, i: (0, col_block0 + j))
    return pl.BlockSpec((None, k, tn), lambda j, i: (layer, 0, col_block0 + j))


def _out_proj_kernel(*refs, widths):
    a_refs = refs[:len(widths)]
    w_ref, res_ref, o_ref = refs[len(widths):]
    acc = res_ref[...]
    k0 = 0
    for a_ref, wk in zip(a_refs, widths):
        acc = acc + jnp.dot(a_ref[...], w_ref[k0:k0 + wk, :], preferred_element_type=F32)
        k0 += wk
    o_ref[...] = acc


def _out_proj(a_parts, w, residual, name):
    m = a_parts[0].shape[0]
    widths = tuple(a.shape[1] for a in a_parts)
    k, n = w.shape
    assert k == sum(widths)
    tm = _largest_tile(m, 1024, 16)
    tn = _largest_tile(n, 1024, LANE)
    in_specs = [pl.BlockSpec((tm, wk), lambda j, i: (i, 0)) for wk in widths]
    in_specs += [pl.BlockSpec((k, tn), lambda j, i: (0, j)),
                 pl.BlockSpec((tm, tn), lambda j, i: (i, j))]
    return pl.pallas_call(
        functools.partial(_out_proj_kernel, widths=widths),
        grid=(n // tn, m // tm),
        in_specs=in_specs,
        out_specs=pl.BlockSpec((tm, tn), lambda j, i: (i, j)),
        out_shape=jax.ShapeDtypeStruct((m, n), F32),
        compiler_params=_params("arbitrary", "arbitrary"),
        name=name,
    )(*a_parts, w, residual)


def _in_proj_kernel(a_ref, wt_ref, *rest, n_side):
    side_refs = rest[:n_side]
    o_ref = rest[n_side]
    side_out_refs = rest[n_side + 1:2 * n_side + 1]
    wb_ref = rest[-1]

    for src, dst in zip(side_refs, side_out_refs):
        dst[...] = src[...].astype(dst.dtype)

    @pl.when(pl.program_id(1) == 0)
    def _():
        _cast_rows(wt_ref.at[0], wb_ref)

    o_ref[...] = lax.dot_general(a_ref[...], wb_ref[...], (((1,), (1,)), ((), ())),
                                 preferred_element_type=F32)


def _in_proj(a, w_t, layer, row_start, n, tn, name, sides=()):
    m, k = a.shape
    assert n % tn == 0
    tm = _largest_tile(m, IN_PROJ_ROWS, 16)
    row_tiles = m // tm
    n_steps = (n // tn) * row_tiles
    in_specs = [pl.BlockSpec((tm, k), lambda j, i: (i, 0)),
                pl.BlockSpec((pl.Element(1), pl.Element(tn), pl.Element(k)),
                             lambda j, i: (layer, row_start(j), 0))]
    out_specs = [pl.BlockSpec((tm, tn), lambda j, i: (i, j))]
    out_shape = [jax.ShapeDtypeStruct((m, n), F32)]
    for side in sides:
        rows, cols = side.shape[1:]
        slab_rows = min(s for s in range(16, rows + 1, 16)
                        if rows % s == 0 and rows // s <= n_steps)

        def slab(j, i, n_slabs=rows // slab_rows):
            return jnp.minimum(j * row_tiles + i, n_slabs - 1)

        in_specs.append(pl.BlockSpec((None, slab_rows, cols),
                                     lambda j, i, slab=slab: (layer, slab(j, i), 0)))
        out_specs.append(pl.BlockSpec((slab_rows, cols),
                                      lambda j, i, slab=slab: (slab(j, i), 0)))
        out_shape.append(jax.ShapeDtypeStruct((rows, cols), BF16))
    return pl.pallas_call(
        functools.partial(_in_proj_kernel, n_side=len(sides)),
        grid=(n // tn, row_tiles),
        in_specs=in_specs,
        out_specs=out_specs,
        out_shape=out_shape,
        scratch_shapes=[pltpu.VMEM((tn, k), BF16)],
        compiler_params=_params("arbitrary", "arbitrary"),
        name=name,
    )(a, w_t, *sides)


FFN_ROW_TILE = 2048
FFN_ROW_BLOCK = 512


def _ffn_up_kernel(a_ref, wg_ref, wu_ref, cwg_ref, cwu_ref, bg_ref, bu_ref, o_ref,
                   wgb, wub, gbuf, ubuf, *, tm, rb, tiles_per_seq):
    i = pl.program_id(1)
    taps = cwg_ref.shape[0]

    @pl.when(i == 0)
    def _():
        _cast_rows(wg_ref, wgb)
        _cast_rows(wu_ref, wub)

    first = (i % tiles_per_seq) == 0
    for buf in (gbuf, ubuf):
        @pl.when(first)
        def _():
            buf[0:HALO, :] = jnp.zeros((HALO, buf.shape[1]), F32)

        @pl.when(jnp.logical_not(first))
        def _():
            buf[0:HALO, :] = buf[tm:tm + HALO, :]

    def conv(buf, cw_ref, r0):
        p = buf[pl.ds(r0, rb + HALO), :]
        acc = cw_ref[0:1, :] * p
        for j in range(1, taps):
            acc = cw_ref[j:j + 1, :] * p + pltpu.roll(acc, 1, 0)
        return acc[HALO:]

    for r in range(tm // rb):
        a = a_ref[r * rb:(r + 1) * rb, :]
        gbuf[HALO + r * rb:HALO + (r + 1) * rb, :] = jnp.dot(
            a, wgb[...], preferred_element_type=F32)
        ubuf[HALO + r * rb:HALO + (r + 1) * rb, :] = jnp.dot(
            a, wub[...], preferred_element_type=F32)
        g = conv(gbuf, cwg_ref, r * rb) + bg_ref[...]
        u = conv(ubuf, cwu_ref, r * rb) + bu_ref[...]
        o_ref[r * rb:(r + 1) * rb, :] = (_silu(g) * u).astype(o_ref.dtype)


def _ffn_up(h, w_up, layer, conv_w, conv_b, seq):
    m, k = h.shape
    f = w_up.shape[-1] // 2
    taps = conv_w.shape[0]
    tm = _largest_tile(seq, FFN_ROW_TILE, FFN_ROW_BLOCK)
    tf = _largest_tile(f, 256, LANE)
    nf = f // tf
    assert k % CAST_ROWS == 0
    conv_b = conv_b.reshape(1, 2 * f)
    return pl.pallas_call(
        functools.partial(_ffn_up_kernel, tm=tm, rb=FFN_ROW_BLOCK, tiles_per_seq=seq // tm),
        grid=(nf, m // tm),
        in_specs=[pl.BlockSpec((tm, k), lambda j, i: (i, 0)),
                  _weight_spec(w_up, layer, k, tf),
                  _weight_spec(w_up, layer, k, tf, nf),
                  pl.BlockSpec((taps, tf), lambda j, i: (0, j)),
                  pl.BlockSpec((taps, tf), lambda j, i: (0, nf + j)),
                  pl.BlockSpec((1, tf), lambda j, i: (0, j)),
                  pl.BlockSpec((1, tf), lambda j, i: (0, nf + j))],
        out_specs=pl.BlockSpec((tm, tf), lambda j, i: (i, j)),
        out_shape=jax.ShapeDtypeStruct((m, f), BF16),
        scratch_shapes=[pltpu.VMEM((k, tf), BF16),
                        pltpu.VMEM((k, tf), BF16),
                        pltpu.VMEM((tm + HALO, tf), F32),
                        pltpu.VMEM((tm + HALO, tf), F32)],
        compiler_params=_params("arbitrary", "arbitrary"),
        name="ffn_up_conv_gate",
    )(h, w_up, w_up, conv_w, conv_w, conv_b, conv_b)


def _load_with_halo(xpad, idx, src_ref, is_first, rows):
    width = xpad.shape[-1]

    @pl.when(is_first)
    def _():
        xpad[idx, 0:HALO, :] = jnp.zeros((HALO, width), F32)

    @pl.when(jnp.logical_not(is_first))
    def _():
        xpad[idx, 0:HALO, :] = xpad[idx, rows:rows + HALO, :]

    xpad[idx, HALO:, :] = src_ref[...]


def _short_conv(xpad, idx, cw_ref, r0, rows, cs=slice(None)):
    p = xpad[idx, pl.ds(r0, rows + HALO), cs]
    acc = cw_ref[0:1, cs] * p
    for j in range(1, cw_ref.shape[0]):
        acc = cw_ref[j:j + 1, cs] * p + pltpu.roll(acc, 1, 0)
    return acc[HALO:]


def _deltanet_kernel(q_ref, k_ref, v_ref, z_ref, ba_ref, cwq_ref, cwk_ref, cwv_ref,
                     gp_ref, nw_ref, o_ref, xpad, qkvs, gct, state, *, sc, hb, nb, n_heads):
    c = DN_CHUNK
    c2 = 2 * c
    ppb = hb // 2
    npairs = nb * ppb
    hblk = pl.program_id(0)
    is_first = pl.program_id(1) == 0

    @pl.when(is_first)
    def _():
        state[...] = jnp.zeros(state.shape, F32)

    for idx, src in enumerate((q_ref, k_ref, v_ref)):
        for b in range(nb):
            _load_with_halo(xpad, idx * nb + b, src.at[b], is_first, sc)

    for idx, cw_ref in enumerate((cwq_ref, cwk_ref, cwv_ref)):
        for b in range(nb):
            for ci in range(sc // c):
                for i in range(hb):
                    cs = slice(i * LANE, (i + 1) * LANE)
                    seg = _silu(_short_conv(xpad, idx * nb + b, cw_ref, ci * c, c, cs))
                    if idx < 2:
                        seg = seg * lax.rsqrt(
                            jnp.sum(seg * seg, axis=-1, keepdims=True) + EPS)
                    qkvs[idx, b * ppb + i // 2, ci, (i % 2) * c:(i % 2 + 1) * c, :] = seg

    row = _iota((c2, c2), 0)
    col = _iota((c2, c2), 1)
    same_head = (row >= c) == (col >= c)
    causal = (row >= col) & same_head
    strict = (row > col) & same_head
    second = (row >= c).astype(jnp.int32)
    left_cols = col < c
    eye = (row == col).astype(F32)
    tril = (_iota((c, c), 0) >= _iota((c, c), 1)).astype(F32)
    lane_row = _iota((1, c2), 1)
    rows1 = _iota((c2, 1), 0)
    rows2 = _iota((2 * c2, 1), 0)
    first_head_rows = (rows2 % c2) < c
    scale = HEAD_DIM ** -0.5
    pairs = range(npairs)

    def chunk_body(ci, carry):
        r0 = ci * c
        gc_stack, beta_stack = [], []
        for b in range(nb):
            ba = ba_ref[b, pl.ds(r0, c), :]
            beta_all = jax.nn.sigmoid(ba)
            g_all = -jnp.exp(gp_ref[0:1, :]) * _softplus(ba + gp_ref[1:2, :])
            gc_all = jnp.dot(tril, g_all, precision=HIGHEST,
                             preferred_element_type=F32)
            gc_stack.append(jnp.concatenate([gc_all, gc_all], axis=0))
            beta_stack.append(jnp.concatenate([beta_all, beta_all], axis=0))
            gct[b] = gc_stack[b].T
        bof = [u // ppb for u in pairs]
        h0 = [hblk * hb + 2 * (u % ppb) for u in pairs]
        bcol = [jnp.sum(jnp.where(col == h0[u] + second, beta_stack[bof[u]], 0.0), axis=-1,
                        keepdims=True) for u in pairs]
        gcol = [jnp.sum(jnp.where(col == h0[u] + n_heads + second, gc_stack[bof[u]], 0.0),
                        axis=-1, keepdims=True) for u in pairs]
        grow = [jnp.where(lane_row < c, gct[bof[u], pl.ds(h0[u] + n_heads, 1), :],
                          gct[bof[u], pl.ds(h0[u] + n_heads + 1, 1), :])
                for u in pairs]
        glast = [jnp.where(rows1 < c, g[c - 1:c, :], g[c2 - 1:c2, :]) for g in gcol]
        decay = [jnp.where(causal, jnp.exp(jnp.where(causal, gcol[p] - grow[p], 0.0)), 0.0)
                 for p in pairs]
        egc = [jnp.exp(g) for g in gcol]

        q = [qkvs[0, p, ci] * scale for p in pairs]
        k = [qkvs[1, p, ci] for p in pairs]
        v = [qkvs[2, p, ci] for p in pairs]
        kb = [k[p] * bcol[p] for p in pairs]
        kq = [_dot_nt(jnp.concatenate([kb[p], q[p]], axis=0), k[p]) for p in pairs]
        a_low = [jnp.where(strict, kq[p][:c2] * decay[p], 0.0) for p in pairs]
        attn = [kq[p][c2:] * decay[p] for p in pairs]

        t_inv = [eye - a for a in a_low]
        pw = a_low
        n = 2
        while n < c:
            pw = [_dot(x, x) for x in pw]
            t_inv = [t_inv[p] + _dot(t_inv[p], pw[p]) for p in pairs]
            n *= 2

        sol = [_dot(t_inv[p], jnp.concatenate([v[p] * bcol[p], kb[p] * egc[p]], axis=1))
               for p in pairs]
        k_dec = [k[p] * jnp.exp(glast[p] - gcol[p]) for p in pairs]
        s_prev = [state[p] for p in pairs]
        ws_qs = []
        for p in pairs:
            x = jnp.concatenate([sol[p][:, LANE:], q[p] * egc[p]], axis=0)
            x_wide = jnp.concatenate([jnp.where(first_head_rows, x, 0.0),
                                      jnp.where(first_head_rows, 0.0, x)], axis=1)
            ws_qs.append(_dot(x_wide, s_prev[p]))
        v_new = [sol[p][:, :LANE] - ws_qs[p][:c2] for p in pairs]
        o = [ws_qs[p][c2:] + _dot(attn[p], v_new[p]) for p in pairs]
        for p in pairs:
            kdt = k_dec[p].T
            kd_stack = jnp.concatenate([jnp.where(left_cols, kdt, 0.0),
                                        jnp.where(left_cols, 0.0, kdt)], axis=0)
            s_decay = jnp.where(rows2 < c2, jnp.exp(gcol[p][c - 1:c, :]),
                                jnp.exp(gcol[p][c2 - 1:c2, :]))
            state[p] = s_prev[p] * s_decay + _dot(kd_stack, v_new[p])
        for p in pairs:
            b, hp = bof[p], p % ppb
            cs0 = slice((2 * hp) * LANE, (2 * hp + 1) * LANE)
            cs1 = slice((2 * hp + 1) * LANE, (2 * hp + 2) * LANE)
            zg = _silu(jnp.concatenate([z_ref[b, pl.ds(r0, c), cs0],
                                        z_ref[b, pl.ds(r0, c), cs1]], axis=0))
            op = o[p] * lax.rsqrt(jnp.mean(o[p] * o[p], axis=-1, keepdims=True) + EPS)
            op = (op * nw_ref[...] * zg).astype(o_ref.dtype)
            o_ref[b, pl.ds(r0, c), cs0] = op[:c]
            o_ref[b, pl.ds(r0, c), cs1] = op[c:]
        return carry

    for ci in range(sc // c):
        chunk_body(ci, 0)


def _deltanet(proj3, ba3, conv_w, a_log, dt_bias, norm_w, n_heads, col0):
    b, s, _ = proj3.shape
    assert n_heads % 2 == 0 and 2 * n_heads <= LANE
    hb = _largest_tile(n_heads, 12, 2)
    width = hb * LANE
    nhb = n_heads // hb
    sc = _largest_tile(s, DN_SEQ_TILE, DN_CHUNK)
    base = col0 // width
    assert col0 % width == 0
    npairs = b * hb // 2
    gp = jnp.zeros((2, LANE), F32)
    gp = gp.at[0, n_heads:2 * n_heads].set(a_log).at[1, n_heads:2 * n_heads].set(dt_bias)

    def col_spec(group):
        return pl.BlockSpec((b, sc, width),
                            lambda hi, si, g=group: (0, si, base + g * nhb + hi))

    def cw_spec(group):
        return pl.BlockSpec((SHORT_CONV, width), lambda hi, si, g=group: (0, g * nhb + hi))

    return pl.pallas_call(
        functools.partial(_deltanet_kernel, sc=sc, hb=hb, nb=b, n_heads=n_heads),
        grid=(nhb, s // sc),
        in_specs=[col_spec(0), col_spec(1), col_spec(2), col_spec(3),
                  pl.BlockSpec((b, sc, LANE), lambda hi, si: (0, si, 0)),
                  cw_spec(0), cw_spec(1), cw_spec(2),
                  pl.BlockSpec((2, LANE), lambda hi, si: (0, 0)),
                  pl.BlockSpec((1, LANE), lambda hi, si: (0, 0))],
        out_specs=pl.BlockSpec((b, sc, width), lambda hi, si: (0, si, hi)),
        out_shape=jax.ShapeDtypeStruct((b, s, n_heads * LANE), BF16),
        scratch_shapes=[pltpu.VMEM((3 * b, sc + HALO, width), F32),
                        pltpu.VMEM((3, npairs, sc // DN_CHUNK, 2 * DN_CHUNK, LANE), F32),
                        pltpu.VMEM((b, LANE, LANE), F32),
                        pltpu.VMEM((npairs, 2 * HEAD_DIM, HEAD_DIM), F32)],
        compiler_params=_params("parallel", "arbitrary"),
        name="gated_deltanet",
    )(proj3, proj3, proj3, proj3, ba3, conv_w, conv_w, conv_w, gp, norm_w.reshape(1, LANE))


def _rglru_kernel(x_ref, g_ref, cw_ref, cb_ref, wa_ref, ba_ref, wx_ref, bx_ref, lam_ref,
                  nw_ref, o_ref, xpad, abuf, bbuf, hcarry, *, sc, gb):
    c = SCAN_CHUNK
    is_first = pl.program_id(2) == 0

    @pl.when(is_first)
    def _():
        hcarry[...] = jnp.zeros(hcarry.shape, F32)

    _load_with_halo(xpad, 0, x_ref, is_first, sc)

    rows = 2 * c
    for ci in range(sc // rows):
        xc = _short_conv(xpad, 0, cw_ref, ci * rows, rows) + cb_ref[...]
        for g in range(gb):
            cs = slice(g * LANE, (g + 1) * LANE)
            xg = xc[:, cs]
            r = jax.nn.sigmoid(_dot(xg, wa_ref[g]) + ba_ref[:, cs])
            ig = jax.nn.sigmoid(_dot(xg, wx_ref[g]) + bx_ref[:, cs])
            a = jnp.exp(-LRU_C * r * _softplus(-lam_ref[:, cs]))
            abuf[ci * rows:(ci + 1) * rows, cs] = a
            bbuf[ci * rows:(ci + 1) * rows, cs] = (
                jnp.sqrt((1.0 - a) * (1.0 + a)) * (ig * xg))

    row = _iota((HALO, LANE), 0)

    def chunk_body(ci, carry):
        r0 = pl.multiple_of(ci * c, c)
        for g in range(gb):
            cs = slice(g * LANE, (g + 1) * LANE)
            a = abuf[pl.ds(r0, c), cs]
            bb = bbuf[pl.ds(r0, c), cs]
            last = jnp.broadcast_to(hcarry[:, cs], (HALO, LANE))
            tiles = []
            for v in range(c // HALO):
                av = a[v * HALO:(v + 1) * HALO]
                bv = bb[v * HALO:(v + 1) * HALO]
                d = 1
                while d < HALO:
                    keep = row >= d
                    a_sh = jnp.where(keep, pltpu.roll(av, d, 0), 1.0)
                    b_sh = jnp.where(keep, pltpu.roll(bv, d, 0), 0.0)
                    bv = av * b_sh + bv
                    av = av * a_sh
                    d *= 2
                hv = bv + av * last
                last = jnp.broadcast_to(hv[HALO - 1:HALO, :], (HALO, LANE))
                tiles.append(hv)
            h = jnp.concatenate(tiles, axis=0)
            hcarry[:, cs] = last[0:1, :]
            y = h * jax.nn.gelu(g_ref[pl.ds(r0, c), cs])
            y = y * lax.rsqrt(jnp.mean(y * y, axis=-1, keepdims=True) + EPS)
            o_ref[pl.ds(r0, c), cs] = (y * nw_ref[:, cs]).astype(o_ref.dtype)
        return carry

    lax.fori_loop(0, sc // c, chunk_body, 0)


def _rglru(proj3, conv_w, conv_b, w_a, b_a, w_x, b_x, lam, norm_w, col_x, col_g):
    b, s, _ = proj3.shape
    nblk = w_a.shape[0]
    gb = 4 if nblk % 4 == 0 else (3 if nblk % 3 == 0 else 1)
    width = gb * LANE
    lw = nblk * LANE
    sc = _largest_tile(s, 512, 2 * SCAN_CHUNK)
    assert col_x % width == 0 and col_g % width == 0
    bx0, bg0 = col_x // width, col_g // width
    vec = lambda: pl.BlockSpec((1, width), lambda bi, gi, si: (0, gi))
    mat = lambda: pl.BlockSpec((gb, LANE, LANE), lambda bi, gi, si: (gi, 0, 0))
    return pl.pallas_call(
        functools.partial(_rglru_kernel, sc=sc, gb=gb),
        grid=(b, nblk // gb, s // sc),
        in_specs=[pl.BlockSpec((None, sc, width), lambda bi, gi, si: (bi, si, bx0 + gi)),
                  pl.BlockSpec((None, sc, width), lambda bi, gi, si: (bi, si, bg0 + gi)),
                  pl.BlockSpec((SHORT_CONV, width), lambda bi, gi, si: (0, gi)),
                  vec(), mat(), vec(), mat(), vec(), vec(), vec()],
        out_specs=pl.BlockSpec((None, sc, width), lambda bi, gi, si: (bi, si, gi)),
        out_shape=jax.ShapeDtypeStruct((b, s, lw), BF16),
        scratch_shapes=[pltpu.VMEM((1, sc + HALO, width), F32),
                        pltpu.VMEM((sc, width), F32),
                        pltpu.VMEM((sc, width), F32),
                        pltpu.VMEM((1, width), F32)],
        compiler_params=_params("parallel", "parallel", "arbitrary"),
        name="rglru",
    )(proj3, proj3, conv_w, conv_b.reshape(1, lw), w_a, b_a.reshape(1, lw), w_x,
      b_x.reshape(1, lw), lam.reshape(1, lw), norm_w.reshape(1, lw))


def _sg_kernel(u_ref, v_ref, lnw_ref, lnb_ref, ws_ref, bst_ref, nw_ref, o_ref, *, sc,
               groups):
    t = SG_CHUNK
    tril = _iota((t, t), 0) >= _iota((t, t), 1)
    for ci in range(sc // t):
        rs = slice(ci * t, (ci + 1) * t)
        v = jax.nn.gelu(v_ref[rs, :])
        mu = jnp.mean(v, axis=-1, keepdims=True)
        vc = v - mu
        var = jnp.mean(vc * vc, axis=-1, keepdims=True)
        vn = vc * lax.rsqrt(var + EPS) * lnw_ref[...] + lnb_ref[...]
        for g in range(groups):
            cs = slice(g * LANE, (g + 1) * LANE)
            w_causal = jnp.where(tril, ws_ref[g], 0.0)
            z = _dot(w_causal, vn[:, cs]) + bst_ref[:, g:g + 1]
            y = jax.nn.gelu(u_ref[rs, cs]) * z
            y = y * lax.rsqrt(jnp.mean(y * y, axis=-1, keepdims=True) + EPS)
            o_ref[rs, cs] = (y * nw_ref[:, cs]).astype(o_ref.dtype)


def _spatial_gating(proj3, ln_w, ln_b, w_s, b_s, norm_w, col_u, col_v):
    b, s, _ = proj3.shape
    groups = w_s.shape[0]
    width = groups * LANE
    sc = _largest_tile(s, 512, SG_CHUNK)
    assert col_u % width == 0 and col_v % width == 0
    bu0, bv0 = col_u // width, col_v // width
    vec = lambda: pl.BlockSpec((1, width), lambda bi, si: (0, 0))
    return pl.pallas_call(
        functools.partial(_sg_kernel, sc=sc, groups=groups),
        grid=(b, s // sc),
        in_specs=[pl.BlockSpec((None, sc, width), lambda bi, si: (bi, si, bu0)),
                  pl.BlockSpec((None, sc, width), lambda bi, si: (bi, si, bv0)),
                  vec(), vec(),
                  pl.BlockSpec((groups, SG_CHUNK, SG_CHUNK), lambda bi, si: (0, 0, 0)),
                  pl.BlockSpec((SG_CHUNK, groups), lambda bi, si: (0, 0)),
                  vec()],
        out_specs=pl.BlockSpec((None, sc, width), lambda bi, si: (bi, si, 0)),
        out_shape=jax.ShapeDtypeStruct((b, s, width), BF16),
        compiler_params=_params("parallel", "parallel"),
        name="spatial_gating",
    )(proj3, proj3, ln_w.reshape(1, width), ln_b.reshape(1, width), w_s, b_s.T,
      norm_w.reshape(1, width))


def kernel(x, norm_mix, w_in, dn_conv_w, dn_a_log, dn_dt_bias, dn_norm_w, lru_conv_w, lru_conv_b, lru_w_a, lru_b_a, lru_w_x, lru_b_x, lru_lambda, lru_norm_w, sg_ln_w, sg_ln_b, sg_w_s, sg_b_s, sg_norm_w, w_out, norm_ffn, w_up, ffn_conv_w, ffn_conv_b, w_down, norm_final):
    bsz, seq, d = x.shape
    depth = w_in.shape[0]
    n_heads = dn_a_log.shape[1]
    dn_w = n_heads * HEAD_DIM
    lru_w = lru_lambda.shape[1]
    sg_w = sg_ln_w.shape[1]
    m = bsz * seq
    gate0 = 4 * dn_w
    tail0 = gate0 + 2 * n_heads
    main_cols = gate0 + 2 * lru_w + 2 * sg_w
    tn = IN_PROJ_TILE
    assert gate0 % tn == 0 and tail0 % HALO == 0
    w_in_t = jnp.swapaxes(w_in, 1, 2)

    def main_rows(j):
        skip = jnp.where(j < gate0 // tn, 0, (tail0 - gate0) // HALO)
        return HALO * (j * (tn // HALO) + skip)

    x2 = x.reshape(m, d)
    for l in range(depth):
        h = _rmsnorm(x2, norm_mix[l], BF16)
        proj, w_out_bf16, w_down_bf16 = _in_proj(h, w_in_t, l, main_rows, main_cols, tn,
                                                 "in_proj", sides=(w_out, w_down))
        ba, = _in_proj(h, w_in_t, l, lambda j: gate0, LANE, LANE, "in_proj_gates")
        proj3 = proj.reshape(bsz, seq, main_cols)
        ba3 = ba.reshape(bsz, seq, LANE)

        y_a = _deltanet(proj3, ba3, dn_conv_w[l], dn_a_log[l], dn_dt_bias[l],
                        dn_norm_w[l], n_heads, 0)
        y_b = _rglru(proj3, lru_conv_w[l], lru_conv_b[l], lru_w_a[l], lru_b_a[l],
                     lru_w_x[l], lru_b_x[l], lru_lambda[l], lru_norm_w[l],
                     gate0, gate0 + lru_w)
        y_c = _spatial_gating(proj3, sg_ln_w[l], sg_ln_b[l], sg_w_s[l], sg_b_s[l],
                              sg_norm_w[l], gate0 + 2 * lru_w, gate0 + 2 * lru_w + sg_w)
        mix_parts = [y.reshape(m, y.shape[-1]) for y in (y_a, y_b, y_c)]
        x2 = _out_proj(mix_parts, w_out_bf16, x2, "out_proj")

        h = _rmsnorm(x2, norm_ffn[l], BF16)
        act = _ffn_up(h, w_up, l, ffn_conv_w[l], ffn_conv_b[l], seq)
        x2 = _matmul(act, w_down_bf16, residual=x2, tk_cap=w_down_bf16.shape[0],
                     name="down_proj")
    return _rmsnorm(x2, norm_final, F32).reshape(bsz, seq, d)
```

```python
import functools

import jax
import jax.numpy as jnp
from jax import lax
from jax.experimental import pallas as pl
from jax.experimental.pallas import tpu as pltpu

F32 = jnp.float32
BF16 = jnp.bfloat16
HIGHEST = lax.Precision.HIGHEST

EPS = 1e-6
LANE = 128
HALO = 8
HEAD_DIM = LANE
DN_CHUNK = 64
DN_SEQ_TILE = 256
SG_CHUNK = 128
SCAN_CHUNK = 64
SHORT_CONV = 4
LRU_C = 8.0
VMEM_LIMIT_BYTES = 62 * 1024 * 1024


def _params(*semantics):
    return pltpu.CompilerParams(dimension_semantics=semantics,
                                vmem_limit_bytes=VMEM_LIMIT_BYTES)


def _largest_tile(n, cap, unit):
    best = None
    t = unit
    while t <= min(n, cap):
        if n % t == 0:
            best = t
        t += unit
    assert best is not None, (n, cap, unit)
    return best


def _dot(a, b):
    return jnp.dot(a.astype(BF16), b.astype(BF16), preferred_element_type=F32)


def _dot_nt(a, b):
    return lax.dot_general(a.astype(BF16), b.astype(BF16), (((1,), (1,)), ((), ())),
                           preferred_element_type=F32)


def _softplus(x):
    return jnp.maximum(x, 0.0) + jnp.log1p(jnp.exp(-jnp.abs(x)))


def _silu(x):
    return x * jax.nn.sigmoid(x)


def _iota(shape, dim):
    return lax.broadcasted_iota(jnp.int32, shape, dim)


def _rmsnorm_kernel(x_ref, w_ref, o_ref):
    x = x_ref[...]
    ms = jnp.mean(x * x, axis=-1, keepdims=True)
    o_ref[...] = (x * lax.rsqrt(ms + EPS) * w_ref[...]).astype(o_ref.dtype)


def _rmsnorm(x2, w, out_dtype):
    m, d = x2.shape
    tm = _largest_tile(m, 512, 16)
    return pl.pallas_call(
        _rmsnorm_kernel,
        grid=(m // tm,),
        in_specs=[pl.BlockSpec((tm, d), lambda i: (i, 0)),
                  pl.BlockSpec((1, d), lambda i: (0, 0))],
        out_specs=pl.BlockSpec((tm, d), lambda i: (i, 0)),
        out_shape=jax.ShapeDtypeStruct((m, d), out_dtype),
        compiler_params=_params("parallel"),
        name="rmsnorm",
    )(x2, w.reshape(1, d))


def _rmsnorm_gates_kernel(x_ref, w_ref, wg_ref, o_ref, ba_ref, wgb):
    @pl.when(pl.program_id(0) == 0)
    def _():
        _cast_rows(wg_ref.at[0], wgb)

    x = x_ref[...]
    ms = jnp.mean(x * x, axis=-1, keepdims=True)
    h = (x * lax.rsqrt(ms + EPS) * w_ref[...]).astype(o_ref.dtype)
    o_ref[...] = h
    ba_ref[...] = lax.dot_general(h, wgb[...], (((1,), (1,)), ((), ())),
                                  preferred_element_type=F32)


def _rmsnorm_gates(x2, w, w_t, layer, row0):
    m, d = x2.shape
    tm = _largest_tile(m, 512, 16)
    return pl.pallas_call(
        _rmsnorm_gates_kernel,
        grid=(m // tm,),
        in_specs=[pl.BlockSpec((tm, d), lambda i: (i, 0)),
                  pl.BlockSpec((1, d), lambda i: (0, 0)),
                  pl.BlockSpec((pl.Element(1), pl.Element(LANE), pl.Element(d)),
                               lambda i: (layer, row0, 0))],
        out_specs=[pl.BlockSpec((tm, d), lambda i: (i, 0)),
                   pl.BlockSpec((tm, LANE), lambda i: (i, 0))],
        out_shape=[jax.ShapeDtypeStruct((m, d), BF16),
                   jax.ShapeDtypeStruct((m, LANE), F32)],
        scratch_shapes=[pltpu.VMEM((LANE, d), BF16)],
        compiler_params=_params("arbitrary"),
        name="rmsnorm_gates",
    )(x2, w.reshape(1, d), w_t)


def _matmul_kernel(*refs, nk, has_res):
    if has_res:
        a_ref, w_ref, res_ref, o_ref = refs[:4]
    else:
        a_ref, w_ref, o_ref = refs[:3]
        res_ref = None
    part = jnp.dot(a_ref[...], w_ref[...], preferred_element_type=F32)

    def finish(acc):
        if res_ref is not None:
            acc = acc + res_ref[...]
        o_ref[...] = acc.astype(o_ref.dtype)

    if nk == 1:
        finish(part)
        return
    acc_ref = refs[-1]
    k = pl.program_id(2)

    @pl.when(k == 0)
    def _():
        acc_ref[...] = part

    @pl.when(k > 0)
    def _():
        acc_ref[...] += part

    @pl.when(k == nk - 1)
    def _():
        finish(acc_ref[...])


def _matmul(a, w, *, residual=None, out_dtype=F32, tm_cap=512, tn_cap=512, tk_cap=4096,
            name="matmul"):
    m, k = a.shape
    n = w.shape[1]
    tm = _largest_tile(m, tm_cap, 16)
    tn = _largest_tile(n, tn_cap, LANE)
    tk = _largest_tile(k, tk_cap, LANE)
    nk = k // tk
    in_specs = [pl.BlockSpec((tm, tk), lambda j, i, kk: (i, kk)),
                pl.BlockSpec((tk, tn), lambda j, i, kk: (kk, j))]
    args = [a, w]
    if residual is not None:
        in_specs.append(pl.BlockSpec((tm, tn), lambda j, i, kk: (i, j)))
        args.append(residual)
    scratch = [pltpu.VMEM((tm, tn), F32)] if nk > 1 else []
    return pl.pallas_call(
        functools.partial(_matmul_kernel, nk=nk, has_res=residual is not None),
        grid=(n // tn, m // tm, nk),
        in_specs=in_specs,
        out_specs=pl.BlockSpec((tm, tn), lambda j, i, kk: (i, j)),
        out_shape=jax.ShapeDtypeStruct((m, n), out_dtype),
        scratch_shapes=scratch,
        compiler_params=_params("parallel", "parallel", "arbitrary"),
        name=name,
    )(*args)


CAST_ROWS = 256
IN_PROJ_TILE = 1024
IN_PROJ_ROWS = 512


def _cast_rows(src_ref, dst_ref):
    step = min(CAST_ROWS, src_ref.shape[0])
    assert src_ref.shape[0] % step == 0

    def body(r, carry):
        rows = pl.ds(pl.multiple_of(r * step, step), step)
        dst_ref[rows, :] = src_ref[rows, :].astype(dst_ref.dtype)
        return carry

    lax.fori_loop(0, src_ref.shape[0] // step, body, 0)


def _weight_spec(w, layer, k, tn, col_block0=0):
    if layer is None:
        return pl.BlockSpec((k, tn), lambda j, i: (0, col_block0 + j))
    return pl.BlockSpec((None, k, tn), lambda j, i: (layer, 0, col_block0 + j))


def _out_proj_kernel(*refs, widths):
    a_refs = refs[:len(widths)]
    w_ref, res_ref, o_ref = refs[len(widths):]
    acc = res_ref[...]
    k0 = 0
    for a_ref, wk in zip(a_refs, widths):
        acc = acc + jnp.dot(a_ref[...], w_ref[k0:k0 + wk, :], preferred_element_type=F32)
        k0 += wk
    o_ref[...] = acc


def _out_proj(a_parts, w, residual, name):
    m = a_parts[0].shape[0]
    widths = tuple(a.shape[1] for a in a_parts)
    k, n = w.shape
    assert k == sum(widths)
    tm = _largest_tile(m, 1024, 16)
    tn = _largest_tile(n, 1024, LANE)
    in_specs = [pl.BlockSpec((tm, wk), lambda j, i: (i, 0)) for wk in widths]
    in_specs += [pl.BlockSpec((k, tn), lambda j, i: (0, j)),
                 pl.BlockSpec((tm, tn), lambda j, i: (i, j))]
    return pl.pallas_call(
        functools.partial(_out_proj_kernel, widths=widths),
        grid=(n // tn, m // tm),
        in_specs=in_specs,
        out_specs=pl.BlockSpec((tm, tn), lambda j, i: (i, j)),
        out_shape=jax.ShapeDtypeStruct((m, n), F32),
        compiler_params=_params("arbitrary", "arbitrary"),
        name=name,
    )(*a_parts, w, residual)


def _in_proj_kernel(a_ref, wt_ref, *rest, n_side):
    side_refs = rest[:n_side]
    o_ref = rest[n_side]
    side_out_refs = rest[n_side + 1:2 * n_side + 1]
    wb_ref = rest[-1]

    for src, dst in zip(side_refs, side_out_refs):
        dst[...] = src[...].astype(dst.dtype)

    @pl.when(pl.program_id(1) == 0)
    def _():
        _cast_rows(wt_ref.at[0], wb_ref)

    o_ref[...] = lax.dot_general(a_ref[...], wb_ref[...], (((1,), (1,)), ((), ())),
                                 preferred_element_type=F32)


def _in_proj(a, w_t, layer, row_start, n, tn, name, sides=()):
    m, k = a.shape
    assert n % tn == 0
    tm = _largest_tile(m, IN_PROJ_ROWS, 16)
    row_tiles = m // tm
    n_steps = (n // tn) * row_tiles
    in_specs = [pl.BlockSpec((tm, k), lambda j, i: (i, 0)),
                pl.BlockSpec((pl.Element(1), pl.Element(tn), pl.Element(k)),
                             lambda j, i: (layer, row_start(j), 0))]
    out_specs = [pl.BlockSpec((tm, tn), lambda j, i: (i, j))]
    out_shape = [jax.ShapeDtypeStruct((m, n), F32)]
    for side in sides:
        rows, cols = side.shape[1:]
        slab_rows = min(s for s in range(16, rows + 1, 16)
                        if rows % s == 0 and rows // s <= n_steps)

        def slab(j, i, n_slabs=rows // slab_rows):
            return jnp.minimum(j * row_tiles + i, n_slabs - 1)

        in_specs.append(pl.BlockSpec((None, slab_rows, cols),
                                     lambda j, i, slab=slab: (layer, slab(j, i), 0)))
        out_specs.append(pl.BlockSpec((slab_rows, cols),
                                      lambda j, i, slab=slab: (slab(j, i), 0)))
        out_shape.append(jax.ShapeDtypeStruct((rows, cols), BF16))
    return pl.pallas_call(
        functools.partial(_in_proj_kernel, n_side=len(sides)),
        grid=(n // tn, row_tiles),
        in_specs=in_specs,
        out_specs=out_specs,
        out_shape=out_shape,
        scratch_shapes=[pltpu.VMEM((tn, k), BF16)],
        compiler_params=_params("arbitrary", "arbitrary"),
        name=name,
    )(a, w_t, *sides)


FFN_ROW_TILE = 2048
FFN_ROW_BLOCK = 512


def _ffn_up_kernel(a_ref, wg_ref, wu_ref, cwg_ref, cwu_ref, bg_ref, bu_ref, o_ref,
                   wgb, wub, gbuf, ubuf, *, tm, rb, tiles_per_seq):
    i = pl.program_id(1)
    taps = cwg_ref.shape[0]

    @pl.when(i == 0)
    def _():
        _cast_rows(wg_ref, wgb)
        _cast_rows(wu_ref, wub)

    first = (i % tiles_per_seq) == 0
    for buf in (gbuf, ubuf):
        @pl.when(first)
        def _():
            buf[0:HALO, :] = jnp.zeros((HALO, buf.shape[1]), F32)

        @pl.when(jnp.logical_not(first))
        def _():
            buf[0:HALO, :] = buf[tm:tm + HALO, :]

    def conv(buf, cw_ref, r0):
        p = buf[pl.ds(r0, rb + HALO), :]
        acc = cw_ref[0:1, :] * p
        for j in range(1, taps):
            acc = cw_ref[j:j + 1, :] * p + pltpu.roll(acc, 1, 0)
        return acc[HALO:]

    for r in range(tm // rb):
        a = a_ref[r * rb:(r + 1) * rb, :]
        gbuf[HALO + r * rb:HALO + (r + 1) * rb, :] = jnp.dot(
            a, wgb[...], preferred_element_type=F32)
        ubuf[HALO + r * rb:HALO + (r + 1) * rb, :] = jnp.dot(
            a, wub[...], preferred_element_type=F32)
        g = conv(gbuf, cwg_ref, r * rb) + bg_ref[...]
        u = conv(ubuf, cwu_ref, r * rb) + bu_ref[...]
        o_ref[r * rb:(r + 1) * rb, :] = (_silu(g) * u).astype(o_ref.dtype)


def _ffn_up(h, w_up, layer, conv_w, conv_b, seq):
    m, k = h.shape
    f = w_up.shape[-1] // 2
    taps = conv_w.shape[0]
    tm = _largest_tile(seq, FFN_ROW_TILE, FFN_ROW_BLOCK)
    tf = _largest_tile(f, 256, LANE)
    nf = f // tf
    assert k % CAST_ROWS == 0
    conv_b = conv_b.reshape(1, 2 * f)
    return pl.pallas_call(
        functools.partial(_ffn_up_kernel, tm=tm, rb=FFN_ROW_BLOCK, tiles_per_seq=seq // tm),
        grid=(nf, m // tm),
        in_specs=[pl.BlockSpec((tm, k), lambda j, i: (i, 0)),
                  _weight_spec(w_up, layer, k, tf),
                  _weight_spec(w_up, layer, k, tf, nf),
                  pl.BlockSpec((taps, tf), lambda j, i: (0, j)),
                  pl.BlockSpec((taps, tf), lambda j, i: (0, nf + j)),
                  pl.BlockSpec((1, tf), lambda j, i: (0, j)),
                  pl.BlockSpec((1, tf), lambda j, i: (0, nf + j))],
        out_specs=pl.BlockSpec((tm, tf), lambda j, i: (i, j)),
        out_shape=jax.ShapeDtypeStruct((m, f), BF16),
        scratch_shapes=[pltpu.VMEM((k, tf), BF16),
                        pltpu.VMEM((k, tf), BF16),
                        pltpu.VMEM((tm + HALO, tf), F32),
                        pltpu.VMEM((tm + HALO, tf), F32)],
        compiler_params=_params("arbitrary", "arbitrary"),
        name="ffn_up_conv_gate",
    )(h, w_up, w_up, conv_w, conv_w, conv_b, conv_b)


def _load_with_halo(xpad, idx, src_ref, is_first, rows):
    width = xpad.shape[-1]

    @pl.when(is_first)
    def _():
        xpad[idx, 0:HALO, :] = jnp.zeros((HALO, width), F32)

    @pl.when(jnp.logical_not(is_first))
    def _():
        xpad[idx, 0:HALO, :] = xpad[idx, rows:rows + HALO, :]

    xpad[idx, HALO:, :] = src_ref[...]


def _short_conv(xpad, idx, cw_ref, r0, rows, cs=slice(None)):
    p = xpad[idx, pl.ds(r0, rows + HALO), cs]
    acc = cw_ref[0:1, cs] * p
    for j in range(1, cw_ref.shape[0]):
        acc = cw_ref[j:j + 1, cs] * p + pltpu.roll(acc, 1, 0)
    return acc[HALO:]


def _deltanet_kernel(q_ref, k_ref, v_ref, z_ref, ba_ref, cwq_ref, cwk_ref, cwv_ref,
                     gp_ref, nw_ref, o_ref, xpad, qkvs, gct, state, *, sc, hb, nb, n_heads):
    c = DN_CHUNK
    c2 = 2 * c
    ppb = hb // 2
    npairs = nb * ppb
    hblk = pl.program_id(0)
    is_first = pl.program_id(1) == 0

    @pl.when(is_first)
    def _():
        state[...] = jnp.zeros(state.shape, F32)

    for idx, src in enumerate((q_ref, k_ref, v_ref)):
        for b in range(nb):
            _load_with_halo(xpad, idx * nb + b, src.at[b], is_first, sc)

    for idx, cw_ref in enumerate((cwq_ref, cwk_ref, cwv_ref)):
        for b in range(nb):
            for ci in range(sc // c):
                for i in range(hb):
                    cs = slice(i * LANE, (i + 1) * LANE)
                    seg = _silu(_short_conv(xpad, idx * nb + b, cw_ref, ci * c, c, cs))
                    if idx < 2:
                        seg = seg * lax.rsqrt(
                            jnp.sum(seg * seg, axis=-1, keepdims=True) + EPS)
                    qkvs[idx, b * ppb + i // 2, ci, (i % 2) * c:(i % 2 + 1) * c, :] = seg

    row = _iota((c2, c2), 0)
    col = _iota((c2, c2), 1)
    same_head = (row >= c) == (col >= c)
    causal = (row >= col) & same_head
    strict = (row > col) & same_head
    second = (row >= c).astype(jnp.int32)
    left_cols = col < c
    eye = (row == col).astype(F32)
    tril = (_iota((c, c), 0) >= _iota((c, c), 1)).astype(F32)
    lane_row = _iota((1, c2), 1)
    rows1 = _iota((c2, 1), 0)
    rows2 = _iota((2 * c2, 1), 0)
    first_head_rows = (rows2 % c2) < c
    scale = HEAD_DIM ** -0.5
    pairs = range(npairs)

    def chunk_body(ci, carry):
        r0 = ci * c
        gc_stack, beta_stack = [], []
        for b in range(nb):
            ba = ba_ref[b, pl.ds(r0, c), :]
            beta_all = jax.nn.sigmoid(ba)
            g_all = -jnp.exp(gp_ref[0:1, :]) * _softplus(ba + gp_ref[1:2, :])
            gc_all = jnp.dot(tril, g_all, precision=HIGHEST,
                             preferred_element_type=F32)
            gc_stack.append(jnp.concatenate([gc_all, gc_all], axis=0))
            beta_stack.append(jnp.concatenate([beta_all, beta_all], axis=0))
            gct[b] = gc_stack[b].T
        bof = [u // ppb for u in pairs]
        h0 = [hblk * hb + 2 * (u % ppb) for u in pairs]
        bcol = [jnp.sum(jnp.where(col == h0[u] + second, beta_stack[bof[u]], 0.0), axis=-1,
                        keepdims=True) for u in pairs]
        gcol = [jnp.sum(jnp.where(col == h0[u] + n_heads + second, gc_stack[bof[u]], 0.0),
                        axis=-1, keepdims=True) for u in pairs]
        grow = [jnp.where(lane_row < c, gct[bof[u], pl.ds(h0[u] + n_heads, 1), :],
                          gct[bof[u], pl.ds(h0[u] + n_heads + 1, 1), :])
                for u in pairs]
        glast = [jnp.where(rows1 < c, g[c - 1:c, :], g[c2 - 1:c2, :]) for g in gcol]
        decay = [jnp.where(causal, jnp.exp(jnp.where(causal, gcol[p] - grow[p], 0.0)), 0.0)
                 for p in pairs]
        egc = [jnp.exp(g) for g in gcol]

        q = [qkvs[0, p, ci] * scale for p in pairs]
        k = [qkvs[1, p, ci] for p in pairs]
        v = [qkvs[2, p, ci] for p in pairs]
        kb = [k[p] * bcol[p] for p in pairs]
        kq = [_dot_nt(jnp.concatenate([kb[p], q[p]], axis=0), k[p]) for p in pairs]
        a_low = [jnp.where(strict, kq[p][:c2] * decay[p], 0.0) for p in pairs]
        attn = [kq[p][c2:] * decay[p] for p in pairs]

        t_inv = [eye - a for a in a_low]
        pw = a_low
        n = 2
        while n < c:
            pw = [_dot(x, x) for x in pw]
            t_inv = [t_inv[p] + _dot(t_inv[p], pw[p]) for p in pairs]
            n *= 2

        sol = [_dot(t_inv[p], jnp.concatenate([v[p] * bcol[p], kb[p] * egc[p]], axis=1))
               for p in pairs]
        k_dec = [k[p] * jnp.exp(glast[p] - gcol[p]) for p in pairs]
        s_prev = [state[p] for p in pairs]
        ws_qs = []
        for p in pairs:
            x = jnp.concatenate([sol[p][:, LANE:], q[p] * egc[p]], axis=0)
            x_wide = jnp.concatenate([jnp.where(first_head_rows, x, 0.0),
                                      jnp.where(first_head_rows, 0.0, x)], axis=1)
            ws_qs.append(_dot(x_wide, s_prev[p]))
        v_new = [sol[p][:, :LANE] - ws_qs[p][:c2] for p in pairs]
        o = [ws_qs[p][c2:] + _dot(attn[p], v_new[p]) for p in pairs]
        for p in pairs:
            kdt = k_dec[p].T
            kd_stack = jnp.concatenate([jnp.where(left_cols, kdt, 0.0),
                                        jnp.where(left_cols, 0.0, kdt)], axis=0)
            s_decay = jnp.where(rows2 < c2, jnp.exp(gcol[p][c - 1:c, :]),
                                jnp.exp(gcol[p][c2 - 1:c2, :]))
            state[p] = s_prev[p] * s_decay + _dot(kd_stack, v_new[p])
        for p in pairs:
            b, hp = bof[p], p % ppb
            cs0 = slice((2 * hp) * LANE, (2 * hp + 1) * LANE)
            cs1 = slice((2 * hp + 1) * LANE, (2 * hp + 2) * LANE)
            zg = _silu(jnp.concatenate([z_ref[b, pl.ds(r0, c), cs0],
                                        z_ref[b, pl.ds(r0, c), cs1]], axis=0))
            op = o[p] * lax.rsqrt(jnp.mean(o[p] * o[p], axis=-1, keepdims=True) + EPS)
            op = (op * nw_ref[...] * zg).astype(o_ref.dtype)
            o_ref[b, pl.ds(r0, c), cs0] = op[:c]
            o_ref[b, pl.ds(r0, c), cs1] = op[c:]
        return carry

    for ci in range(sc // c):
        chunk_body(ci, 0)


def _deltanet(proj3, ba3, conv_w, a_log, dt_bias, norm_w, n_heads, col0):
    b, s, _ = proj3.shape
    assert n_heads % 2 == 0 and 2 * n_heads <= LANE
    hb = _largest_tile(n_heads, 12, 2)
    width = hb * LANE
    nhb = n_heads // hb
    sc = _largest_tile(s, DN_SEQ_TILE, DN_CHUNK)
    base = col0 // width
    assert col0 % width == 0
    npairs = b * hb // 2
    gp = jnp.zeros((2, LANE), F32)
    gp = gp.at[0, n_heads:2 * n_heads].set(a_log).at[1, n_heads:2 * n_heads].set(dt_bias)

    def col_spec(group):
        return pl.BlockSpec((b, sc, width),
                            lambda hi, si, g=group: (0, si, base + g * nhb + hi))

    def cw_spec(group):
        return pl.BlockSpec((SHORT_CONV, width), lambda hi, si, g=group: (0, g * nhb + hi))

    return pl.pallas_call(
        functools.partial(_deltanet_kernel, sc=sc, hb=hb, nb=b, n_heads=n_heads),
        grid=(nhb, s // sc),
        in_specs=[col_spec(0), col_spec(1), col_spec(2), col_spec(3),
                  pl.BlockSpec((b, sc, LANE), lambda hi, si: (0, si, 0)),
                  cw_spec(0), cw_spec(1), cw_spec(2),
                  pl.BlockSpec((2, LANE), lambda hi, si: (0, 0)),
                  pl.BlockSpec((1, LANE), lambda hi, si: (0, 0))],
        out_specs=pl.BlockSpec((b, sc, width), lambda hi, si: (0, si, hi)),
        out_shape=jax.ShapeDtypeStruct((b, s, n_heads * LANE), BF16),
        scratch_shapes=[pltpu.VMEM((3 * b, sc + HALO, width), F32),
                        pltpu.VMEM((3, npairs, sc // DN_CHUNK, 2 * DN_CHUNK, LANE), F32),
                        pltpu.VMEM((b, LANE, LANE), F32),
                        pltpu.VMEM((npairs, 2 * HEAD_DIM, HEAD_DIM), F32)],
        compiler_params=_params("parallel", "arbitrary"),
        name="gated_deltanet",
    )(proj3, proj3, proj3, proj3, ba3, conv_w, conv_w, conv_w, gp, norm_w.reshape(1, LANE))


def _rglru_kernel(x_ref, g_ref, cw_ref, cb_ref, wa_ref, ba_ref, wx_ref, bx_ref, lam_ref,
                  nw_ref, o_ref, xpad, abuf, bbuf, hcarry, *, sc, gb):
    c = SCAN_CHUNK
    is_first = pl.program_id(2) == 0

    @pl.when(is_first)
    def _():
        hcarry[...] = jnp.zeros(hcarry.shape, F32)

    _load_with_halo(xpad, 0, x_ref, is_first, sc)

    rows = 2 * c
    for ci in range(sc // rows):
        xc = _short_conv(xpad, 0, cw_ref, ci * rows, rows) + cb_ref[...]
        for g in range(gb):
            cs = slice(g * LANE, (g + 1) * LANE)
            xg = xc[:, cs]
            r = jax.nn.sigmoid(_dot(xg, wa_ref[g]) + ba_ref[:, cs])
            ig = jax.nn.sigmoid(_dot(xg, wx_ref[g]) + bx_ref[:, cs])
            a = jnp.exp(-LRU_C * r * _softplus(-lam_ref[:, cs]))
            abuf[ci * rows:(ci + 1) * rows, cs] = a
            bbuf[ci * rows:(ci + 1) * rows, cs] = (
                jnp.sqrt((1.0 - a) * (1.0 + a)) * (ig * xg))

    row = _iota((HALO, LANE), 0)

    def chunk_body(ci, carry):
        r0 = pl.multiple_of(ci * c, c)
        for g in range(gb):
            cs = slice(g * LANE, (g + 1) * LANE)
            a = abuf[pl.ds(r0, c), cs]
            bb = bbuf[pl.ds(r0, c), cs]
            last = jnp.broadcast_to(hcarry[:, cs], (HALO, LANE))
            tiles = []
            for v in range(c // HALO):
                av = a[v * HALO:(v + 1) * HALO]
                bv = bb[v * HALO:(v + 1) * HALO]
                d = 1
                while d < HALO:
                    keep = row >= d
                    a_sh = jnp.where(keep, pltpu.roll(av, d, 0), 1.0)
                    b_sh = jnp.where(keep, pltpu.roll(bv, d, 0), 0.0)
                    bv = av * b_sh + bv
                    av = av * a_sh
                    d *= 2
                hv = bv + av * last
                last = jnp.broadcast_to(hv[HALO - 1:HALO, :], (HALO, LANE))
                tiles.append(hv)
            h = jnp.concatenate(tiles, axis=0)
            hcarry[:, cs] = last[0:1, :]
            y = h * jax.nn.gelu(g_ref[pl.ds(r0, c), cs])
            y = y * lax.rsqrt(jnp.mean(y * y, axis=-1, keepdims=True) + EPS)
            o_ref[pl.ds(r0, c), cs] = (y * nw_ref[:, cs]).astype(o_ref.dtype)
        return carry

    lax.fori_loop(0, sc // c, chunk_body, 0)


def _rglru(proj3, conv_w, conv_b, w_a, b_a, w_x, b_x, lam, norm_w, col_x, col_g):
    b, s, _ = proj3.shape
    nblk = w_a.shape[0]
    gb = 4 if nblk % 4 == 0 else (3 if nblk % 3 == 0 else 1)
    width = gb * LANE
    lw = nblk * LANE
    sc = _largest_tile(s, 512, 2 * SCAN_CHUNK)
    assert col_x % width == 0 and col_g % width == 0
    bx0, bg0 = col_x // width, col_g // width
    vec = lambda: pl.BlockSpec((1, width), lambda bi, gi, si: (0, gi))
    mat = lambda: pl.BlockSpec((gb, LANE, LANE), lambda bi, gi, si: (gi, 0, 0))
    return pl.pallas_call(
        functools.partial(_rglru_kernel, sc=sc, gb=gb),
        grid=(b, nblk // gb, s // sc),
        in_specs=[pl.BlockSpec((None, sc, width), lambda bi, gi, si: (bi, si, bx0 + gi)),
                  pl.BlockSpec((None, sc, width), lambda bi, gi, si: (bi, si, bg0 + gi)),
                  pl.BlockSpec((SHORT_CONV, width), lambda bi, gi, si: (0, gi)),
                  vec(), mat(), vec(), mat(), vec(), vec(), vec()],
        out_specs=pl.BlockSpec((None, sc, width), lambda bi, gi, si: (bi, si, gi)),
        out_shape=jax.ShapeDtypeStruct((b, s, lw), BF16),
        scratch_shapes=[pltpu.VMEM((1, sc + HALO, width), F32),
                        pltpu.VMEM((sc, width), F32),
                        pltpu.VMEM((sc, width), F32),
                        pltpu.VMEM((1, width), F32)],
        compiler_params=_params("parallel", "parallel", "arbitrary"),
        name="rglru",
    )(proj3, proj3, conv_w, conv_b.reshape(1, lw), w_a, b_a.reshape(1, lw), w_x,
      b_x.reshape(1, lw), lam.reshape(1, lw), norm_w.reshape(1, lw))


def _sg_kernel(u_ref, v_ref, lnw_ref, lnb_ref, ws_ref, bst_ref, nw_ref, o_ref, *, sc,
               groups):
    t = SG_CHUNK
    tril = _iota((t, t), 0) >= _iota((t, t), 1)
    for ci in range(sc // t):
        rs = slice(ci * t, (ci + 1) * t)
        v = jax.nn.gelu(v_ref[rs, :])
        mu = jnp.mean(v, axis=-1, keepdims=True)
        vc = v - mu
        var = jnp.mean(vc * vc, axis=-1, keepdims=True)
        vn = vc * lax.rsqrt(var + EPS) * lnw_ref[...] + lnb_ref[...]
        for g in range(groups):
            cs = slice(g * LANE, (g + 1) * LANE)
            w_causal = jnp.where(tril, ws_ref[g], 0.0)
            z = _dot(w_causal, vn[:, cs]) + bst_ref[:, g:g + 1]
            y = jax.nn.gelu(u_ref[rs, cs]) * z
            y = y * lax.rsqrt(jnp.mean(y * y, axis=-1, keepdims=True) + EPS)
            o_ref[rs, cs] = (y * nw_ref[:, cs]).astype(o_ref.dtype)


def _spatial_gating(proj3, ln_w, ln_b, w_s, b_s, norm_w, col_u, col_v):
    b, s, _ = proj3.shape
    groups = w_s.shape[0]
    width = groups * LANE
    sc = _largest_tile(s, 512, SG_CHUNK)
    assert col_u % width == 0 and col_v % width == 0
    bu0, bv0 = col_u // width, col_v // width
    vec = lambda: pl.BlockSpec((1, width), lambda bi, si: (0, 0))
    return pl.pallas_call(
        functools.partial(_sg_kernel, sc=sc, groups=groups),
        grid=(b, s // sc),
        in_specs=[pl.BlockSpec((None, sc, width), lambda bi, si: (bi, si, bu0)),
                  pl.BlockSpec((None, sc, width), lambda bi, si: (bi, si, bv0)),
                  vec(), vec(),
                  pl.BlockSpec((groups, SG_CHUNK, SG_CHUNK), lambda bi, si: (0, 0, 0)),
                  pl.BlockSpec((SG_CHUNK, groups), lambda bi, si: (0, 0)),
                  vec()],
        out_specs=pl.BlockSpec((None, sc, width), lambda bi, si: (bi, si, 0)),
        out_shape=jax.ShapeDtypeStruct((b, s, width), BF16),
        compiler_params=_params("parallel", "parallel"),
        name="spatial_gating",
    )(proj3, proj3, ln_w.reshape(1, width), ln_b.reshape(1, width), w_s, b_s.T,
      norm_w.reshape(1, width))


def kernel(x, norm_mix, w_in, dn_conv_w, dn_a_log, dn_dt_bias, dn_norm_w, lru_conv_w, lru_conv_b, lru_w_a, lru_b_a, lru_w_x, lru_b_x, lru_lambda, lru_norm_w, sg_ln_w, sg_ln_b, sg_w_s, sg_b_s, sg_norm_w, w_out, norm_ffn, w_up, ffn_conv_w, ffn_conv_b, w_down, norm_final):
    bsz, seq, d = x.shape
    depth = w_in.shape[0]
    n_heads = dn_a_log.shape[1]
    dn_w = n_heads * HEAD_DIM
    lru_w = lru_lambda.shape[1]
    sg_w = sg_ln_w.shape[1]
    m = bsz * seq
    gate0 = 4 * dn_w
    tail0 = gate0 + 2 * n_heads
    main_cols = gate0 + 2 * lru_w + 2 * sg_w
    tn = IN_PROJ_TILE
    assert gate0 % tn == 0 and tail0 % HALO == 0
    w_in_t = jnp.swapaxes(w_in, 1, 2)

    def main_rows(j):
        skip = jnp.where(j < gate0 // tn, 0, (tail0 - gate0) // HALO)
        return HALO * (j * (tn // HALO) + skip)

    x2 = x.reshape(m, d)
    for l in range(depth):
        h, ba = _rmsnorm_gates(x2, norm_mix[l], w_in_t, l, gate0)
        proj, w_out_bf16, w_down_bf16 = _in_proj(h, w_in_t, l, main_rows, main_cols, tn,
                                                 "in_proj", sides=(w_out, w_down))
        proj3 = proj.reshape(bsz, seq, main_cols)
        ba3 = ba.reshape(bsz, seq, LANE)

        y_a = _deltanet(proj3, ba3, dn_conv_w[l], dn_a_log[l], dn_dt_bias[l],
                        dn_norm_w[l], n_heads, 0)
        y_b = _rglru(proj3, lru_conv_w[l], lru_conv_b[l], lru_w_a[l], lru_b_a[l],
                     lru_w_x[l], lru_b_x[l], lru_lambda[l], lru_norm_w[l],
                     gate0, gate0 + lru_w)
        y_c = _spatial_gating(proj3, sg_ln_w[l], sg_ln_b[l], sg_w_s[l], sg_b_s[l],
                              sg_norm_w[l], gate0 + 2 * lru_w, gate0 + 2 * lru_w + sg_w)
        mix_parts = [y.reshape(m, y.shape[-1]) for y in (y_a, y_b, y_c)]
        x2 = _out_proj(mix_parts, w_out_bf16, x2, "out_proj")

        h = _rmsnorm(x2, norm_ffn[l], BF16)
        act = _ffn_up(h, w_up, l, ffn_conv_w[l], ffn_conv_b[l], seq)
        x2 = _matmul(act, w_down_bf16, residual=x2, tk_cap=w_down_bf16.shape[0],
                     name="down_proj")
    return _rmsnorm(x2, norm_final, F32).reshape(bsz, seq, d)
```

```python
import functools

import jax
import jax.numpy as jnp
from jax import lax
from jax.experimental import pallas as pl
from jax.experimental.pallas import tpu as pltpu

F32 = jnp.float32
BF16 = jnp.bfloat16
HIGHEST = lax.Precision.HIGHEST

EPS = 1e-6
LANE = 128
HALO = 8
HEAD_DIM = LANE
DN_CHUNK = 64
DN_SEQ_TILE = 256
SG_CHUNK = 128
SCAN_CHUNK = 64
SHORT_CONV = 4
LRU_C = 8.0
VMEM_LIMIT_BYTES = 62 * 1024 * 1024


def _params(*semantics):
    return pltpu.CompilerParams(dimension_semantics=semantics,
                                vmem_limit_bytes=VMEM_LIMIT_BYTES)


def _largest_tile(n, cap, unit):
    best = None
    t = unit
    while t <= min(n, cap):
        if n % t == 0:
            best = t
        t += unit
    assert best is not None, (n, cap, unit)
    return best


def _dot(a, b):
    return jnp.dot(a.astype(BF16), b.astype(BF16), preferred_element_type=F32)


def _dot_nt(a, b):
    return lax.dot_general(a.astype(BF16), b.astype(BF16), (((1,), (1,)), ((), ())),
                           preferred_element_type=F32)


def _softplus(x):
    return jnp.maximum(x, 0.0) + jnp.log1p(jnp.exp(-jnp.abs(x)))


def _silu(x):
    return x * jax.nn.sigmoid(x)


def _iota(shape, dim):
    return lax.broadcasted_iota(jnp.int32, shape, dim)


def _rmsnorm_kernel(x_ref, w_ref, o_ref):
    x = x_ref[...]
    ms = jnp.mean(x * x, axis=-1, keepdims=True)
    o_ref[...] = (x * lax.rsqrt(ms + EPS) * w_ref[...]).astype(o_ref.dtype)


def _rmsnorm(x2, w, out_dtype):
    m, d = x2.shape
    tm = _largest_tile(m, 512, 16)
    return pl.pallas_call(
        _rmsnorm_kernel,
        grid=(m // tm,),
        in_specs=[pl.BlockSpec((tm, d), lambda i: (i, 0)),
                  pl.BlockSpec((1, d), lambda i: (0, 0))],
        out_specs=pl.BlockSpec((tm, d), lambda i: (i, 0)),
        out_shape=jax.ShapeDtypeStruct((m, d), out_dtype),
        compiler_params=_params("parallel"),
        name="rmsnorm",
    )(x2, w.reshape(1, d))


def _rmsnorm_gates_kernel(x_ref, w_ref, wg_ref, o_ref, ba_ref, wgb):
    @pl.when(pl.program_id(0) == 0)
    def _():
        _cast_rows(wg_ref.at[0], wgb)

    x = x_ref[...]
    ms = jnp.mean(x * x, axis=-1, keepdims=True)
    h = (x * lax.rsqrt(ms + EPS) * w_ref[...]).astype(o_ref.dtype)
    o_ref[...] = h
    ba_ref[...] = lax.dot_general(h, wgb[...], (((1,), (1,)), ((), ())),
                                  preferred_element_type=F32)


def _rmsnorm_gates(x2, w, w_t, layer, row0):
    m, d = x2.shape
    tm = _largest_tile(m, 512, 16)
    return pl.pallas_call(
        _rmsnorm_gates_kernel,
        grid=(m // tm,),
        in_specs=[pl.BlockSpec((tm, d), lambda i: (i, 0)),
                  pl.BlockSpec((1, d), lambda i: (0, 0)),
                  pl.BlockSpec((pl.Element(1), pl.Element(LANE), pl.Element(d)),
                               lambda i: (layer, row0, 0))],
        out_specs=[pl.BlockSpec((tm, d), lambda i: (i, 0)),
                   pl.BlockSpec((tm, LANE), lambda i: (i, 0))],
        out_shape=[jax.ShapeDtypeStruct((m, d), BF16),
                   jax.ShapeDtypeStruct((m, LANE), F32)],
        scratch_shapes=[pltpu.VMEM((LANE, d), BF16)],
        compiler_params=_params("arbitrary"),
        name="rmsnorm_gates",
    )(x2, w.reshape(1, d), w_t)


def _matmul_kernel(*refs, nk, has_res):
    if has_res:
        a_ref, w_ref, res_ref, o_ref = refs[:4]
    else:
        a_ref, w_ref, o_ref = refs[:3]
        res_ref = None
    part = jnp.dot(a_ref[...], w_ref[...], preferred_element_type=F32)

    def finish(acc):
        if res_ref is not None:
            acc = acc + res_ref[...]
        o_ref[...] = acc.astype(o_ref.dtype)

    if nk == 1:
        finish(part)
        return
    acc_ref = refs[-1]
    k = pl.program_id(2)

    @pl.when(k == 0)
    def _():
        acc_ref[...] = part

    @pl.when(k > 0)
    def _():
        acc_ref[...] += part

    @pl.when(k == nk - 1)
    def _():
        finish(acc_ref[...])


def _matmul(a, w, *, residual=None, out_dtype=F32, tm_cap=512, tn_cap=512, tk_cap=4096,
            name="matmul"):
    m, k = a.shape
    n = w.shape[1]
    tm = _largest_tile(m, tm_cap, 16)
    tn = _largest_tile(n, tn_cap, LANE)
    tk = _largest_tile(k, tk_cap, LANE)
    nk = k // tk
    in_specs = [pl.BlockSpec((tm, tk), lambda j, i, kk: (i, kk)),
                pl.BlockSpec((tk, tn), lambda j, i, kk: (kk, j))]
    args = [a, w]
    if residual is not None:
        in_specs.append(pl.BlockSpec((tm, tn), lambda j, i, kk: (i, j)))
        args.append(residual)
    scratch = [pltpu.VMEM((tm, tn), F32)] if nk > 1 else []
    return pl.pallas_call(
        functools.partial(_matmul_kernel, nk=nk, has_res=residual is not None),
        grid=(n // tn, m // tm, nk),
        in_specs=in_specs,
        out_specs=pl.BlockSpec((tm, tn), lambda j, i, kk: (i, j)),
        out_shape=jax.ShapeDtypeStruct((m, n), out_dtype),
        scratch_shapes=scratch,
        compiler_params=_params("parallel", "parallel", "arbitrary"),
        name=name,
    )(*args)


CAST_ROWS = 256
IN_PROJ_TILE = 1024
IN_PROJ_ROWS = 512


def _cast_rows(src_ref, dst_ref):
    step = min(CAST_ROWS, src_ref.shape[0])
    assert src_ref.shape[0] % step == 0

    def body(r, carry):
        rows = pl.ds(pl.multiple_of(r * step, step), step)
        dst_ref[rows, :] = src_ref[rows, :].astype(dst_ref.dtype)
        return carry

    lax.fori_loop(0, src_ref.shape[0] // step, body, 0)


def _weight_spec(w, layer, k, tn, col_block0=0):
    if layer is None:
        return pl.BlockSpec((k, tn), lambda j, i: (0, col_block0 + j))
    return pl.BlockSpec((None, k, tn), lambda j, i: (layer, 0, col_block0 + j))


def _out_proj_kernel(*refs, widths):
    a_refs = refs[:len(widths)]
    w_ref, res_ref, o_ref = refs[len(widths):]
    acc = res_ref[...]
    k0 = 0
    for a_ref, wk in zip(a_refs, widths):
        acc = acc + jnp.dot(a_ref[...], w_ref[k0:k0 + wk, :], preferred_element_type=F32)
        k0 += wk
    o_ref[...] = acc


def _out_proj_norm_kernel(*refs, widths):
    a_refs = refs[:len(widths)]
    w_ref, res_ref, nw_ref, o_ref, h_ref = refs[len(widths):]
    acc = res_ref[...]
    k0 = 0
    for a_ref, wk in zip(a_refs, widths):
        acc = acc + jnp.dot(a_ref[...], w_ref[k0:k0 + wk, :], preferred_element_type=F32)
        k0 += wk
    o_ref[...] = acc
    ms = jnp.mean(acc * acc, axis=-1, keepdims=True)
    h_ref[...] = (acc * lax.rsqrt(ms + EPS) * nw_ref[...]).astype(h_ref.dtype)


OUT_PROJ_ROWS = 256


def _out_proj_norm(a_parts, w, residual, norm_w, name):
    m = a_parts[0].shape[0]
    widths = tuple(a.shape[1] for a in a_parts)
    k, n = w.shape
    assert k == sum(widths)
    tm = _largest_tile(m, OUT_PROJ_ROWS, 16)
    in_specs = [pl.BlockSpec((tm, wk), lambda i: (i, 0)) for wk in widths]
    in_specs += [pl.BlockSpec((k, n), lambda i: (0, 0), pipeline_mode=pl.Buffered(1)),
                 pl.BlockSpec((tm, n), lambda i: (i, 0)),
                 pl.BlockSpec((1, n), lambda i: (0, 0))]
    return pl.pallas_call(
        functools.partial(_out_proj_norm_kernel, widths=widths),
        grid=(m // tm,),
        in_specs=in_specs,
        out_specs=[pl.BlockSpec((tm, n), lambda i: (i, 0)),
                   pl.BlockSpec((tm, n), lambda i: (i, 0))],
        out_shape=[jax.ShapeDtypeStruct((m, n), F32),
                   jax.ShapeDtypeStruct((m, n), BF16)],
        compiler_params=_params("arbitrary"),
        name=name,
    )(*a_parts, w, residual, norm_w.reshape(1, n))


def _out_proj(a_parts, w, residual, name):
    m = a_parts[0].shape[0]
    widths = tuple(a.shape[1] for a in a_parts)
    k, n = w.shape
    assert k == sum(widths)
    tm = _largest_tile(m, 1024, 16)
    tn = _largest_tile(n, 1024, LANE)
    in_specs = [pl.BlockSpec((tm, wk), lambda j, i: (i, 0)) for wk in widths]
    in_specs += [pl.BlockSpec((k, tn), lambda j, i: (0, j)),
                 pl.BlockSpec((tm, tn), lambda j, i: (i, j))]
    return pl.pallas_call(
        functools.partial(_out_proj_kernel, widths=widths),
        grid=(n // tn, m // tm),
        in_specs=in_specs,
        out_specs=pl.BlockSpec((tm, tn), lambda j, i: (i, j)),
        out_shape=jax.ShapeDtypeStruct((m, n), F32),
        compiler_params=_params("arbitrary", "arbitrary"),
        name=name,
    )(*a_parts, w, residual)


def _in_proj_kernel(a_ref, wt_ref, *rest, n_side):
    side_refs = rest[:n_side]
    o_ref = rest[n_side]
    side_out_refs = rest[n_side + 1:2 * n_side + 1]
    wb_ref = rest[-1]

    for src, dst in zip(side_refs, side_out_refs):
        dst[...] = src[...].astype(dst.dtype)

    @pl.when(pl.program_id(1) == 0)
    def _():
        _cast_rows(wt_ref.at[0], wb_ref)

    o_ref[...] = lax.dot_general(a_ref[...], wb_ref[...], (((1,), (1,)), ((), ())),
                                 preferred_element_type=F32)


def _in_proj(a, w_t, layer, row_start, n, tn, name, sides=()):
    m, k = a.shape
    assert n % tn == 0
    tm = _largest_tile(m, IN_PROJ_ROWS, 16)
    row_tiles = m // tm
    n_steps = (n // tn) * row_tiles
    in_specs = [pl.BlockSpec((tm, k), lambda j, i: (i, 0)),
                pl.BlockSpec((pl.Element(1), pl.Element(tn), pl.Element(k)),
                             lambda j, i: (layer, row_start(j), 0))]
    out_specs = [pl.BlockSpec((tm, tn), lambda j, i: (i, j))]
    out_shape = [jax.ShapeDtypeStruct((m, n), F32)]
    for side in sides:
        rows, cols = side.shape[1:]
        slab_rows = min(s for s in range(16, rows + 1, 16)
                        if rows % s == 0 and rows // s <= n_steps)

        def slab(j, i, n_slabs=rows // slab_rows):
            return jnp.minimum(j * row_tiles + i, n_slabs - 1)

        in_specs.append(pl.BlockSpec((None, slab_rows, cols),
                                     lambda j, i, slab=slab: (layer, slab(j, i), 0)))
        out_specs.append(pl.BlockSpec((slab_rows, cols),
                                      lambda j, i, slab=slab: (slab(j, i), 0)))
        out_shape.append(jax.ShapeDtypeStruct((rows, cols), BF16))
    return pl.pallas_call(
        functools.partial(_in_proj_kernel, n_side=len(sides)),
        grid=(n // tn, row_tiles),
        in_specs=in_specs,
        out_specs=out_specs,
        out_shape=out_shape,
        scratch_shapes=[pltpu.VMEM((tn, k), BF16)],
        compiler_params=_params("arbitrary", "arbitrary"),
        name=name,
    )(a, w_t, *sides)


FFN_ROW_TILE = 2048
FFN_ROW_BLOCK = 512


def _ffn_up_kernel(a_ref, wg_ref, wu_ref, cwg_ref, cwu_ref, bg_ref, bu_ref, o_ref,
                   wgb, wub, gbuf, ubuf, *, tm, rb, tiles_per_seq):
    i = pl.program_id(1)
    taps = cwg_ref.shape[0]

    @pl.when(i == 0)
    def _():
        _cast_rows(wg_ref, wgb)
        _cast_rows(wu_ref, wub)

    first = (i % tiles_per_seq) == 0
    for buf in (gbuf, ubuf):
        @pl.when(first)
        def _():
            buf[0:HALO, :] = jnp.zeros((HALO, buf.shape[1]), F32)

        @pl.when(jnp.logical_not(first))
        def _():
            buf[0:HALO, :] = buf[tm:tm + HALO, :]

    def conv(buf, cw_ref, r0):
        p = buf[pl.ds(r0, rb + HALO), :]
        acc = cw_ref[0:1, :] * p
        for j in range(1, taps):
            acc = cw_ref[j:j + 1, :] * p + pltpu.roll(acc, 1, 0)
        return acc[HALO:]

    for r in range(tm // rb):
        a = a_ref[r * rb:(r + 1) * rb, :]
        gbuf[HALO + r * rb:HALO + (r + 1) * rb, :] = jnp.dot(
            a, wgb[...], preferred_element_type=F32)
        ubuf[HALO + r * rb:HALO + (r + 1) * rb, :] = jnp.dot(
            a, wub[...], preferred_element_type=F32)
        g = conv(gbuf, cwg_ref, r * rb) + bg_ref[...]
        u = conv(ubuf, cwu_ref, r * rb) + bu_ref[...]
        o_ref[r * rb:(r + 1) * rb, :] = (_silu(g) * u).astype(o_ref.dtype)


def _ffn_up(h, w_up, layer, conv_w, conv_b, seq):
    m, k = h.shape
    f = w_up.shape[-1] // 2
    taps = conv_w.shape[0]
    tm = _largest_tile(seq, FFN_ROW_TILE, FFN_ROW_BLOCK)
    tf = _largest_tile(f, 256, LANE)
    nf = f // tf
    assert k % CAST_ROWS == 0
    conv_b = conv_b.reshape(1, 2 * f)
    return pl.pallas_call(
        functools.partial(_ffn_up_kernel, tm=tm, rb=FFN_ROW_BLOCK, tiles_per_seq=seq // tm),
        grid=(nf, m // tm),
        in_specs=[pl.BlockSpec((tm, k), lambda j, i: (i, 0)),
                  _weight_spec(w_up, layer, k, tf),
                  _weight_spec(w_up, layer, k, tf, nf),
                  pl.BlockSpec((taps, tf), lambda j, i: (0, j)),
                  pl.BlockSpec((taps, tf), lambda j, i: (0, nf + j)),
                  pl.BlockSpec((1, tf), lambda j, i: (0, j)),
                  pl.BlockSpec((1, tf), lambda j, i: (0, nf + j))],
        out_specs=pl.BlockSpec((tm, tf), lambda j, i: (i, j)),
        out_shape=jax.ShapeDtypeStruct((m, f), BF16),
        scratch_shapes=[pltpu.VMEM((k, tf), BF16),
                        pltpu.VMEM((k, tf), BF16),
                        pltpu.VMEM((tm + HALO, tf), F32),
                        pltpu.VMEM((tm + HALO, tf), F32)],
        compiler_params=_params("arbitrary", "arbitrary"),
        name="ffn_up_conv_gate",
    )(h, w_up, w_up, conv_w, conv_w, conv_b, conv_b)


def _load_with_halo(xpad, idx, src_ref, is_first, rows):
    width = xpad.shape[-1]

    @pl.when(is_first)
    def _():
        xpad[idx, 0:HALO, :] = jnp.zeros((HALO, width), F32)

    @pl.when(jnp.logical_not(is_first))
    def _():
        xpad[idx, 0:HALO, :] = xpad[idx, rows:rows + HALO, :]

    xpad[idx, HALO:, :] = src_ref[...]


def _short_conv(xpad, idx, cw_ref, r0, rows, cs=slice(None)):
    p = xpad[idx, pl.ds(r0, rows + HALO), cs]
    acc = cw_ref[0:1, cs] * p
    for j in range(1, cw_ref.shape[0]):
        acc = cw_ref[j:j + 1, cs] * p + pltpu.roll(acc, 1, 0)
    return acc[HALO:]


def _deltanet_kernel(q_ref, k_ref, v_ref, z_ref, ba_ref, cwq_ref, cwk_ref, cwv_ref,
                     gp_ref, nw_ref, o_ref, xpad, qkvs, gct, state, *, sc, hb, nb, n_heads):
    c = DN_CHUNK
    c2 = 2 * c
    ppb = hb // 2
    npairs = nb * ppb
    hblk = pl.program_id(0)
    is_first = pl.program_id(1) == 0

    @pl.when(is_first)
    def _():
        state[...] = jnp.zeros(state.shape, F32)

    for idx, src in enumerate((q_ref, k_ref, v_ref)):
        for b in range(nb):
            _load_with_halo(xpad, idx * nb + b, src.at[b], is_first, sc)

    for idx, cw_ref in enumerate((cwq_ref, cwk_ref, cwv_ref)):
        for b in range(nb):
            for ci in range(sc // c):
                for i in range(hb):
                    cs = slice(i * LANE, (i + 1) * LANE)
                    seg = _silu(_short_conv(xpad, idx * nb + b, cw_ref, ci * c, c, cs))
                    if idx < 2:
                        seg = seg * lax.rsqrt(
                            jnp.sum(seg * seg, axis=-1, keepdims=True) + EPS)
                    qkvs[idx, b * ppb + i // 2, ci, (i % 2) * c:(i % 2 + 1) * c, :] = seg

    row = _iota((c2, c2), 0)
    col = _iota((c2, c2), 1)
    same_head = (row >= c) == (col >= c)
    causal = (row >= col) & same_head
    strict = (row > col) & same_head
    second = (row >= c).astype(jnp.int32)
    left_cols = col < c
    eye = (row == col).astype(F32)
    tril = (_iota((c, c), 0) >= _iota((c, c), 1)).astype(F32)
    lane_row = _iota((1, c2), 1)
    rows1 = _iota((c2, 1), 0)
    rows2 = _iota((2 * c2, 1), 0)
    first_head_rows = (rows2 % c2) < c
    scale = HEAD_DIM ** -0.5
    pairs = range(npairs)

    def chunk_body(ci, carry):
        r0 = ci * c
        gc_stack, beta_stack = [], []
        for b in range(nb):
            ba = ba_ref[b, pl.ds(r0, c), :]
            beta_all = jax.nn.sigmoid(ba)
            g_all = -jnp.exp(gp_ref[0:1, :]) * _softplus(ba + gp_ref[1:2, :])
            gc_all = jnp.dot(tril, g_all, precision=HIGHEST,
                             preferred_element_type=F32)
            gc_stack.append(jnp.concatenate([gc_all, gc_all], axis=0))
            beta_stack.append(jnp.concatenate([beta_all, beta_all], axis=0))
            gct[b] = gc_stack[b].T
        bof = [u // ppb for u in pairs]
        h0 = [hblk * hb + 2 * (u % ppb) for u in pairs]
        bcol = [jnp.sum(jnp.where(col == h0[u] + second, beta_stack[bof[u]], 0.0), axis=-1,
                        keepdims=True) for u in pairs]
        gcol = [jnp.sum(jnp.where(col == h0[u] + n_heads + second, gc_stack[bof[u]], 0.0),
                        axis=-1, keepdims=True) for u in pairs]
        grow = [jnp.where(lane_row < c, gct[bof[u], pl.ds(h0[u] + n_heads, 1), :],
                          gct[bof[u], pl.ds(h0[u] + n_heads + 1, 1), :])
                for u in pairs]
        glast = [jnp.where(rows1 < c, g[c - 1:c, :], g[c2 - 1:c2, :]) for g in gcol]
        decay = [jnp.where(causal, jnp.exp(jnp.where(causal, gcol[p] - grow[p], 0.0)), 0.0)
                 for p in pairs]
        egc = [jnp.exp(g) for g in gcol]

        q = [qkvs[0, p, ci] * scale for p in pairs]
        k = [qkvs[1, p, ci] for p in pairs]
        v = [qkvs[2, p, ci] for p in pairs]
        kb = [k[p] * bcol[p] for p in pairs]
        kq = [_dot_nt(jnp.concatenate([kb[p], q[p]], axis=0), k[p]) for p in pairs]
        a_low = [jnp.where(strict, kq[p][:c2] * decay[p], 0.0) for p in pairs]
        attn = [kq[p][c2:] * decay[p] for p in pairs]

        t_inv = [eye - a for a in a_low]
        pw = a_low
        n = 2
        while n < c:
            pw = [_dot(x, x) for x in pw]
            t_inv = [t_inv[p] + _dot(t_inv[p], pw[p]) for p in pairs]
            n *= 2

        sol = [_dot(t_inv[p], jnp.concatenate([v[p] * bcol[p], kb[p] * egc[p]], axis=1))
               for p in pairs]
        k_dec = [k[p] * jnp.exp(glast[p] - gcol[p]) for p in pairs]
        s_prev = [state[p] for p in pairs]
        ws_qs = []
        for p in pairs:
            x = jnp.concatenate([sol[p][:, LANE:], q[p] * egc[p]], axis=0)
            x_wide = jnp.concatenate([jnp.where(first_head_rows, x, 0.0),
                                      jnp.where(first_head_rows, 0.0, x)], axis=1)
            ws_qs.append(_dot(x_wide, s_prev[p]))
        v_new = [sol[p][:, :LANE] - ws_qs[p][:c2] for p in pairs]
        o = [ws_qs[p][c2:] + _dot(attn[p], v_new[p]) for p in pairs]
        for p in pairs:
            kdt = k_dec[p].T
            kd_stack = jnp.concatenate([jnp.where(left_cols, kdt, 0.0),
                                        jnp.where(left_cols, 0.0, kdt)], axis=0)
            s_decay = jnp.where(rows2 < c2, jnp.exp(gcol[p][c - 1:c, :]),
                                jnp.exp(gcol[p][c2 - 1:c2, :]))
            state[p] = s_prev[p] * s_decay + _dot(kd_stack, v_new[p])
        for p in pairs:
            b, hp = bof[p], p % ppb
            cs0 = slice((2 * hp) * LANE, (2 * hp + 1) * LANE)
            cs1 = slice((2 * hp + 1) * LANE, (2 * hp + 2) * LANE)
            zg = _silu(jnp.concatenate([z_ref[b, pl.ds(r0, c), cs0],
                                        z_ref[b, pl.ds(r0, c), cs1]], axis=0))
            op = o[p] * lax.rsqrt(jnp.mean(o[p] * o[p], axis=-1, keepdims=True) + EPS)
            op = (op * nw_ref[...] * zg).astype(o_ref.dtype)
            o_ref[b, pl.ds(r0, c), cs0] = op[:c]
            o_ref[b, pl.ds(r0, c), cs1] = op[c:]
        return carry

    for ci in range(sc // c):
        chunk_body(ci, 0)


def _deltanet(proj3, ba3, conv_w, a_log, dt_bias, norm_w, n_heads, col0):
    b, s, _ = proj3.shape
    assert n_heads % 2 == 0 and 2 * n_heads <= LANE
    hb = _largest_tile(n_heads, 12, 2)
    width = hb * LANE
    nhb = n_heads // hb
    sc = _largest_tile(s, DN_SEQ_TILE, DN_CHUNK)
    base = col0 // width
    assert col0 % width == 0
    npairs = b * hb // 2
    gp = jnp.zeros((2, LANE), F32)
    gp = gp.at[0, n_heads:2 * n_heads].set(a_log).at[1, n_heads:2 * n_heads].set(dt_bias)

    def col_spec(group):
        return pl.BlockSpec((b, sc, width),
                            lambda hi, si, g=group: (0, si, base + g * nhb + hi))

    def cw_spec(group):
        return pl.BlockSpec((SHORT_CONV, width), lambda hi, si, g=group: (0, g * nhb + hi))

    return pl.pallas_call(
        functools.partial(_deltanet_kernel, sc=sc, hb=hb, nb=b, n_heads=n_heads),
        grid=(nhb, s // sc),
        in_specs=[col_spec(0), col_spec(1), col_spec(2), col_spec(3),
                  pl.BlockSpec((b, sc, LANE), lambda hi, si: (0, si, 0)),
                  cw_spec(0), cw_spec(1), cw_spec(2),
                  pl.BlockSpec((2, LANE), lambda hi, si: (0, 0)),
                  pl.BlockSpec((1, LANE), lambda hi, si: (0, 0))],
        out_specs=pl.BlockSpec((b, sc, width), lambda hi, si: (0, si, hi)),
        out_shape=jax.ShapeDtypeStruct((b, s, n_heads * LANE), BF16),
        scratch_shapes=[pltpu.VMEM((3 * b, sc + HALO, width), F32),
                        pltpu.VMEM((3, npairs, sc // DN_CHUNK, 2 * DN_CHUNK, LANE), F32),
                        pltpu.VMEM((b, LANE, LANE), F32),
                        pltpu.VMEM((npairs, 2 * HEAD_DIM, HEAD_DIM), F32)],
        compiler_params=_params("parallel", "arbitrary"),
        name="gated_deltanet",
    )(proj3, proj3, proj3, proj3, ba3, conv_w, conv_w, conv_w, gp, norm_w.reshape(1, LANE))


def _rglru_kernel(x_ref, g_ref, cw_ref, cb_ref, wa_ref, ba_ref, wx_ref, bx_ref, lam_ref,
                  nw_ref, o_ref, xpad, abuf, bbuf, hcarry, *, sc, gb):
    c = SCAN_CHUNK
    is_first = pl.program_id(2) == 0

    @pl.when(is_first)
    def _():
        hcarry[...] = jnp.zeros(hcarry.shape, F32)

    _load_with_halo(xpad, 0, x_ref, is_first, sc)

    rows = 2 * c
    for ci in range(sc // rows):
        xc = _short_conv(xpad, 0, cw_ref, ci * rows, rows) + cb_ref[...]
        for g in range(gb):
            cs = slice(g * LANE, (g + 1) * LANE)
            xg = xc[:, cs]
            r = jax.nn.sigmoid(_dot(xg, wa_ref[g]) + ba_ref[:, cs])
            ig = jax.nn.sigmoid(_dot(xg, wx_ref[g]) + bx_ref[:, cs])
            a = jnp.exp(-LRU_C * r * _softplus(-lam_ref[:, cs]))
            abuf[ci * rows:(ci + 1) * rows, cs] = a
            bbuf[ci * rows:(ci + 1) * rows, cs] = (
                jnp.sqrt((1.0 - a) * (1.0 + a)) * (ig * xg))

    row = _iota((HALO, LANE), 0)

    def chunk_body(ci, carry):
        r0 = pl.multiple_of(ci * c, c)
        for g in range(gb):
            cs = slice(g * LANE, (g + 1) * LANE)
            a = abuf[pl.ds(r0, c), cs]
            bb = bbuf[pl.ds(r0, c), cs]
            last = jnp.broadcast_to(hcarry[:, cs], (HALO, LANE))
            tiles = []
            for v in range(c // HALO):
                av = a[v * HALO:(v + 1) * HALO]
                bv = bb[v * HALO:(v + 1) * HALO]
                d = 1
                while d < HALO:
                    keep = row >= d
                    a_sh = jnp.where(keep, pltpu.roll(av, d, 0), 1.0)
                    b_sh = jnp.where(keep, pltpu.roll(bv, d, 0), 0.0)
                    bv = av * b_sh + bv
                    av = av * a_sh
                    d *= 2
                hv = bv + av * last
                last = jnp.broadcast_to(hv[HALO - 1:HALO, :], (HALO, LANE))
                tiles.append(hv)
            h = jnp.concatenate(tiles, axis=0)
            hcarry[:, cs] = last[0:1, :]
            y = h * jax.nn.gelu(g_ref[pl.ds(r0, c), cs])
            y = y * lax.rsqrt(jnp.mean(y * y, axis=-1, keepdims=True) + EPS)
            o_ref[pl.ds(r0, c), cs] = (y * nw_ref[:, cs]).astype(o_ref.dtype)
        return carry

    lax.fori_loop(0, sc // c, chunk_body, 0)


def _rglru(proj3, conv_w, conv_b, w_a, b_a, w_x, b_x, lam, norm_w, col_x, col_g):
    b, s, _ = proj3.shape
    nblk = w_a.shape[0]
    gb = 4 if nblk % 4 == 0 else (3 if nblk % 3 == 0 else 1)
    width = gb * LANE
    lw = nblk * LANE
    sc = _largest_tile(s, 512, 2 * SCAN_CHUNK)
    assert col_x % width == 0 and col_g % width == 0
    bx0, bg0 = col_x // width, col_g // width
    vec = lambda: pl.BlockSpec((1, width), lambda bi, gi, si: (0, gi))
    mat = lambda: pl.BlockSpec((gb, LANE, LANE), lambda bi, gi, si: (gi, 0, 0))
    return pl.pallas_call(
        functools.partial(_rglru_kernel, sc=sc, gb=gb),
        grid=(b, nblk // gb, s // sc),
        in_specs=[pl.BlockSpec((None, sc, width), lambda bi, gi, si: (bi, si, bx0 + gi)),
                  pl.BlockSpec((None, sc, width), lambda bi, gi, si: (bi, si, bg0 + gi)),
                  pl.BlockSpec((SHORT_CONV, width), lambda bi, gi, si: (0, gi)),
                  vec(), mat(), vec(), mat(), vec(), vec(), vec()],
        out_specs=pl.BlockSpec((None, sc, width), lambda bi, gi, si: (bi, si, gi)),
        out_shape=jax.ShapeDtypeStruct((b, s, lw), BF16),
        scratch_shapes=[pltpu.VMEM((1, sc + HALO, width), F32),
                        pltpu.VMEM((sc, width), F32),
                        pltpu.VMEM((sc, width), F32),
                        pltpu.VMEM((1, width), F32)],
        compiler_params=_params("parallel", "parallel", "arbitrary"),
        name="rglru",
    )(proj3, proj3, conv_w, conv_b.reshape(1, lw), w_a, b_a.reshape(1, lw), w_x,
      b_x.reshape(1, lw), lam.reshape(1, lw), norm_w.reshape(1, lw))


def _sg_kernel(u_ref, v_ref, lnw_ref, lnb_ref, ws_ref, bst_ref, nw_ref, o_ref, *, sc,
               groups):
    t = SG_CHUNK
    tril = _iota((t, t), 0) >= _iota((t, t), 1)
    for ci in range(sc // t):
        rs = slice(ci * t, (ci + 1) * t)
        v = jax.nn.gelu(v_ref[rs, :])
        mu = jnp.mean(v, axis=-1, keepdims=True)
        vc = v - mu
        var = jnp.mean(vc * vc, axis=-1, keepdims=True)
        vn = vc * lax.rsqrt(var + EPS) * lnw_ref[...] + lnb_ref[...]
        for g in range(groups):
            cs = slice(g * LANE, (g + 1) * LANE)
            w_causal = jnp.where(tril, ws_ref[g], 0.0)
            z = _dot(w_causal, vn[:, cs]) + bst_ref[:, g:g + 1]
            y = jax.nn.gelu(u_ref[rs, cs]) * z
            y = y * lax.rsqrt(jnp.mean(y * y, axis=-1, keepdims=True) + EPS)
            o_ref[rs, cs] = (y * nw_ref[:, cs]).astype(o_ref.dtype)


def _spatial_gating(proj3, ln_w, ln_b, w_s, b_s, norm_w, col_u, col_v):
    b, s, _ = proj3.shape
    groups = w_s.shape[0]
    width = groups * LANE
    sc = _largest_tile(s, 512, SG_CHUNK)
    assert col_u % width == 0 and col_v % width == 0
    bu0, bv0 = col_u // width, col_v // width
    vec = lambda: pl.BlockSpec((1, width), lambda bi, si: (0, 0))
    return pl.pallas_call(
        functools.partial(_sg_kernel, sc=sc, groups=groups),
        grid=(b, s // sc),
        in_specs=[pl.BlockSpec((None, sc, width), lambda bi, si: (bi, si, bu0)),
                  pl.BlockSpec((None, sc, width), lambda bi, si: (bi, si, bv0)),
                  vec(), vec(),
                  pl.BlockSpec((groups, SG_CHUNK, SG_CHUNK), lambda bi, si: (0, 0, 0)),
                  pl.BlockSpec((SG_CHUNK, groups), lambda bi, si: (0, 0)),
                  vec()],
        out_specs=pl.BlockSpec((None, sc, width), lambda bi, si: (bi, si, 0)),
        out_shape=jax.ShapeDtypeStruct((b, s, width), BF16),
        compiler_params=_params("parallel", "parallel"),
        name="spatial_gating",
    )(proj3, proj3, ln_w.reshape(1, width), ln_b.reshape(1, width), w_s, b_s.T,
      norm_w.reshape(1, width))


def kernel(x, norm_mix, w_in, dn_conv_w, dn_a_log, dn_dt_bias, dn_norm_w, lru_conv_w, lru_conv_b, lru_w_a, lru_b_a, lru_w_x, lru_b_x, lru_lambda, lru_norm_w, sg_ln_w, sg_ln_b, sg_w_s, sg_b_s, sg_norm_w, w_out, norm_ffn, w_up, ffn_conv_w, ffn_conv_b, w_down, norm_final):
    bsz, seq, d = x.shape
    depth = w_in.shape[0]
    n_heads = dn_a_log.shape[1]
    dn_w = n_heads * HEAD_DIM
    lru_w = lru_lambda.shape[1]
    sg_w = sg_ln_w.shape[1]
    m = bsz * seq
    gate0 = 4 * dn_w
    tail0 = gate0 + 2 * n_heads
    main_cols = gate0 + 2 * lru_w + 2 * sg_w
    tn = IN_PROJ_TILE
    assert gate0 % tn == 0 and tail0 % HALO == 0
    w_in_t = jnp.swapaxes(w_in, 1, 2)

    def main_rows(j):
        skip = jnp.where(j < gate0 // tn, 0, (tail0 - gate0) // HALO)
        return HALO * (j * (tn // HALO) + skip)

    x2 = x.reshape(m, d)
    for l in range(depth):
        h, ba = _rmsnorm_gates(x2, norm_mix[l], w_in_t, l, gate0)
        proj, w_out_bf16, w_down_bf16 = _in_proj(h, w_in_t, l, main_rows, main_cols, tn,
                                                 "in_proj", sides=(w_out, w_down))
        proj3 = proj.reshape(bsz, seq, main_cols)
        ba3 = ba.reshape(bsz, seq, LANE)

        y_a = _deltanet(proj3, ba3, dn_conv_w[l], dn_a_log[l], dn_dt_bias[l],
                        dn_norm_w[l], n_heads, 0)
        y_b = _rglru(proj3, lru_conv_w[l], lru_conv_b[l], lru_w_a[l], lru_b_a[l],
                     lru_w_x[l], lru_b_x[l], lru_lambda[l], lru_norm_w[l],
                     gate0, gate0 + lru_w)
        y_c = _spatial_gating(proj3, sg_ln_w[l], sg_ln_b[l], sg_w_s[l], sg_b_s[l],
                              sg_norm_w[l], gate0 + 2 * lru_w, gate0 + 2 * lru_w + sg_w)
        mix_parts = [y.reshape(m, y.shape[-1]) for y in (y_a, y_b, y_c)]
        x2, h = _out_proj_norm(mix_parts, w_out_bf16, x2, norm_ffn[l], "out_proj_norm")

        act = _ffn_up(h, w_up, l, ffn_conv_w[l], ffn_conv_b[l], seq)
        x2 = _matmul(act, w_down_bf16, residual=x2, tk_cap=w_down_bf16.shape[0],
                     name="down_proj")
    return _rmsnorm(x2, norm_final, F32).reshape(bsz, seq, d)
```
